```python
import math
import jax
import jax.numpy as jnp
from jax import lax
import numpy as np

D_MODEL = 2048
BATCH = 2
SEQ = 4096
DEPTH = 4
DEC_BATCH = 32
DEC_SEQ = 4
PAST_LEN = 16384
PAGE_SIZE = 128

PLE_DIM = 256
N_BUCKETS = 32
REL_MAX_DIST = 1024
RMS_EPS = 1e-6
D_FF = (((8 * D_MODEL + 2) // 3 + 255) // 256) * 256
QBLK = 128

HD_A = 64
H_A = D_MODEL // (2 * HD_A)
KVH_A = max(1, H_A // 8)
G_A = H_A // KVH_A
D_A = H_A * HD_A
WIN_A = 128
A_COLS = D_A + 2 * KVH_A * HD_A

HD_B = 64
C_B = D_MODEL - D_A
H_B = C_B // HD_B
LORA_W = 64
LORA_A = 64
LORA_G = 128
B_COLS = 3 * C_B + LORA_W + LORA_A + LORA_G
AB_COLS = A_COLS + B_COLS
GN_EPS = 64e-5

HD_C = 128
H_C = D_MODEL // HD_C
KVH_C = 2
G_C = H_C // KVH_C
D_C = H_C * HD_C
CMP_BLOCK = 64
SEL_BLOCK = CMP_BLOCK
N_TOP = 15
N_SEL = N_TOP + 1
WIN_C = 512
CMP_HIDDEN = 128
C_COLS = D_C + 6 * KVH_C * HD_C + 3 * H_C

N_AB = (DEPTH + 1) // 2
N_C = DEPTH // 2
NEG_INF = -1e30
FORCE_SCORE = 1e4

kernel_name = 'hybrid_swa_rwkv7_nsa_decoder_step'


def rms_norm(x, g):
    xf = x.astype(jnp.float32)
    y = xf * lax.rsqrt(jnp.mean(xf * xf, axis=-1, keepdims=True) + RMS_EPS)
    return (y * g.astype(jnp.float32)).astype(x.dtype)


def t5_bucket(dist):
    n = jnp.maximum(dist, 0)
    max_exact = N_BUCKETS // 2
    nf = jnp.maximum(n, max_exact).astype(jnp.float32)
    large = max_exact + (jnp.log(nf / max_exact) / math.log(REL_MAX_DIST / max_exact) * (N_BUCKETS - max_exact)).astype(jnp.int32)
    return jnp.where(n < max_exact, n, jnp.minimum(large, N_BUCKETS - 1))


def masked_softmax(logits, mask, sink=None):
    lf = jnp.where(mask, logits.astype(jnp.float32), NEG_INF)
    m = jnp.max(lf, axis=-1, keepdims=True)
    if sink is not None:
        m = jnp.maximum(m, sink)
    e = jnp.where(mask, jnp.exp(lf - m), 0.0)
    den = jnp.sum(e, axis=-1, keepdims=True)
    if sink is not None:
        den = den + jnp.exp(sink - m)
    return e / jnp.maximum(den, 1e-30)


def local_attention(q, k, v, q_pos, k_pos, window, rel_bias, sinks):
    b, n, nq, kvh, g, d = q.shape
    ns = k.shape[2]
    dist = q_pos[:, :, None] - k_pos[:, None, :]
    mask = (dist >= 0) & (dist <= window) & (k_pos[:, None, :] >= 0)
    bias = rel_bias.astype(jnp.float32)[t5_bucket(dist)]
    bias = bias.reshape(n, nq, ns, kvh, g).transpose(0, 3, 4, 1, 2)
    logits = jnp.einsum('bnqhgd,bnshd->bnhgqs', q, k).astype(jnp.float32) * (d ** -0.5) + bias[None]
    sink = None if sinks is None else sinks.astype(jnp.float32).reshape(kvh, g)[None, None, :, :, None, None]
    p = masked_softmax(logits, mask[None, :, None, None], sink)
    return jnp.einsum('bnhgqs,bnshd->bnqhgd', p.astype(v.dtype), v)


def banded_attention(q, k, v, window, rel_bias, sinks):
    b, t, kvh, g, d = q.shape
    nb = t // QBLK
    nw = -(-window // QBLK)

    def band(z):
        zb = z.reshape(b, nb, QBLK, kvh, d)
        zp = jnp.pad(zb, ((0, 0), (nw, 0), (0, 0), (0, 0), (0, 0)))
        return jnp.concatenate([zp[:, j:j + nb] for j in range(nw + 1)], axis=2)

    q_pos = jnp.arange(t, dtype=jnp.int32).reshape(nb, QBLK)
    k_pos = (jnp.arange(nb, dtype=jnp.int32)[:, None] - nw) * QBLK + jnp.arange((nw + 1) * QBLK, dtype=jnp.int32)[None, :]
    o = local_attention(q.reshape(b, nb, QBLK, kvh, g, d), band(k), band(v), q_pos, k_pos, window, rel_bias, sinks)
    return o.reshape(b, t, kvh, g, d)


def buffered_attention(q, kv_new, kv_buf, past_len, window, rel_bias, sinks):
    t = q.shape[1]
    wb = kv_buf.shape[1]
    kv_all = jnp.concatenate([kv_buf.astype(kv_new.dtype), kv_new], axis=1)
    q_pos = past_len + jnp.arange(t, dtype=jnp.int32)
    k_pos = past_len - wb + jnp.arange(wb + t, dtype=jnp.int32)
    o = local_attention(q[:, None], kv_all[:, None, :, 0], kv_all[:, None, :, 1], q_pos[None], k_pos[None], window, rel_bias, sinks)
    return o[:, 0], kv_all[:, t:]


def rwkv7_time_mix(P, shift0, S0, W, l):
    b, t, _ = P.shape
    f32 = jnp.float32
    prev = jnp.concatenate([shift0[:, None].astype(P.dtype), P[:, :-1]], axis=1)
    xs = P + (prev - P) * W['rwkv_mu'][l]
    o = 3 * C_B
    r, k, v = xs[..., :C_B], xs[..., C_B:2 * C_B], xs[..., 2 * C_B:o]
    wd = xs[..., o:o + LORA_W]
    ad = xs[..., o + LORA_W:o + LORA_W + LORA_A]
    gd = xs[..., o + LORA_W + LORA_A:]
    w_raw = (W['rwkv_w0'][l] + jnp.tanh(wd) @ W['rwkv_w2'][l]).astype(f32)
    decay = jnp.exp(-jnp.exp(-jax.nn.softplus(-w_raw) - 0.5))
    a = jax.nn.sigmoid(W['rwkv_a0'][l] + ad @ W['rwkv_a2'][l]).astype(f32)
    g = (jax.nn.sigmoid(gd) @ W['rwkv_g2'][l]).astype(f32)
    r, k, v = r.astype(f32), k.astype(f32), v.astype(f32)
    kk = (k * W['rwkv_k_k'][l]).reshape(b, t, H_B, HD_B)
    kk = kk / jnp.maximum(jnp.sqrt(jnp.sum(kk * kk, axis=-1, keepdims=True)), 1e-12)
    k = k * (1.0 + (a - 1.0) * W['rwkv_k_a'][l])
    heads = lambda z: z.reshape(b, t, H_B, HD_B)
    rh, kh, vh, dh, ah = heads(r), heads(k), heads(v), heads(decay), heads(a)

    def step(S, inp):
        r_t, d_t, k_t, v_t, kk_t, a_t = inp
        sa = jnp.einsum('bhvk,bhk->bhv', S, -kk_t)
        S = S * d_t[:, :, None, :] + sa[..., None] * (kk_t * a_t)[:, :, None, :] + v_t[..., None] * k_t[:, :, None, :]
        return S, jnp.einsum('bhvk,bhk->bhv', S, r_t)

    seq = tuple(jnp.moveaxis(z, 1, 0) for z in (rh, dh, kh, vh, kk, ah))
    S_fin, y = lax.scan(step, S0.astype(f32), seq)
    y = jnp.moveaxis(y, 0, 1)
    mean = jnp.mean(y, axis=-1, keepdims=True)
    var = jnp.mean(jnp.square(y - mean), axis=-1, keepdims=True)
    y = ((y - mean) * lax.rsqrt(var + GN_EPS)).reshape(b, t, C_B) * W['rwkv_ln_g'][l] + W['rwkv_ln_b'][l]
    bonus = jnp.sum(rh * kh * W['rwkv_r_k'][l], axis=-1, keepdims=True) * vh
    y = (y + bonus.reshape(b, t, C_B)) * g
    return y.astype(P.dtype), P[:, -1], S_fin


def nsa_compress(kv, pos_emb, w1, w2):
    b, L = kv.shape[:2]
    n = L // CMP_BLOCK
    blk = kv[:, :n * CMP_BLOCK].reshape(b, n, CMP_BLOCK, 2, KVH_C, HD_C)
    blk = blk + pos_emb.transpose(1, 0, 2)[:, :, None, :]
    flat = blk.transpose(0, 1, 3, 4, 2, 5).reshape(b, n, 2, KVH_C, CMP_BLOCK * HD_C)
    hdn = jax.nn.gelu(jnp.einsum('bnchf,cfe->bnche', flat, w1))
    return jnp.einsum('bnche,ced->bnchd', hdn, w2)


def make_row_gather(kv_rows):
    b, t = kv_rows.shape[:2]
    b_idx = jnp.arange(b)[:, None, None, None]
    h_idx = jnp.arange(KVH_C)[None, :, None, None]

    def gather(pos):
        kv = kv_rows[b_idx, jnp.clip(pos, 0, t - 1), :, h_idx]
        return kv[..., 0, :], kv[..., 1, :]
    return gather


def make_paged_gather(pool, page_table, kv_new, past_len):
    b, t = kv_new.shape[:2]
    b_idx = jnp.arange(b)[:, None, None, None]
    h_idx = jnp.arange(KVH_C)[None, :, None, None]

    def gather(pos):
        pp = jnp.clip(pos, 0, past_len - 1)
        phys = page_table[b_idx, pp // PAGE_SIZE]
        kv_p = pool[phys, pp % PAGE_SIZE, :, h_idx].astype(kv_new.dtype)
        kv_n = kv_new[b_idx, jnp.clip(pos - past_len, 0, t - 1), :, h_idx]
        kv = jnp.where((pos < past_len)[..., None, None], kv_p, kv_n)
        return kv[..., 0, :], kv[..., 1, :]
    return gather


def nsa_compressed_and_selected(q, q_pos0, kv_c, gather, rel_bias):
    b, t = q.shape[:2]
    qblk = QBLK if t % QBLK == 0 else t
    nb = t // qblk
    n_cmp = kv_c.shape[1]
    n_pad = max(n_cmp, N_TOP)
    k_c, v_c = kv_c[:, :, 0], kv_c[:, :, 1]
    blk_ids = jnp.arange(n_cmp, dtype=jnp.int32)
    table_g = rel_bias.astype(jnp.float32).reshape(N_BUCKETS, KVH_C, G_C).transpose(1, 0, 2)
    h_idx = jnp.arange(KVH_C)[None, :, None, None]
    offs = jnp.arange(SEL_BLOCK, dtype=jnp.int32)
    scale = HD_C ** -0.5

    def one_block(args):
        qb, qp = args
        cl = jnp.einsum('bqhgd,bnhd->bhgqn', qb, k_c).astype(jnp.float32) * scale
        cmask = (blk_ids[None, :] + 1) * CMP_BLOCK <= qp[:, None] + 1
        p_c = masked_softmax(cl, cmask)
        o_c = jnp.einsum('bhgqn,bnhd->bqhgd', p_c.astype(v_c.dtype), v_c)
        cur = qp // SEL_BLOCK
        imp = jnp.sum(p_c, axis=2) + jnp.where(blk_ids == 0, FORCE_SCORE, 0.0)
        score = jnp.where(blk_ids[None, :] < cur[:, None], imp, -1.0)
        score = jnp.pad(score, ((0, 0), (0, 0), (0, 0), (0, n_pad - n_cmp)), constant_values=-1.0)
        top_v, top_i = lax.top_k(score, N_TOP)
        sel = jnp.concatenate([top_i.astype(jnp.int32), jnp.broadcast_to(cur[None, None, :, None], (b, KVH_C, qblk, 1))], axis=-1)
        ok_blk = jnp.concatenate([top_v >= 0.0, jnp.ones((b, KVH_C, qblk, 1), dtype=bool)], axis=-1)
        pos = (sel[..., None] * SEL_BLOCK + offs).reshape(b, KVH_C, qblk, N_SEL * SEL_BLOCK)
        ok = jnp.repeat(ok_blk, SEL_BLOCK, axis=-1) & (pos <= qp[None, None, :, None])
        k_s, v_s = gather(pos)
        bias = table_g[h_idx, t5_bucket(qp[None, None, :, None] - pos)].transpose(0, 1, 4, 2, 3)
        sl = jnp.einsum('bqhgd,bhqsd->bhgqs', qb, k_s).astype(jnp.float32) * scale + bias
        p_s = masked_softmax(sl, ok[:, :, None])
        o_s = jnp.einsum('bhgqs,bhqsd->bqhgd', p_s.astype(v_s.dtype), v_s)
        return o_c, o_s

    qs = jnp.moveaxis(q.reshape(b, nb, qblk, KVH_C, G_C, HD_C), 1, 0)
    qps = (q_pos0 + jnp.arange(t, dtype=jnp.int32)).reshape(nb, qblk)
    o_c, o_s = lax.map(one_block, (qs, qps))
    unblock = lambda o: jnp.moveaxis(o, 0, 1).reshape(b, t, KVH_C, G_C, HD_C)
    return unblock(o_c), unblock(o_s)


def swiglu(h, wg, wu, wd):
    return (jax.nn.silu(h @ wg) * (h @ wu)) @ wd


def trunk(x, p, W, st):
    prompt = st is None
    b, t, _ = x.shape
    past_len = 0 if prompt else st['page_table'].shape[1] * PAGE_SIZE
    swa_l, shift_l, wkv_l, cmp_l, sel_l, win_l = [], [], [], [], [], []
    for i in range(DEPTH):
        l = i // 2
        h = rms_norm(x, W['mix_norm'][i])
        if i % 2 == 0:
            proj = h @ W['ab_w_in'][l]
            qkv = proj[..., :A_COLS] + W['ab_b_qkv'][l]
            q = qkv[..., :D_A].reshape(b, t, KVH_A, G_A, HD_A)
            kv = qkv[..., D_A:].reshape(b, t, 2, KVH_A, HD_A)
            if prompt:
                o_a = banded_attention(q, kv[:, :, 0], kv[:, :, 1], WIN_A, W['rel_bias'], W['swa_sinks'][l])
                buf = kv[:, -min(WIN_A, t):]
                shift0 = jnp.zeros((b, B_COLS), x.dtype)
                S0 = jnp.zeros((b, H_B, HD_B, HD_B), jnp.float32)
            else:
                o_a, buf = buffered_attention(q, kv, st['swa'][l], past_len, WIN_A, W['rel_bias'], W['swa_sinks'][l])
                shift0 = st['shift'][l]
                S0 = st['wkv'][l]
            y_b, shift_new, S_new = rwkv7_time_mix(proj[..., A_COLS:], shift0, S0, W, l)
            mix = jnp.concatenate([o_a.reshape(b, t, D_A), y_b], axis=-1) @ W['ab_w_out'][l]
            swa_l.append(buf)
            shift_l.append(shift_new)
            wkv_l.append(S_new)
        else:
            proj = h @ W['c_w_in'][l]
            q = proj[..., :D_C].reshape(b, t, KVH_C, G_C, HD_C)
            kv3 = proj[..., D_C:D_C + 6 * KVH_C * HD_C].reshape(b, t, 3, 2, KVH_C, HD_C)
            kv_cmp_new, kv_sel_new, kv_win_new = kv3[:, :, 0], kv3[:, :, 1], kv3[:, :, 2]
            gates = jax.nn.sigmoid(proj[..., D_C + 6 * KVH_C * HD_C:] + W['c_gate_b'][l]).reshape(b, t, 3, KVH_C, G_C, 1)
            if prompt:
                kv_cmp_full = kv_cmp_new
                gather = make_row_gather(kv_sel_new)
                o_w = banded_attention(q, kv_win_new[:, :, 0], kv_win_new[:, :, 1], WIN_C, W['rel_bias'], None)
                win_buf = kv_win_new[:, -min(WIN_C, t):]
            else:
                past = st['cmp'][l][st['page_table']].reshape(b, past_len, 2, KVH_C, HD_C)
                kv_cmp_full = jnp.concatenate([past.astype(kv_cmp_new.dtype), kv_cmp_new], axis=1)
                gather = make_paged_gather(st['sel'][l], st['page_table'], kv_sel_new, past_len)
                o_w, win_buf = buffered_attention(q, kv_win_new, st['win'][l], past_len, WIN_C, W['rel_bias'], None)
            kv_c = nsa_compress(kv_cmp_full, W['nsa_cmp_pos'][l], W['nsa_cmp_w1'][l], W['nsa_cmp_w2'][l])
            o_c, o_s = nsa_compressed_and_selected(q, past_len, kv_c, gather, W['rel_bias'])
            o = gates[:, :, 0] * o_c + gates[:, :, 1] * o_s + gates[:, :, 2] * o_w
            mix = o.reshape(b, t, D_C) @ W['c_w_out'][l]
            cmp_l.append(kv_cmp_new)
            sel_l.append(kv_sel_new)
            win_l.append(win_buf)
        x = x + mix
        x = x + swiglu(rms_norm(x, W['ffn_norm'][i]), W['ffn_w_gate'][i], W['ffn_w_up'][i], W['ffn_w_down'][i])
        e = p[i] @ W['ple_w_proj'][i]
        gate = jax.nn.sigmoid(rms_norm(x, W['ple_gate_norm'][i]) @ W['ple_w_gate'][i])
        x = x + rms_norm(gate * e, W['ple_post_norm'][i])
    y = rms_norm(x, W['final_norm'])
    return y, jnp.stack(swa_l), jnp.stack(shift_l), jnp.stack(wkv_l), jnp.stack(cmp_l), jnp.stack(sel_l), jnp.stack(win_l)


def setup_inputs(seed: int = 0) -> dict:
    key = jax.random.key(seed)
    ks = iter(jax.random.split(key, 64))
    f32 = jnp.float32

    def nrm(shape, scale=1.0):
        return jax.random.normal(next(ks), shape, f32) * scale

    def gain(shape):
        return 1.0 + nrm(shape, 0.05)

    n_pages = PAST_LEN // PAGE_SIZE
    n_used = DEC_BATCH * n_pages
    n_phys = n_used + n_used // 4
    wb_a = min(WIN_A, PAST_LEN)
    wb_c = min(WIN_C, PAST_LEN)
    page_table = jax.random.permutation(next(ks), n_phys)[:n_used].reshape(DEC_BATCH, n_pages).astype(jnp.int32)
    return {
        'x_prompt': nrm((BATCH, SEQ, D_MODEL)),
        'x_sample': nrm((DEC_BATCH, DEC_SEQ, D_MODEL)),
        'state_swa_kv': nrm((N_AB, DEC_BATCH, wb_a, 2, KVH_A, HD_A)),
        'state_rwkv_shift': nrm((N_AB, DEC_BATCH, B_COLS)),
        'state_rwkv_wkv': nrm((N_AB, DEC_BATCH, H_B, HD_B, HD_B), 0.3),
        'cache_nsa_cmp_kv': nrm((N_C, n_phys, PAGE_SIZE, 2, KVH_C, HD_C)),
        'cache_nsa_sel_kv': nrm((N_C, n_phys, PAGE_SIZE, 2, KVH_C, HD_C)),
        'state_nsa_win_kv': nrm((N_C, DEC_BATCH, wb_c, 2, KVH_C, HD_C)),
        'page_table': page_table,
        'p_prompt': nrm((DEPTH, BATCH, SEQ, PLE_DIM)),
        'p_sample': nrm((DEPTH, DEC_BATCH, DEC_SEQ, PLE_DIM)),
        'rel_bias': nrm((N_BUCKETS, H_A), 0.5),
        'mix_norm': gain((DEPTH, D_MODEL)),
        'ab_w_in': nrm((N_AB, D_MODEL, AB_COLS), D_MODEL ** -0.5),
        'ab_b_qkv': nrm((N_AB, A_COLS), 0.02),
        'swa_sinks': nrm((N_AB, H_A), 1.0),
        'rwkv_mu': jax.random.uniform(next(ks), (N_AB, B_COLS), f32),
        'rwkv_w0': nrm((N_AB, C_B), 0.5),
        'rwkv_w2': nrm((N_AB, LORA_W, C_B), LORA_W ** -0.5),
        'rwkv_a0': nrm((N_AB, C_B), 0.5),
        'rwkv_a2': nrm((N_AB, LORA_A, C_B), LORA_A ** -0.5),
        'rwkv_g2': nrm((N_AB, LORA_G, C_B), LORA_G ** -0.5),
        'rwkv_k_k': 0.85 + nrm((N_AB, C_B), 0.05),
        'rwkv_k_a': gain((N_AB, C_B)),
        'rwkv_r_k': nrm((N_AB, H_B, HD_B), 0.1),
        'rwkv_ln_g': gain((N_AB, C_B)),
        'rwkv_ln_b': nrm((N_AB, C_B), 0.02),
        'ab_w_out': nrm((N_AB, D_MODEL, D_MODEL), D_MODEL ** -0.5),
        'c_w_in': nrm((N_C, D_MODEL, C_COLS), D_MODEL ** -0.5),
        'c_gate_b': nrm((N_C, 3 * H_C), 0.1),
        'nsa_cmp_pos': nrm((N_C, 2, CMP_BLOCK, HD_C), 0.1),
        'nsa_cmp_w1': nrm((N_C, 2, CMP_BLOCK * HD_C, CMP_HIDDEN), (CMP_BLOCK * HD_C) ** -0.5),
        'nsa_cmp_w2': nrm((N_C, 2, CMP_HIDDEN, HD_C), CMP_HIDDEN ** -0.5),
        'c_w_out': nrm((N_C, D_MODEL, D_MODEL), D_MODEL ** -0.5),
        'ffn_norm': gain((DEPTH, D_MODEL)),
        'ffn_w_gate': nrm((DEPTH, D_MODEL, D_FF), D_MODEL ** -0.5),
        'ffn_w_up': nrm((DEPTH, D_MODEL, D_FF), D_MODEL ** -0.5),
        'ffn_w_down': nrm((DEPTH, D_FF, D_MODEL), D_FF ** -0.5),
        'ple_w_proj': nrm((DEPTH, PLE_DIM, D_MODEL), PLE_DIM ** -0.5),
        'ple_gate_norm': gain((DEPTH, D_MODEL)),
        'ple_w_gate': nrm((DEPTH, D_MODEL, D_MODEL), D_MODEL ** -0.5),
        'ple_post_norm': gain((DEPTH, D_MODEL)),
        'final_norm': gain((D_MODEL,)),
    }


def reference(x_prompt, x_sample, state_swa_kv, state_rwkv_shift, state_rwkv_wkv, cache_nsa_cmp_kv, cache_nsa_sel_kv, state_nsa_win_kv, page_table, p_prompt, p_sample, rel_bias, mix_norm, ab_w_in, ab_b_qkv, swa_sinks, rwkv_mu, rwkv_w0, rwkv_w2, rwkv_a0, rwkv_a2, rwkv_g2, rwkv_k_k, rwkv_k_a, rwkv_r_k, rwkv_ln_g, rwkv_ln_b, ab_w_out, c_w_in, c_gate_b, nsa_cmp_pos, nsa_cmp_w1, nsa_cmp_w2, c_w_out, ffn_norm, ffn_w_gate, ffn_w_up, ffn_w_down, ple_w_proj, ple_gate_norm, ple_w_gate, ple_post_norm, final_norm):
    W = dict(rel_bias=rel_bias, mix_norm=mix_norm, ab_w_in=ab_w_in, ab_b_qkv=ab_b_qkv, swa_sinks=swa_sinks,
             rwkv_mu=rwkv_mu, rwkv_w0=rwkv_w0, rwkv_w2=rwkv_w2, rwkv_a0=rwkv_a0, rwkv_a2=rwkv_a2, rwkv_g2=rwkv_g2,
             rwkv_k_k=rwkv_k_k, rwkv_k_a=rwkv_k_a, rwkv_r_k=rwkv_r_k, rwkv_ln_g=rwkv_ln_g, rwkv_ln_b=rwkv_ln_b,
             ab_w_out=ab_w_out, c_w_in=c_w_in, c_gate_b=c_gate_b, nsa_cmp_pos=nsa_cmp_pos, nsa_cmp_w1=nsa_cmp_w1,
             nsa_cmp_w2=nsa_cmp_w2, c_w_out=c_w_out, ffn_norm=ffn_norm, ffn_w_gate=ffn_w_gate, ffn_w_up=ffn_w_up,
             ffn_w_down=ffn_w_down, ple_w_proj=ple_w_proj, ple_gate_norm=ple_gate_norm, ple_w_gate=ple_w_gate,
             ple_post_norm=ple_post_norm, final_norm=final_norm)
    st = dict(swa=state_swa_kv, shift=state_rwkv_shift, wkv=state_rwkv_wkv, cmp=cache_nsa_cmp_kv,
              sel=cache_nsa_sel_kv, win=state_nsa_win_kv, page_table=page_table)
    y_p, swa_p, shift_p, wkv_p, cmp_p, sel_p, win_p = trunk(x_prompt, p_prompt, W, None)
    y_s, swa_s, shift_s, wkv_s, cmp_s, sel_s, win_s = trunk(x_sample, p_sample, W, st)
    return (y_p, y_s, swa_p, swa_s, shift_p, shift_s, wkv_p, wkv_s, cmp_p, cmp_s, sel_p, sel_s, win_p, win_s)
```

```python
import functools
import math

import jax
import jax.numpy as jnp
import numpy as np
from jax import lax
from jax.experimental import pallas as pl
from jax.experimental.pallas import tpu as pltpu

F32 = jnp.float32
BF16 = jnp.bfloat16

D_MODEL = 2048
DEPTH = 4
PAGE_SIZE = 128
PLE_DIM = 256
N_BUCKETS = 32
REL_MAX_DIST = 1024
RMS_EPS = 1e-6
D_FF = 5632
QBLK = 128
HD_A = 64
H_A = 16
KVH_A = 2
G_A = 8
D_A = 1024
WIN_A = 128
A_COLS = D_A + 2 * KVH_A * HD_A
HD_B = 64
C_B = 1024
H_B = 16
LORA_W = 64
LORA_A = 64
LORA_G = 128
B_COLS = 3 * C_B + LORA_W + LORA_A + LORA_G
AB_COLS = A_COLS + B_COLS
GN_EPS = 64e-5
HD_C = 128
H_C = 16
KVH_C = 2
G_C = 8
D_C = 2048
CMP_BLOCK = 64
SEL_BLOCK = 64
SEL_SHIFT = 6
N_TOP = 15
WIN_C = 512
CMP_HIDDEN = 128
C_KV_COLS = 6 * KVH_C * HD_C
C_COLS = D_C + C_KV_COLS + 3 * H_C
NEG_INF = -1e30
FORCE_SCORE = 1e4

V7X_LANES = 128
V7X_SUBLANES = 8
V7X_VMEM_BYTES = 64 * 1024 * 1024
VMEM_LIMIT_CAP = V7X_VMEM_BYTES - 8 * 1024 * 1024

C_COLS_PAD = 4096
RWKV_CHUNK = 64


def _cparams(sem, vmem_bytes):
    limit = int(min(max(2 * vmem_bytes, 32 * 1024 * 1024), VMEM_LIMIT_CAP))
    return pltpu.CompilerParams(dimension_semantics=sem, vmem_limit_bytes=limit)


def _row_tile(m, cap):
    t = min(m, cap)
    assert m % t == 0, (m, t)
    return t


def _rms(x, g):
    return x * lax.rsqrt(jnp.mean(x * x, axis=-1, keepdims=True) + RMS_EPS) * g


def t5_bucket(dist):
    n = jnp.maximum(dist, 0)
    max_exact = N_BUCKETS // 2
    nf = jnp.maximum(n, max_exact).astype(F32)
    large = max_exact + (jnp.log(nf / max_exact) / math.log(REL_MAX_DIST / max_exact) * (N_BUCKETS - max_exact)).astype(jnp.int32)
    return jnp.where(n < max_exact, n, jnp.minimum(large, N_BUCKETS - 1))


def _mm_norm_kernel(x_ref, g_ref, w_ref, b_ref, o_ref, h_ref):
    @pl.when(pl.program_id(1) == 0)
    def _():
        h_ref[...] = _rms(x_ref[...], g_ref[...]).astype(BF16)

    o_ref[...] = jnp.dot(h_ref[...], w_ref[...], preferred_element_type=F32) + b_ref[...]


def mm_norm(x, g, w, b, *, tn=512):
    m, k = x.shape
    n = w.shape[1]
    tm = _row_tile(m, 1024)
    vmem = 2 * tm * k * 4 + tm * k * 2 + 2 * k * tn * 2 + 2 * tm * tn * 4
    return pl.pallas_call(
        _mm_norm_kernel,
        out_shape=jax.ShapeDtypeStruct((m, n), F32),
        grid=(m // tm, n // tn),
        in_specs=[pl.BlockSpec((tm, k), lambda i, j: (i, 0)),
                  pl.BlockSpec((1, k), lambda i, j: (0, 0)),
                  pl.BlockSpec((k, tn), lambda i, j: (0, j)),
                  pl.BlockSpec((1, tn), lambda i, j: (0, j))],
        out_specs=pl.BlockSpec((tm, tn), lambda i, j: (i, j)),
        scratch_shapes=[pltpu.VMEM((tm, k), BF16)],
        compiler_params=_cparams(("parallel", "arbitrary"), vmem),
        name="mm_norm",
    )(x, g, w, b)


def _mm_res_kernel(a_ref, w_ref, r_ref, o_ref, acc_ref, *, nk):
    kk = pl.program_id(2)

    @pl.when(kk == 0)
    def _():
        acc_ref[...] = jnp.zeros_like(acc_ref)

    acc_ref[...] += jnp.dot(a_ref[...], w_ref[...], preferred_element_type=F32)

    @pl.when(kk == nk - 1)
    def _():
        o_ref[...] = r_ref[...] + acc_ref[...]


def mm_res(a, w, r, *, tn=512, tk=None):
    m, k = a.shape
    n = w.shape[1]
    tm = _row_tile(m, 1024)
    tk = k if tk is None else tk
    nk = k // tk
    vmem = 2 * tm * tk * 2 + 2 * tk * tn * 2 + 5 * tm * tn * 4
    return pl.pallas_call(
        functools.partial(_mm_res_kernel, nk=nk),
        out_shape=jax.ShapeDtypeStruct((m, n), F32),
        grid=(m // tm, n // tn, nk),
        in_specs=[pl.BlockSpec((tm, tk), lambda i, j, q: (i, q)),
                  pl.BlockSpec((tk, tn), lambda i, j, q: (q, j)),
                  pl.BlockSpec((tm, tn), lambda i, j, q: (i, j))],
        out_specs=pl.BlockSpec((tm, tn), lambda i, j, q: (i, j)),
        scratch_shapes=[pltpu.VMEM((tm, tn), F32)],
        compiler_params=_cparams(("parallel", "parallel", "arbitrary"), vmem),
        name="mm_res",
    )(a, w, r)


def _ffn_up_kernel(x_ref, g_ref, wg_ref, wu_ref, o_ref, h_ref):
    @pl.when(pl.program_id(1) == 0)
    def _():
        h_ref[...] = _rms(x_ref[...], g_ref[...]).astype(BF16)

    h = h_ref[...]
    gate = jnp.dot(h, wg_ref[...], preferred_element_type=F32)
    up = jnp.dot(h, wu_ref[...], preferred_element_type=F32)
    o_ref[...] = (jax.nn.silu(gate) * up).astype(BF16)


def ffn_up(x, g, wg, wu, *, tn=512):
    m, k = x.shape
    n = wg.shape[1]
    tm = _row_tile(m, 1024)
    vmem = 2 * tm * k * 4 + tm * k * 2 + 4 * k * tn * 2 + 2 * tm * tn * 2 + 3 * tm * tn * 4
    return pl.pallas_call(
        _ffn_up_kernel,
        out_shape=jax.ShapeDtypeStruct((m, n), BF16),
        grid=(m // tm, n // tn),
        in_specs=[pl.BlockSpec((tm, k), lambda i, j: (i, 0)),
                  pl.BlockSpec((1, k), lambda i, j: (0, 0)),
                  pl.BlockSpec((k, tn), lambda i, j: (0, j)),
                  pl.BlockSpec((k, tn), lambda i, j: (0, j))],
        out_specs=pl.BlockSpec((tm, tn), lambda i, j: (i, j)),
        scratch_shapes=[pltpu.VMEM((tm, k), BF16)],
        compiler_params=_cparams(("parallel", "arbitrary"), vmem),
        name="ffn_up",
    )(x, g, wg, wu)


def _ple_kernel(x_ref, p_ref, gn_ref, wg_ref, wp_ref, pn_ref, fn_ref, o_ref, *, final):
    x = x_ref[...]
    h = _rms(x, gn_ref[...]).astype(BF16)
    gate = jax.nn.sigmoid(jnp.dot(h, wg_ref[...], preferred_element_type=F32))
    e = jnp.dot(p_ref[...].astype(BF16), wp_ref[...], preferred_element_type=F32)
    x = x + _rms(gate * e, pn_ref[...])
    if final:
        x = _rms(x, fn_ref[...])
    o_ref[...] = x


def ple(x, p, gn, wg, wp, pn, fn, *, final):
    m, d = x.shape
    tm = _row_tile(m, 512)
    vmem = 4 * tm * d * 4 + 2 * d * d * 2 + 2 * PLE_DIM * d * 2 + 4 * tm * d * 4
    row = lambda i: (i, 0)
    fix = lambda i: (0, 0)
    return pl.pallas_call(
        functools.partial(_ple_kernel, final=final),
        out_shape=jax.ShapeDtypeStruct((m, d), F32),
        grid=(m // tm,),
        in_specs=[pl.BlockSpec((tm, d), row), pl.BlockSpec((tm, PLE_DIM), row),
                  pl.BlockSpec((1, d), fix), pl.BlockSpec((d, d), fix),
                  pl.BlockSpec((PLE_DIM, d), fix), pl.BlockSpec((1, d), fix),
                  pl.BlockSpec((1, d), fix)],
        out_specs=pl.BlockSpec((tm, d), row),
        compiler_params=_cparams(("parallel",), vmem),
        name="ple",
    )(x, p, gn, wg, wp, pn, fn)


def _masked_softmax(logits, mask, sink=None):
    lf = jnp.where(mask, logits, NEG_INF)
    m = jnp.max(lf, axis=-1, keepdims=True)
    if sink is not None:
        m = jnp.maximum(m, sink)
    e = jnp.where(mask, jnp.exp(lf - m), 0.0)
    den = jnp.sum(e, axis=-1, keepdims=True)
    if sink is not None:
        den = den + jnp.exp(sink - m)
    return e / jnp.maximum(den, 1e-30)


def _stack_heads(q, h0, g, hd):
    return jnp.concatenate([q[:, (h0 + j) * hd:(h0 + j + 1) * hd] for j in range(g)], axis=0)


def _band_attn_kernel(q_ref, k_ref, v_ref, bias_ref, sink_ref, o_ref, *,
                      tq, sk, hd, n_kvh, g, window, delta, kpos_base, kstride, has_sink):
    ks = pl.multiple_of(pl.program_id(2) * kstride, V7X_SUBLANES)
    kslab = k_ref[pl.ds(ks, sk), :]
    vslab = v_ref[pl.ds(ks, sk), :]
    r = lax.broadcasted_iota(jnp.int32, (tq, sk), 0)
    c = lax.broadcasted_iota(jnp.int32, (tq, sk), 1)
    dist = delta + r - c
    mask = ((dist >= 0) & (dist <= window) & (kpos_base + ks + c >= 0))[None]
    scale = hd ** -0.5
    q = q_ref[...]
    for h in range(n_kvh):
        q8 = _stack_heads(q, h * g, g, hd).astype(BF16)
        kh = kslab[:, h * hd:(h + 1) * hd].astype(BF16)
        vh = vslab[:, h * hd:(h + 1) * hd].astype(BF16)
        s = lax.dot_general(q8, kh, (((1,), (1,)), ((), ())), preferred_element_type=F32) * scale
        s = s.reshape(g, tq, sk) + bias_ref[h * g:(h + 1) * g]
        sink = sink_ref[h * g:(h + 1) * g] if has_sink else None
        p = _masked_softmax(s, mask, sink)
        o = jnp.dot(p.reshape(g * tq, sk).astype(BF16), vh, preferred_element_type=F32)
        for j in range(g):
            o_ref[:, (h * g + j) * hd:(h * g + j + 1) * hd] = o[j * tq:(j + 1) * tq].astype(o_ref.dtype)


def band_attn(q_arr, k_arr, v_arr, bias, sink, *, tq, sk, hd, n_kvh_step, g, window, delta,
              kpos_base, kstride, out_dtype):
    b, t = q_arr.shape[:2]
    tk = k_arr.shape[1]
    n_kv_blocks = k_arr.shape[2] // (n_kvh_step * hd)
    qw = n_kvh_step * g * hd
    has_sink = sink is not None
    if not has_sink:
        sink = jnp.zeros((n_kv_blocks * n_kvh_step * g, 1, 1), F32)
    vmem = 2 * tq * qw * 4 * 2 + 4 * tk * n_kvh_step * hd * 4 + 2 * n_kvh_step * g * tq * sk * 4 + 6 * g * tq * sk * 4
    kern = functools.partial(_band_attn_kernel, tq=tq, sk=sk, hd=hd, n_kvh=n_kvh_step, g=g, window=window,
                             delta=delta, kpos_base=kpos_base, kstride=kstride, has_sink=has_sink)
    return pl.pallas_call(
        kern,
        out_shape=jax.ShapeDtypeStruct((b, t, n_kv_blocks * qw), out_dtype),
        grid=(b, n_kv_blocks, t // tq),
        in_specs=[pl.BlockSpec((None, tq, qw), lambda bb, kv, i: (bb, i, kv)),
                  pl.BlockSpec((None, tk, n_kvh_step * hd), lambda bb, kv, i: (bb, 0, kv)),
                  pl.BlockSpec((None, tk, n_kvh_step * hd), lambda bb, kv, i: (bb, 0, kv)),
                  pl.BlockSpec((n_kvh_step * g, tq, sk), lambda bb, kv, i: (kv, 0, 0)),
                  pl.BlockSpec((n_kvh_step * g, 1, 1), lambda bb, kv, i: (kv, 0, 0))],
        out_specs=pl.BlockSpec((None, tq, qw), lambda bb, kv, i: (bb, i, kv)),
        compiler_params=_cparams(("parallel", "parallel", "arbitrary"), vmem),
        name="band_attn",
    )(q_arr, k_arr, v_arr, bias, sink)


def _toeplitz_bias(rel_bias, tq, sk, delta):
    dist = delta + jnp.arange(tq, dtype=jnp.int32)[:, None] - jnp.arange(sk, dtype=jnp.int32)[None, :]
    return rel_bias.astype(F32)[t5_bucket(dist)].transpose(2, 0, 1)


def _nsa_cmp_kernel(q_ref, kc_ref, vc_ref, oc_ref, sel_ref, *, tq, n_cmp, g, q0):
    hd = HD_C
    qp = q0 + pl.program_id(2) * tq + lax.broadcasted_iota(jnp.int32, (tq, n_cmp), 0)
    blk = lax.broadcasted_iota(jnp.int32, (tq, n_cmp), 1)
    q8 = _stack_heads(q_ref[...], 0, g, hd)
    kc = kc_ref[...]
    vc = vc_ref[...].astype(BF16)
    q_hi = q8.astype(BF16)
    q_lo = (q8 - q_hi.astype(F32)).astype(BF16)
    k_hi = kc.astype(BF16)
    k_lo = (kc - k_hi.astype(F32)).astype(BF16)
    cl = (lax.dot_general(q_hi, k_hi, _NT, preferred_element_type=F32)
          + lax.dot_general(q_hi, k_lo, _NT, preferred_element_type=F32)
          + lax.dot_general(q_lo, k_hi, _NT, preferred_element_type=F32)) * (hd ** -0.5)
    cmask = (blk + 1) * CMP_BLOCK <= qp + 1
    p_c = _masked_softmax(cl.reshape(g, tq, n_cmp), cmask[None])
    o = jnp.dot(p_c.reshape(g * tq, n_cmp).astype(BF16), vc, preferred_element_type=F32)
    for j in range(g):
        oc_ref[:, j * hd:(j + 1) * hd] = o[j * tq:(j + 1) * tq]
    cur = qp >> SEL_SHIFT
    imp = jnp.sum(p_c, axis=0) + jnp.where(blk == 0, FORCE_SCORE, 0.0)
    score = jnp.where(blk < cur, imp, -1.0)
    sel = jnp.where(blk == cur, 1.0, 0.0)
    for _ in range(N_TOP):
        m = jnp.max(score, axis=-1, keepdims=True)
        idx = jnp.min(jnp.where(score == m, blk, n_cmp), axis=-1, keepdims=True)
        hit = blk == idx
        sel = jnp.where(hit & (m >= 0.0), 1.0, sel)
        score = jnp.where(hit, -2.0, score)
    sel_ref[...] = sel


def nsa_cmp(q_arr, kv_c, *, tq, q0):
    b, t = q_arr.shape[:2]
    n_cmp = kv_c.shape[1]
    assert n_cmp >= N_TOP
    qw = G_C * HD_C
    vmem = 4 * tq * qw * 4 + 4 * n_cmp * HD_C * 4 + 8 * G_C * tq * n_cmp * 4
    return pl.pallas_call(
        functools.partial(_nsa_cmp_kernel, tq=tq, n_cmp=n_cmp, g=G_C, q0=q0),
        out_shape=(jax.ShapeDtypeStruct((b, t, D_C), F32),
                   jax.ShapeDtypeStruct((b, KVH_C, t, n_cmp), F32)),
        grid=(b, KVH_C, t // tq),
        in_specs=[pl.BlockSpec((None, tq, qw), lambda bb, kv, i: (bb, i, kv)),
                  pl.BlockSpec((None, n_cmp, HD_C), lambda bb, kv, i: (bb, 0, kv)),
                  pl.BlockSpec((None, n_cmp, HD_C), lambda bb, kv, i: (bb, 0, KVH_C + kv))],
        out_specs=(pl.BlockSpec((None, tq, qw), lambda bb, kv, i: (bb, i, kv)),
                   pl.BlockSpec((None, None, tq, n_cmp), lambda bb, kv, i: (bb, kv, i, 0))),
        compiler_params=_cparams(("parallel", "parallel", "arbitrary"), vmem),
        name="nsa_cmp",
    )(q_arr, kv_c, kv_c)


def _nsa_sel_kernel(q_ref, k_ref, v_ref, sel_ref, bias_ref, o_ref, selk_ref, m_ref, l_ref, acc_ref, *,
                    tq, t, g, n_far):
    hd = HD_C
    i = pl.program_id(2)
    n_blk = t // SEL_BLOCK
    eb = lax.broadcasted_iota(jnp.int32, (n_blk, t), 0)
    ek = lax.broadcasted_iota(jnp.int32, (n_blk, t), 1)
    expand = jnp.where((ek >> SEL_SHIFT) == eb, 1.0, 0.0).astype(BF16)
    selk_ref[...] = jnp.dot(sel_ref[...].astype(BF16), expand, preferred_element_type=F32)
    q8 = _stack_heads(q_ref[...], 0, g, hd).astype(BF16)
    m_ref[...] = jnp.full(m_ref.shape, NEG_INF, F32)
    l_ref[...] = jnp.zeros(l_ref.shape, F32)
    acc_ref[...] = jnp.zeros(acc_ref.shape, F32)
    r = lax.broadcasted_iota(jnp.int32, (tq, tq), 0)
    c = lax.broadcasted_iota(jnp.int32, (tq, tq), 1)
    scale = hd ** -0.5

    def body(j, carry):
        ks = pl.multiple_of(j * tq, tq)
        kj = k_ref[pl.ds(ks, tq), :].astype(BF16)
        vj = v_ref[pl.ds(ks, tq), :].astype(BF16)
        s = lax.dot_general(q8, kj, (((1,), (1,)), ((), ())), preferred_element_type=F32) * scale
        s = s.reshape(g, tq, tq) + bias_ref[jnp.minimum(i - j, n_far)]
        mask = ((selk_ref[:, pl.ds(ks, tq)] > 0.5) & ((j - i) * tq + c <= r))[None]
        lf = jnp.where(mask, s, NEG_INF)
        m_old = m_ref[...]
        m_new = jnp.maximum(m_old, jnp.max(lf, axis=-1, keepdims=True))
        e = jnp.where(mask, jnp.exp(lf - m_new), 0.0)
        alpha = jnp.exp(m_old - m_new)
        l_ref[...] = alpha * l_ref[...] + jnp.sum(e, axis=-1, keepdims=True)
        pv = jnp.dot(e.reshape(g * tq, tq).astype(BF16), vj, preferred_element_type=F32)
        acc_ref[...] = alpha.reshape(g * tq, 1) * acc_ref[...] + pv
        m_ref[...] = m_new
        return carry

    lax.fori_loop(0, i + 1, body, 0)
    o = acc_ref[...] / jnp.maximum(l_ref[...].reshape(g * tq, 1), 1e-30)
    for j in range(g):
        o_ref[:, j * hd:(j + 1) * hd] = o[j * tq:(j + 1) * tq]


def _bias_saturation_offset(tq, t):
    d = np.arange(0, t + tq, dtype=np.float64)
    nf = np.maximum(d, N_BUCKETS // 2)
    large = N_BUCKETS // 2 + np.floor(np.log(nf / (N_BUCKETS // 2)) / math.log(REL_MAX_DIST / (N_BUCKETS // 2))
                                      * (N_BUCKETS - N_BUCKETS // 2) - 1e-3)
    saturated = np.where(d < N_BUCKETS // 2, 0, large) >= N_BUCKETS - 1
    if not saturated.any():
        return t // tq
    first_sat = int(np.argmax(saturated))
    return min(t // tq, -(-(first_sat + tq) // tq))


def nsa_sel_prompt(q_arr, k_arr, v_arr, sel, rel_bias, *, tq):
    b, t = q_arr.shape[:2]
    n_blk = t // SEL_BLOCK
    n_far = _bias_saturation_offset(tq, t)
    tiles = jnp.stack([_toeplitz_bias(rel_bias, tq, tq, o * tq) for o in range(n_far + 1)], axis=0)
    tiles = tiles.reshape(n_far + 1, KVH_C, G_C, tq, tq).transpose(1, 0, 2, 3, 4)
    qw = G_C * HD_C
    vmem = 4 * tq * qw * 4 + 4 * t * HD_C * 4 + 2 * (n_far + 1) * G_C * tq * tq * 4 + tq * t * 4 + 8 * G_C * tq * tq * 4 + n_blk * t * 4
    return pl.pallas_call(
        functools.partial(_nsa_sel_kernel, tq=tq, t=t, g=G_C, n_far=n_far),
        out_shape=jax.ShapeDtypeStruct((b, t, D_C), F32),
        grid=(b, KVH_C, t // tq),
        in_specs=[pl.BlockSpec((None, tq, qw), lambda bb, kv, i: (bb, i, kv)),
                  pl.BlockSpec((None, t, HD_C), lambda bb, kv, i: (bb, 0, kv)),
                  pl.BlockSpec((None, t, HD_C), lambda bb, kv, i: (bb, 0, kv)),
                  pl.BlockSpec((None, None, tq, n_blk), lambda bb, kv, i: (bb, kv, i, 0)),
                  pl.BlockSpec((None, n_far + 1, G_C, tq, tq), lambda bb, kv, i: (kv, 0, 0, 0, 0))],
        out_specs=pl.BlockSpec((None, tq, qw), lambda bb, kv, i: (bb, i, kv)),
        scratch_shapes=[pltpu.VMEM((tq, t), F32), pltpu.VMEM((G_C, tq, 1), F32),
                        pltpu.VMEM((G_C, tq, 1), F32), pltpu.VMEM((G_C * tq, HD_C), F32)],
        compiler_params=_cparams(("parallel", "parallel", "arbitrary"), vmem),
        name="nsa_sel_prompt",
    )(q_arr, k_arr, v_arr, sel, tiles)


def _bdot(a, b, dims=(((1,), (0,)), ((), ()))):
    return lax.dot_general(a.astype(BF16), b.astype(BF16), dims, preferred_element_type=F32)


_NT = (((1,), (1,)), ((), ()))
_TN = (((0,), (0,)), ((), ()))


def _rwkv_kernel(p_ref, shift_ref, s0_ref, mu_ref, w0_ref, w2_ref, a0_ref, a2_ref, g2_ref, kk_ref, ka_ref,
                 rk_ref, lng_ref, lnb_ref, y_ref, sout_ref, carry_ref, state_ref, *, c, t_valid, n_chunks):
    ci = pl.program_id(1)

    @pl.when(ci == 0)
    def _():
        carry_ref[...] = shift_ref[...]
        state_ref[...] = s0_ref[...]

    p = p_ref[...]
    row = lax.broadcasted_iota(jnp.int32, (c, 1), 0)
    prev = jnp.where(row == 0, carry_ref[...], pltpu.roll(p, 1, axis=0))
    carry_ref[...] = p[c - 1:c, :]
    xs = p + (prev - p) * mu_ref[...]
    o = 3 * C_B
    r = xs[:, :C_B]
    k = xs[:, C_B:2 * C_B]
    v = xs[:, 2 * C_B:o]
    wd = xs[:, o:o + LORA_W]
    ad = xs[:, o + LORA_W:o + LORA_W + LORA_A]
    gd = xs[:, o + LORA_W + LORA_A:]
    w_raw = w0_ref[...] + _bdot(jnp.tanh(wd), w2_ref[...])
    logd = -jnp.exp(-jax.nn.softplus(-w_raw) - 0.5)
    a = jax.nn.sigmoid(a0_ref[...] + _bdot(ad, a2_ref[...]))
    gate = _bdot(jax.nn.sigmoid(gd), g2_ref[...])
    kk = k * kk_ref[...]
    k = k * (1.0 + (a - 1.0) * ka_ref[...])
    if t_valid < c:
        valid = row < t_valid
        logd = jnp.where(valid, logd, 0.0)
        r = jnp.where(valid, r, 0.0)
        k = jnp.where(valid, k, 0.0)
        v = jnp.where(valid, v, 0.0)
        kk = jnp.where(valid, kk, 0.0)
    ti = lax.broadcasted_iota(jnp.int32, (c, c), 0)
    si = lax.broadcasted_iota(jnp.int32, (c, c), 1)
    incl = si <= ti
    strict = si < ti
    tri = jnp.where(incl, 1.0, 0.0).astype(BF16)
    hi = logd.astype(BF16)
    rem = logd - hi.astype(F32)
    mid = rem.astype(BF16)
    lo = (rem - mid.astype(F32)).astype(BF16)
    cs = (jnp.dot(tri, hi, preferred_element_type=F32) + jnp.dot(tri, mid, preferred_element_type=F32)
          + jnp.dot(tri, lo, preferred_element_type=F32))
    e_pos = jnp.exp(cs)
    e_prev = jnp.exp(cs - logd)
    e_neg = jnp.exp(-cs)
    rk = r * k * rk_ref[...]
    n_levels = int(math.log2(c))
    assert 2 ** n_levels == c
    outs = []
    for h in range(H_B):
        sl = slice(h * HD_B, (h + 1) * HD_B)
        kk_h = kk[:, sl]
        kk_h = kk_h / jnp.maximum(jnp.sqrt(jnp.sum(kk_h * kk_h, axis=-1, keepdims=True)), 1e-12)
        v_h = v[:, sl]
        a_t = -kk_h * e_prev[:, sl]
        r_t = r[:, sl] * e_pos[:, sl]
        b_t = kk_h * a[:, sl] * e_neg[:, sl]
        k_t = k[:, sl] * e_neg[:, sl]
        lhs = jnp.concatenate([a_t, r_t], axis=0)
        rhs = jnp.concatenate([b_t, k_t], axis=0)
        mm = _bdot(lhs, rhs, _NT)
        m_ab = jnp.where(strict, mm[:c, :c], 0.0)
        m_ak = jnp.where(strict, mm[:c, c:], 0.0)
        m_rb = jnp.where(incl, mm[c:, :c], 0.0)
        m_rk = jnp.where(incl, mm[c:, c:], 0.0)
        s0 = state_ref[h]
        ars = _bdot(lhs, s0, _NT)
        u = ars[:c] + _bdot(m_ak, v_h)
        lp = m_ab
        for lvl in range(n_levels):
            u = u + _bdot(lp, u)
            if lvl < n_levels - 1:
                lp = _bdot(lp, lp)
        uv = jnp.concatenate([u, v_h], axis=0)
        y = ars[c:] + _bdot(jnp.concatenate([m_rb, m_rk], axis=1), uv)
        state_ref[h] = (s0 + _bdot(uv, rhs, _TN)) * e_pos[c - 1:c, sl]
        mean = jnp.mean(y, axis=-1, keepdims=True)
        var = jnp.mean(jnp.square(y - mean), axis=-1, keepdims=True)
        yn = (y - mean) * lax.rsqrt(var + GN_EPS) * lng_ref[:, sl] + lnb_ref[:, sl]
        bonus = jnp.sum(rk[:, sl], axis=-1, keepdims=True) * v_h
        outs.append((yn + bonus) * gate[:, sl])
    y_ref[...] = jnp.concatenate(outs, axis=-1).astype(y_ref.dtype)

    @pl.when(ci == n_chunks - 1)
    def _():
        sout_ref[...] = state_ref[...]


def rwkv_mix(p, shift0, s0, w, *, c, t_valid):
    b, t, _ = p.shape
    n_chunks = t // c
    fix2 = lambda bb, ci: (0, 0)
    vec = lambda n: pl.BlockSpec((1, n), fix2)
    vmem = 6 * c * B_COLS * 4 + 4 * H_B * HD_B * HD_B * 4 + 40 * c * C_B * 4 + (LORA_W + LORA_A + LORA_G) * C_B * 4
    return pl.pallas_call(
        functools.partial(_rwkv_kernel, c=c, t_valid=t_valid, n_chunks=n_chunks),
        out_shape=(jax.ShapeDtypeStruct((b, t, C_B), BF16),
                   jax.ShapeDtypeStruct((b, H_B, HD_B, HD_B), F32)),
        grid=(b, n_chunks),
        in_specs=[pl.BlockSpec((None, c, B_COLS), lambda bb, ci: (bb, ci, 0)),
                  pl.BlockSpec((None, 1, B_COLS), lambda bb, ci: (bb, 0, 0)),
                  pl.BlockSpec((None, H_B, HD_B, HD_B), lambda bb, ci: (bb, 0, 0, 0)),
                  vec(B_COLS), vec(C_B), pl.BlockSpec((LORA_W, C_B), fix2),
                  vec(C_B), pl.BlockSpec((LORA_A, C_B), fix2), pl.BlockSpec((LORA_G, C_B), fix2),
                  vec(C_B), vec(C_B), vec(C_B), vec(C_B), vec(C_B)],
        out_specs=(pl.BlockSpec((None, c, C_B), lambda bb, ci: (bb, ci, 0)),
                   pl.BlockSpec((None, H_B, HD_B, HD_B), lambda bb, ci: (bb, 0, 0, 0))),
        scratch_shapes=[pltpu.VMEM((1, B_COLS), F32), pltpu.VMEM((H_B, HD_B, HD_B), F32)],
        compiler_params=_cparams(("parallel", "arbitrary"), vmem),
        name="rwkv_mix",
    )(p, shift0, s0, w["mu"], w["w0"], w["w2"], w["a0"], w["a2"], w["g2"], w["k_k"], w["k_a"],
      w["r_k"], w["ln_g"], w["ln_b"])


def _page_copy(cache_ref, buf_ref, sem_ref, table_ref, step, slot, k, pp, rows):
    page = table_ref[step * pp + k]
    if rows is None:
        dst = buf_ref.at[slot, k]
    else:
        dst = buf_ref.at[slot, pl.ds(k * rows, rows)]
    return pltpu.make_async_copy(cache_ref.at[page], dst, sem_ref.at[slot])


def _stream_pages(cache_ref, buf_ref, sem_ref, table_ref, pp, rows):
    step = pl.program_id(0)
    n_steps = pl.num_programs(0)
    slot = step % 2

    @pl.when(step == 0)
    def _():
        for k in range(pp):
            _page_copy(cache_ref, buf_ref, sem_ref, table_ref, step, slot, k, pp, rows).start()

    @pl.when(step + 1 < n_steps)
    def _():
        for k in range(pp):
            _page_copy(cache_ref, buf_ref, sem_ref, table_ref, step + 1, 1 - slot, k, pp, rows).start()

    for k in range(pp):
        _page_copy(cache_ref, buf_ref, sem_ref, table_ref, step, slot, k, pp, rows).wait()
    return slot


CMP_ROW = CMP_BLOCK * 2 * KVH_C * HD_C
CMP_CH = 2 * KVH_C


def _compress_kernel(table_ref, cache_ref, pe_ref, w1_ref, w2_ref, o_ref, buf_ref, sem_ref, *, pp):
    slot = _stream_pages(cache_ref, buf_ref, sem_ref, table_ref, pp, 2)
    nr = 2 * pp
    for cc in range(2):
        def body(pos, acc):
            parts = []
            for h in range(KVH_C):
                off = pl.multiple_of((pos * CMP_CH + cc * KVH_C + h) * HD_C, HD_C)
                parts.append(buf_ref[slot, :, pl.ds(off, HD_C)] + pe_ref[:, pl.ds(off, HD_C)])
            lhs = jnp.concatenate(parts, axis=0).astype(BF16)
            w = w1_ref[cc, pl.ds(pl.multiple_of(pos * HD_C, HD_C), HD_C), :]
            return acc + jnp.dot(lhs, w, preferred_element_type=F32)

        acc = lax.fori_loop(0, CMP_BLOCK, body, jnp.zeros((KVH_C * nr, CMP_HIDDEN), F32))
        res = jnp.dot(jax.nn.gelu(acc).astype(BF16), w2_ref[cc], preferred_element_type=F32)
        for h in range(KVH_C):
            col = (cc * KVH_C + h) * HD_C
            o_ref[:, col:col + HD_C] = res[h * nr:(h + 1) * nr]


def nsa_compress(cache, table, pe, w1, w2, *, pp):
    n_pages = table.shape[0]
    assert n_pages % pp == 0
    nr = 2 * pp
    grid_spec = pltpu.PrefetchScalarGridSpec(
        num_scalar_prefetch=1,
        grid=(n_pages // pp,),
        in_specs=[pl.BlockSpec(memory_space=pl.ANY),
                  pl.BlockSpec((1, CMP_ROW), lambda s, tbl: (0, 0)),
                  pl.BlockSpec((2, CMP_BLOCK * HD_C, CMP_HIDDEN), lambda s, tbl: (0, 0, 0)),
                  pl.BlockSpec((2, CMP_HIDDEN, HD_C), lambda s, tbl: (0, 0, 0))],
        out_specs=pl.BlockSpec((nr, CMP_CH * HD_C), lambda s, tbl: (s, 0)),
        scratch_shapes=[pltpu.VMEM((2, nr, CMP_ROW), F32), pltpu.SemaphoreType.DMA((2,))],
    )
    vmem = 2 * nr * CMP_ROW * 4 + 2 * CMP_ROW * 4 * 8 + 4 * CMP_BLOCK * HD_C * CMP_HIDDEN * 2 + 8 * nr * 512 * 4
    return pl.pallas_call(
        functools.partial(_compress_kernel, pp=pp),
        out_shape=jax.ShapeDtypeStruct((2 * n_pages, CMP_CH * HD_C), F32),
        grid_spec=grid_spec,
        compiler_params=_cparams(("arbitrary",), vmem),
        name="nsa_compress",
    )(table, cache, pe, w1, w2)


def _nsa_sel_paged_kernel(table_ref, cache_ref, q_ref, sel_ref, bias_ref, knew_ref, vnew_ref, bnew_ref, o_ref,
                          buf_ref, sem_ref, m_ref, l_ref, acc_ref, *, pp, chunks, tq, g):
    hd = HD_C
    slot = _stream_pages(cache_ref, buf_ref, sem_ref, table_ref, pp, None)
    chunk = pl.program_id(0) % chunks
    nk = pp * PAGE_SIZE
    n_blk = sel_ref.shape[-1]
    scale = hd ** -0.5

    @pl.when(chunk == 0)
    def _():
        m_ref[...] = jnp.full(m_ref.shape, NEG_INF, F32)
        l_ref[...] = jnp.zeros(l_ref.shape, F32)
        acc_ref[...] = jnp.zeros(acc_ref.shape, F32)

    eb = lax.broadcasted_iota(jnp.int32, (n_blk, nk), 0)
    ek = lax.broadcasted_iota(jnp.int32, (n_blk, nk), 1)
    expand = jnp.where(eb == chunk * (nk // SEL_BLOCK) + (ek >> SEL_SHIFT), 1.0, 0.0).astype(BF16)
    q = q_ref[...]

    def update(h, s, mask, vv):
        lf = jnp.where(mask, s, NEG_INF)
        m_old = m_ref[h]
        m_new = jnp.maximum(m_old, jnp.max(lf, axis=-1, keepdims=True))
        e = jnp.where(mask, jnp.exp(lf - m_new), 0.0)
        alpha = jnp.exp(m_old - m_new)
        l_ref[h] = alpha * l_ref[h] + jnp.sum(e, axis=-1, keepdims=True)
        pv = jnp.dot(e.reshape(g * tq, -1).astype(BF16), vv, preferred_element_type=F32)
        acc_ref[h] = alpha.reshape(g * tq, 1) * acc_ref[h] + pv
        m_ref[h] = m_new

    for h in range(KVH_C):
        q8 = _stack_heads(q, h * g, g, hd).astype(BF16)
        kk = buf_ref[slot, :, :, h * hd:(h + 1) * hd].reshape(nk, hd).astype(BF16)
        vv = buf_ref[slot, :, :, (KVH_C + h) * hd:(KVH_C + h + 1) * hd].reshape(nk, hd).astype(BF16)
        s = lax.dot_general(q8, kk, _NT, preferred_element_type=F32) * scale
        s = s.reshape(g, tq, nk) + bias_ref[h * g:(h + 1) * g]
        selk = jnp.dot(sel_ref[h].astype(BF16), expand, preferred_element_type=F32)
        update(h, s, (selk > 0.5)[None], vv)

    @pl.when(chunk == chunks - 1)
    def _():
        r = lax.broadcasted_iota(jnp.int32, (tq, tq), 0)
        c = lax.broadcasted_iota(jnp.int32, (tq, tq), 1)
        for h in range(KVH_C):
            q8 = _stack_heads(q, h * g, g, hd).astype(BF16)
            kn = knew_ref[:, h * hd:(h + 1) * hd].astype(BF16)
            vn = vnew_ref[:, h * hd:(h + 1) * hd].astype(BF16)
            s = lax.dot_general(q8, kn, _NT, preferred_element_type=F32) * scale
            s = s.reshape(g, tq, tq) + bnew_ref[h * g:(h + 1) * g]
            update(h, s, (c <= r)[None], vn)
            o = acc_ref[h] / jnp.maximum(l_ref[h].reshape(g * tq, 1), 1e-30)
            for j in range(g):
                col = (h * g + j) * hd
                o_ref[:, col:col + hd] = o[j * tq:(j + 1) * tq]


def nsa_sel_paged(q_arr, cache, table, sel, bias, k_new, v_new, bias_new, *, pp, tq):
    b = q_arr.shape[0]
    n_pages = table.shape[0] // b
    chunks = n_pages // pp
    n_blk = sel.shape[-1]
    nk = pp * PAGE_SIZE
    kvw = KVH_C * HD_C
    grid_spec = pltpu.PrefetchScalarGridSpec(
        num_scalar_prefetch=1,
        grid=(b * chunks,),
        in_specs=[pl.BlockSpec(memory_space=pl.ANY),
                  pl.BlockSpec((None, tq, D_C), lambda s, tbl: (s // chunks, 0, 0)),
                  pl.BlockSpec((None, KVH_C, tq, n_blk), lambda s, tbl: (s // chunks, 0, 0, 0)),
                  pl.BlockSpec((H_C, tq, nk), lambda s, tbl: (0, 0, s % chunks)),
                  pl.BlockSpec((None, tq, kvw), lambda s, tbl: (s // chunks, 0, 0)),
                  pl.BlockSpec((None, tq, kvw), lambda s, tbl: (s // chunks, 0, 0)),
                  pl.BlockSpec((H_C, tq, tq), lambda s, tbl: (0, 0, 0))],
        out_specs=pl.BlockSpec((None, tq, D_C), lambda s, tbl: (s // chunks, 0, 0)),
        scratch_shapes=[pltpu.VMEM((2, pp, PAGE_SIZE, 2 * kvw), F32), pltpu.SemaphoreType.DMA((2,)),
                        pltpu.VMEM((KVH_C, G_C, tq, 1), F32), pltpu.VMEM((KVH_C, G_C, tq, 1), F32),
                        pltpu.VMEM((KVH_C, G_C * tq, HD_C), F32)],
    )
    vmem = 2 * pp * PAGE_SIZE * 2 * kvw * 4 + 2 * H_C * tq * nk * 4 + 10 * G_C * tq * nk * 4 + n_blk * nk * 4
    return pl.pallas_call(
        functools.partial(_nsa_sel_paged_kernel, pp=pp, chunks=chunks, tq=tq, g=G_C),
        out_shape=jax.ShapeDtypeStruct((b, tq, D_C), F32),
        grid_spec=grid_spec,
        compiler_params=_cparams(("arbitrary",), vmem),
        name="nsa_sel_paged",
    )(table, cache, q_arr, sel, bias, k_new, v_new, bias_new)


def _nsa_combine_kernel(gl_ref, gb_ref, oc_ref, os_ref, ow_ref, o_ref):
    gates = jax.nn.sigmoid(gl_ref[...] + gb_ref[...])
    for h in range(H_C):
        sl = slice(h * HD_C, (h + 1) * HD_C)
        o_ref[:, sl] = (gates[:, h:h + 1] * oc_ref[:, sl] + gates[:, H_C + h:H_C + h + 1] * os_ref[:, sl]
                        + gates[:, 2 * H_C + h:2 * H_C + h + 1] * ow_ref[:, sl]).astype(o_ref.dtype)


def nsa_combine(gate_logits, gate_bias, o_c, o_s, o_w):
    m = o_c.shape[0]
    tm = _row_tile(m, 512)
    row = lambda i: (i, 0)
    vmem = 2 * tm * (128 + 3 * D_C) * 4 + 2 * tm * D_C * 2
    return pl.pallas_call(
        _nsa_combine_kernel,
        out_shape=jax.ShapeDtypeStruct((m, D_C), BF16),
        grid=(m // tm,),
        in_specs=[pl.BlockSpec((tm, 128), row), pl.BlockSpec((1, 128), lambda i: (0, 0)),
                  pl.BlockSpec((tm, D_C), row), pl.BlockSpec((tm, D_C), row), pl.BlockSpec((tm, D_C), row)],
        out_specs=pl.BlockSpec((tm, D_C), row),
        compiler_params=_cparams(("parallel",), vmem),
        name="nsa_combine",
    )(gate_logits, gate_bias, o_c, o_s, o_w)


CMP_PAGES_PER_STEP = 32
SEL_PAGES_PER_STEP = 16
Q_PAD = V7X_SUBLANES


def _pad_rows(a, rows, front=0):
    return jnp.pad(a, ((0, 0), (front, rows - a.shape[1] - front), (0, 0)))


def _mixer_ab(x2, b, t, W, l, i, st):
    prompt = st is None
    proj3 = mm_norm(x2, W["mix_norm"][i], W["ab_w_in"][l], W["ab_bias"][l]).reshape(b, t, AB_COLS)
    k_new = proj3[:, :, D_A:D_A + KVH_A * HD_A]
    v_new = proj3[:, :, D_A + KVH_A * HD_A:A_COLS]
    p_b = proj3[:, :, A_COLS:]
    sinks = W["swa_sinks"][l].reshape(H_A, 1, 1)
    if prompt:
        nw = -(-WIN_A // QBLK)
        sk = (nw + 1) * QBLK
        bias = _toeplitz_bias(W["rel_bias"], QBLK, sk, nw * QBLK)
        o_a = band_attn(proj3, _pad_rows(k_new, t + nw * QBLK, nw * QBLK), _pad_rows(v_new, t + nw * QBLK, nw * QBLK),
                        bias, sinks, tq=QBLK, sk=sk, hd=HD_A, n_kvh_step=KVH_A, g=G_A, window=WIN_A,
                        delta=nw * QBLK, kpos_base=-nw * QBLK, kstride=QBLK, out_dtype=BF16)
        wb = min(WIN_A, t)
        buf = jnp.concatenate([k_new[:, t - wb:], v_new[:, t - wb:]], axis=-1)
        y_b, s_new = rwkv_mix(p_b, jnp.zeros((b, 1, B_COLS), F32), jnp.zeros((b, H_B, HD_B, HD_B), F32),
                              W["rwkv"][l], c=RWKV_CHUNK, t_valid=RWKV_CHUNK)
    else:
        past_len = st["past_len"]
        old = st["swa"][l].reshape(b, -1, 2 * KVH_A * HD_A)
        wb = old.shape[1]
        kv_all = jnp.concatenate([old, jnp.concatenate([k_new, v_new], axis=-1)], axis=1)
        buf = kv_all[:, t:]
        sk = -(-(wb + t) // V7X_LANES) * V7X_LANES
        kv_pad = _pad_rows(kv_all, sk)
        bias = _toeplitz_bias(W["rel_bias"], Q_PAD, sk, wb)
        o_a = band_attn(_pad_rows(proj3, Q_PAD), kv_pad[:, :, :KVH_A * HD_A], kv_pad[:, :, KVH_A * HD_A:],
                        bias, sinks, tq=Q_PAD, sk=sk, hd=HD_A, n_kvh_step=KVH_A, g=G_A, window=WIN_A,
                        delta=wb, kpos_base=past_len - wb, kstride=0, out_dtype=BF16)[:, :t]
        y_b, s_new = rwkv_mix(_pad_rows(p_b, Q_PAD), st["shift"][l][:, None], st["wkv"][l],
                              W["rwkv"][l], c=Q_PAD, t_valid=t)
        y_b = y_b[:, :t]
    mix_in = jnp.concatenate([o_a, y_b], axis=-1).reshape(b * t, D_MODEL)
    x2 = mm_res(mix_in, W["ab_w_out"][l], x2)
    buf = buf.reshape(b, -1, 2, KVH_A, HD_A)
    return x2, buf, p_b[:, -1], s_new


def _mixer_c(x2, b, t, W, l, i, st):
    prompt = st is None
    kvw = KVH_C * HD_C
    proj3 = mm_norm(x2, W["mix_norm"][i], W["c_w_in"][l], W["c_zero_bias"]).reshape(b, t, C_COLS_PAD)
    kv_cmp_new = proj3[:, :, D_C:D_C + 2 * kvw]
    kv_sel_new = proj3[:, :, D_C + 2 * kvw:D_C + 4 * kvw]
    kv_win_new = proj3[:, :, D_C + 4 * kvw:D_C + 6 * kvw]
    gate_logits = proj3[:, :, D_C + C_KV_COLS:D_C + C_KV_COLS + V7X_LANES].reshape(b * t, V7X_LANES)
    pe, w1, w2 = W["cmp_pe"][l], W["cmp_w1"][l], W["cmp_w2"][l]
    if prompt:
        n_pages = b * t // PAGE_SIZE
        kv_c = nsa_compress(kv_cmp_new.reshape(n_pages, 2, CMP_ROW), jnp.arange(n_pages, dtype=jnp.int32), pe, w1, w2,
                            pp=min(CMP_PAGES_PER_STEP, n_pages)).reshape(b, t // CMP_BLOCK, 2 * kvw)
        o_c, sel = nsa_cmp(proj3, kv_c, tq=QBLK, q0=0)
        o_s = nsa_sel_prompt(proj3, kv_sel_new[:, :, :kvw], kv_sel_new[:, :, kvw:], sel, W["rel_bias"], tq=QBLK)
        nw = -(-WIN_C // QBLK)
        sk = (nw + 1) * QBLK
        bias = _toeplitz_bias(W["rel_bias"], QBLK, sk, nw * QBLK)
        o_w = band_attn(proj3, _pad_rows(kv_win_new[:, :, :kvw], t + nw * QBLK, nw * QBLK),
                        _pad_rows(kv_win_new[:, :, kvw:], t + nw * QBLK, nw * QBLK), bias, None,
                        tq=QBLK, sk=sk, hd=HD_C, n_kvh_step=1, g=G_C, window=WIN_C,
                        delta=nw * QBLK, kpos_base=-nw * QBLK, kstride=QBLK, out_dtype=F32)
        win_buf = kv_win_new[:, t - min(WIN_C, t):]
    else:
        past_len = st["past_len"]
        table = st["page_table"].reshape(-1)
        n_phys = st["cmp"].shape[1]
        kv_c = nsa_compress(st["cmp"][l].reshape(n_phys, 2, CMP_ROW), table, pe, w1, w2,
                            pp=CMP_PAGES_PER_STEP).reshape(b, past_len // CMP_BLOCK, 2 * kvw)
        q8 = _pad_rows(proj3, Q_PAD)
        o_c, sel = nsa_cmp(q8, kv_c, tq=Q_PAD, q0=past_len)
        rows = jnp.arange(Q_PAD, dtype=jnp.int32)
        dist_past = past_len + rows[:, None] - jnp.arange(past_len, dtype=jnp.int32)[None, :]
        bias_past = W["rel_bias"].astype(F32)[t5_bucket(dist_past)].transpose(2, 0, 1)
        bias_new = _toeplitz_bias(W["rel_bias"], Q_PAD, Q_PAD, 0)
        sel_new = _pad_rows(kv_sel_new, Q_PAD)
        o_s = nsa_sel_paged(q8, st["sel"][l].reshape(n_phys, PAGE_SIZE, 2 * kvw), table, sel, bias_past,
                            sel_new[:, :, :kvw], sel_new[:, :, kvw:], bias_new, pp=SEL_PAGES_PER_STEP, tq=Q_PAD)
        old = st["win"][l].reshape(b, -1, 2 * kvw)
        wb = old.shape[1]
        kv_all = jnp.concatenate([old, kv_win_new], axis=1)
        win_buf = kv_all[:, t:]
        sk = -(-(wb + t) // V7X_LANES) * V7X_LANES
        kv_pad = _pad_rows(kv_all, sk)
        bias = _toeplitz_bias(W["rel_bias"], Q_PAD, sk, wb)
        o_w = band_attn(q8, kv_pad[:, :, :kvw], kv_pad[:, :, kvw:], bias, None, tq=Q_PAD, sk=sk, hd=HD_C,
                        n_kvh_step=1, g=G_C, window=WIN_C, delta=wb, kpos_base=past_len - wb, kstride=0,
                        out_dtype=F32)
        o_c, o_s, o_w = o_c[:, :t], o_s[:, :t], o_w[:, :t]
    comb = nsa_combine(gate_logits, W["c_gate_b"][l], o_c.reshape(b * t, D_C), o_s.reshape(b * t, D_C),
                       o_w.reshape(b * t, D_C))
    x2 = mm_res(comb, W["c_w_out"][l], x2)
    shape5 = lambda a: a.reshape(b, -1, 2, KVH_C, HD_C)
    return x2, shape5(kv_cmp_new), shape5(kv_sel_new), shape5(win_buf)


def _trunk(x, p, W, st):
    b, t, _ = x.shape
    x2 = x.reshape(b * t, D_MODEL)
    swa_l, shift_l, wkv_l, cmp_l, sel_l, win_l = [], [], [], [], [], []
    for i in range(DEPTH):
        l = i // 2
        if i % 2 == 0:
            x2, buf, shift_new, s_new = _mixer_ab(x2, b, t, W, l, i, st)
            swa_l.append(buf)
            shift_l.append(shift_new)
            wkv_l.append(s_new)
        else:
            x2, cmp_new, sel_new, win_buf = _mixer_c(x2, b, t, W, l, i, st)
            cmp_l.append(cmp_new)
            sel_l.append(sel_new)
            win_l.append(win_buf)
        act = ffn_up(x2, W["ffn_norm"][i], W["ffn_w_gate"][i], W["ffn_w_up"][i])
        x2 = mm_res(act, W["ffn_w_down"][i], x2, tk=D_FF // 4)
        x2 = ple(x2, p[i].reshape(b * t, PLE_DIM), W["ple_gate_norm"][i], W["ple_w_gate"][i], W["ple_w_proj"][i],
                 W["ple_post_norm"][i], W["final_norm"], final=(i == DEPTH - 1))
    y = x2.reshape(b, t, D_MODEL)
    return (y, jnp.stack(swa_l), jnp.stack(shift_l), jnp.stack(wkv_l), jnp.stack(cmp_l), jnp.stack(sel_l),
            jnp.stack(win_l))


def kernel(x_prompt, x_sample, state_swa_kv, state_rwkv_shift, state_rwkv_wkv, cache_nsa_cmp_kv, cache_nsa_sel_kv, state_nsa_win_kv, page_table, p_prompt, p_sample, rel_bias, mix_norm, ab_w_in, ab_b_qkv, swa_sinks, rwkv_mu, rwkv_w0, rwkv_w2, rwkv_a0, rwkv_a2, rwkv_g2, rwkv_k_k, rwkv_k_a, rwkv_r_k, rwkv_ln_g, rwkv_ln_b, ab_w_out, c_w_in, c_gate_b, nsa_cmp_pos, nsa_cmp_w1, nsa_cmp_w2, c_w_out, ffn_norm, ffn_w_gate, ffn_w_up, ffn_w_down, ple_w_proj, ple_gate_norm, ple_w_gate, ple_post_norm, final_norm):
    n_ab, n_c = ab_w_in.shape[0], c_w_in.shape[0]
    bf = lambda a: a.astype(BF16)
    row = lambda a: a.reshape(a.shape[0], 1, -1).astype(F32)
    pe = jnp.broadcast_to(nsa_cmp_pos.transpose(0, 2, 1, 3)[:, :, :, None, :],
                          (n_c, CMP_BLOCK, 2, KVH_C, HD_C)).reshape(n_c, 1, CMP_ROW)
    W = dict(
        rel_bias=rel_bias, mix_norm=row(mix_norm), ab_w_in=bf(ab_w_in),
        ab_bias=jnp.pad(ab_b_qkv, ((0, 0), (0, AB_COLS - A_COLS))).reshape(n_ab, 1, AB_COLS),
        swa_sinks=swa_sinks, ab_w_out=bf(ab_w_out),
        rwkv=[dict(mu=rwkv_mu[l][None], w0=rwkv_w0[l][None], w2=bf(rwkv_w2[l]), a0=rwkv_a0[l][None], a2=bf(rwkv_a2[l]),
                   g2=bf(rwkv_g2[l]), k_k=rwkv_k_k[l][None], k_a=rwkv_k_a[l][None], r_k=rwkv_r_k[l].reshape(1, C_B),
                   ln_g=rwkv_ln_g[l][None], ln_b=rwkv_ln_b[l][None]) for l in range(n_ab)],
        c_w_in=bf(jnp.pad(c_w_in, ((0, 0), (0, 0), (0, C_COLS_PAD - C_COLS)))),
        c_zero_bias=jnp.zeros((1, C_COLS_PAD), F32),
        c_gate_b=jnp.pad(c_gate_b, ((0, 0), (0, V7X_LANES - 3 * H_C))).reshape(n_c, 1, V7X_LANES),
        cmp_pe=pe, cmp_w1=bf(nsa_cmp_w1), cmp_w2=bf(nsa_cmp_w2), c_w_out=bf(c_w_out),
        ffn_norm=row(ffn_norm), ffn_w_gate=bf(ffn_w_gate), ffn_w_up=bf(ffn_w_up), ffn_w_down=bf(ffn_w_down),
        ple_w_proj=bf(ple_w_proj), ple_gate_norm=row(ple_gate_norm), ple_w_gate=bf(ple_w_gate),
        ple_post_norm=row(ple_post_norm), final_norm=final_norm.reshape(1, D_MODEL),
    )
    st = dict(swa=state_swa_kv, shift=state_rwkv_shift, wkv=state_rwkv_wkv, cmp=cache_nsa_cmp_kv,
              sel=cache_nsa_sel_kv, win=state_nsa_win_kv, page_table=page_table,
              past_len=page_table.shape[1] * PAGE_SIZE)
    y_p, swa_p, shift_p, wkv_p, cmp_p, sel_p, win_p = _trunk(x_prompt, p_prompt, W, None)
    y_s, swa_s, shift_s, wkv_s, cmp_s, sel_s, win_s = _trunk(x_sample, p_sample, W, st)
    return (y_p, y_s, swa_p, swa_s, shift_p, shift_s, wkv_p, wkv_s, cmp_p, cmp_s, sel_p, sel_s, win_p, win_s)
```

```python
import functools
import math

import jax
import jax.numpy as jnp
import numpy as np
from jax import lax
from jax.experimental import pallas as pl
from jax.experimental.pallas import tpu as pltpu

F32 = jnp.float32
BF16 = jnp.bfloat16

D_MODEL = 2048
DEPTH = 4
PAGE_SIZE = 128
PLE_DIM = 256
N_BUCKETS = 32
REL_MAX_DIST = 1024
RMS_EPS = 1e-6
D_FF = 5632
QBLK = 128
HD_A = 64
H_A = 16
KVH_A = 2
G_A = 8
D_A = 1024
WIN_A = 128
A_COLS = D_A + 2 * KVH_A * HD_A
HD_B = 64
C_B = 1024
H_B = 16
LORA_W = 64
LORA_A = 64
LORA_G = 128
B_COLS = 3 * C_B + LORA_W + LORA_A + LORA_G
AB_COLS = A_COLS + B_COLS
GN_EPS = 64e-5
HD_C = 128
H_C = 16
KVH_C = 2
G_C = 8
D_C = 2048
CMP_BLOCK = 64
SEL_BLOCK = 64
SEL_SHIFT = 6
N_TOP = 15
WIN_C = 512
CMP_HIDDEN = 128
C_KV_COLS = 6 * KVH_C * HD_C
C_COLS = D_C + C_KV_COLS + 3 * H_C
NEG_INF = -1e30
FORCE_SCORE = 1e4

V7X_LANES = 128
V7X_SUBLANES = 8
V7X_VMEM_BYTES = 64 * 1024 * 1024
VMEM_LIMIT_CAP = V7X_VMEM_BYTES - 8 * 1024 * 1024

C_COLS_PAD = 4096
RWKV_CHUNK = 64


def _cparams(sem, vmem_bytes):
    limit = int(min(max(2 * vmem_bytes, 32 * 1024 * 1024), VMEM_LIMIT_CAP))
    return pltpu.CompilerParams(dimension_semantics=sem, vmem_limit_bytes=limit)


def _row_tile(m, cap):
    t = min(m, cap)
    assert m % t == 0, (m, t)
    return t


def _rms(x, g):
    return x * lax.rsqrt(jnp.mean(x * x, axis=-1, keepdims=True) + RMS_EPS) * g


def t5_bucket(dist):
    n = jnp.maximum(dist, 0)
    max_exact = N_BUCKETS // 2
    nf = jnp.maximum(n, max_exact).astype(F32)
    large = max_exact + (jnp.log(nf / max_exact) / math.log(REL_MAX_DIST / max_exact) * (N_BUCKETS - max_exact)).astype(jnp.int32)
    return jnp.where(n < max_exact, n, jnp.minimum(large, N_BUCKETS - 1))


def _mm_norm_kernel(x_ref, g_ref, w_ref, b_ref, o_ref, h_ref):
    @pl.when(pl.program_id(1) == 0)
    def _():
        h_ref[...] = _rms(x_ref[...], g_ref[...]).astype(BF16)

    o_ref[...] = jnp.dot(h_ref[...], w_ref[...], preferred_element_type=F32) + b_ref[...]


def mm_norm(x, g, w, b, *, tn=512):
    m, k = x.shape
    n = w.shape[1]
    tm = _row_tile(m, 1024)
    vmem = 2 * tm * k * 4 + tm * k * 2 + 2 * k * tn * 2 + 2 * tm * tn * 4
    return pl.pallas_call(
        _mm_norm_kernel,
        out_shape=jax.ShapeDtypeStruct((m, n), F32),
        grid=(m // tm, n // tn),
        in_specs=[pl.BlockSpec((tm, k), lambda i, j: (i, 0)),
                  pl.BlockSpec((1, k), lambda i, j: (0, 0)),
                  pl.BlockSpec((k, tn), lambda i, j: (0, j)),
                  pl.BlockSpec((1, tn), lambda i, j: (0, j))],
        out_specs=pl.BlockSpec((tm, tn), lambda i, j: (i, j)),
        scratch_shapes=[pltpu.VMEM((tm, k), BF16)],
        compiler_params=_cparams(("parallel", "arbitrary"), vmem),
        name="mm_norm",
    )(x, g, w, b)


def _mm_res_kernel(a_ref, w_ref, r_ref, o_ref, acc_ref, *, nk):
    kk = pl.program_id(2)

    @pl.when(kk == 0)
    def _():
        acc_ref[...] = jnp.zeros_like(acc_ref)

    acc_ref[...] += jnp.dot(a_ref[...], w_ref[...], preferred_element_type=F32)

    @pl.when(kk == nk - 1)
    def _():
        o_ref[...] = r_ref[...] + acc_ref[...]


def mm_res(a, w, r, *, tn=512, tk=None):
    m, k = a.shape
    n = w.shape[1]
    tm = _row_tile(m, 1024)
    tk = k if tk is None else tk
    nk = k // tk
    vmem = 2 * tm * tk * 2 + 2 * tk * tn * 2 + 5 * tm * tn * 4
    return pl.pallas_call(
        functools.partial(_mm_res_kernel, nk=nk),
        out_shape=jax.ShapeDtypeStruct((m, n), F32),
        grid=(m // tm, n // tn, nk),
        in_specs=[pl.BlockSpec((tm, tk), lambda i, j, q: (i, q)),
                  pl.BlockSpec((tk, tn), lambda i, j, q: (q, j)),
                  pl.BlockSpec((tm, tn), lambda i, j, q: (i, j))],
        out_specs=pl.BlockSpec((tm, tn), lambda i, j, q: (i, j)),
        scratch_shapes=[pltpu.VMEM((tm, tn), F32)],
        compiler_params=_cparams(("parallel", "parallel", "arbitrary"), vmem),
        name="mm_res",
    )(a, w, r)


def _ffn_up_kernel(x_ref, g_ref, wg_ref, wu_ref, o_ref, h_ref):
    @pl.when(pl.program_id(1) == 0)
    def _():
        h_ref[...] = _rms(x_ref[...], g_ref[...]).astype(BF16)

    h = h_ref[...]
    gate = jnp.dot(h, wg_ref[...], preferred_element_type=F32)
    up = jnp.dot(h, wu_ref[...], preferred_element_type=F32)
    o_ref[...] = (jax.nn.silu(gate) * up).astype(BF16)


def ffn_up(x, g, wg, wu, *, tn=512):
    m, k = x.shape
    n = wg.shape[1]
    tm = _row_tile(m, 1024)
    vmem = 2 * tm * k * 4 + tm * k * 2 + 4 * k * tn * 2 + 2 * tm * tn * 2 + 3 * tm * tn * 4
    return pl.pallas_call(
        _ffn_up_kernel,
        out_shape=jax.ShapeDtypeStruct((m, n), BF16),
        grid=(m // tm, n // tn),
        in_specs=[pl.BlockSpec((tm, k), lambda i, j: (i, 0)),
                  pl.BlockSpec((1, k), lambda i, j: (0, 0)),
                  pl.BlockSpec((k, tn), lambda i, j: (0, j)),
                  pl.BlockSpec((k, tn), lambda i, j: (0, j))],
        out_specs=pl.BlockSpec((tm, tn), lambda i, j: (i, j)),
        scratch_shapes=[pltpu.VMEM((tm, k), BF16)],
        compiler_params=_cparams(("parallel", "arbitrary"), vmem),
        name="ffn_up",
    )(x, g, wg, wu)


def _ple_kernel(x_ref, p_ref, gn_ref, wg_ref, wp_ref, pn_ref, fn_ref, o_ref, *, final):
    x = x_ref[...]
    h = _rms(x, gn_ref[...]).astype(BF16)
    gate = jax.nn.sigmoid(jnp.dot(h, wg_ref[...], preferred_element_type=F32))
    e = jnp.dot(p_ref[...].astype(BF16), wp_ref[...], preferred_element_type=F32)
    x = x + _rms(gate * e, pn_ref[...])
    if final:
        x = _rms(x, fn_ref[...])
    o_ref[...] = x


def ple(x, p, gn, wg, wp, pn, fn, *, final):
    m, d = x.shape
    tm = _row_tile(m, 512)
    vmem = 4 * tm * d * 4 + 2 * d * d * 2 + 2 * PLE_DIM * d * 2 + 4 * tm * d * 4
    row = lambda i: (i, 0)
    fix = lambda i: (0, 0)
    return pl.pallas_call(
        functools.partial(_ple_kernel, final=final),
        out_shape=jax.ShapeDtypeStruct((m, d), F32),
        grid=(m // tm,),
        in_specs=[pl.BlockSpec((tm, d), row), pl.BlockSpec((tm, PLE_DIM), row),
                  pl.BlockSpec((1, d), fix), pl.BlockSpec((d, d), fix),
                  pl.BlockSpec((PLE_DIM, d), fix), pl.BlockSpec((1, d), fix),
                  pl.BlockSpec((1, d), fix)],
        out_specs=pl.BlockSpec((tm, d), row),
        compiler_params=_cparams(("parallel",), vmem),
        name="ple",
    )(x, p, gn, wg, wp, pn, fn)


def _masked_softmax(logits, mask, sink=None):
    lf = jnp.where(mask, logits, NEG_INF)
    m = jnp.max(lf, axis=-1, keepdims=True)
    if sink is not None:
        m = jnp.maximum(m, sink)
    e = jnp.where(mask, jnp.exp(lf - m), 0.0)
    den = jnp.sum(e, axis=-1, keepdims=True)
    if sink is not None:
        den = den + jnp.exp(sink - m)
    return e / jnp.maximum(den, 1e-30)


def _stack_heads(q, h0, g, hd):
    return jnp.concatenate([q[:, (h0 + j) * hd:(h0 + j + 1) * hd] for j in range(g)], axis=0)


def _band_attn_kernel(q_ref, k_ref, v_ref, bias_ref, sink_ref, o_ref, *,
                      tq, sk, hd, n_kvh, g, window, delta, kpos_base, kstride, has_sink):
    ks = pl.multiple_of(pl.program_id(2) * kstride, V7X_SUBLANES)
    kslab = k_ref[pl.ds(ks, sk), :]
    vslab = v_ref[pl.ds(ks, sk), :]
    r = lax.broadcasted_iota(jnp.int32, (tq, sk), 0)
    c = lax.broadcasted_iota(jnp.int32, (tq, sk), 1)
    dist = delta + r - c
    mask = ((dist >= 0) & (dist <= window) & (kpos_base + ks + c >= 0))[None]
    scale = hd ** -0.5
    q = q_ref[...]
    for h in range(n_kvh):
        q8 = _stack_heads(q, h * g, g, hd).astype(BF16)
        kh = kslab[:, h * hd:(h + 1) * hd].astype(BF16)
        vh = vslab[:, h * hd:(h + 1) * hd].astype(BF16)
        s = lax.dot_general(q8, kh, (((1,), (1,)), ((), ())), preferred_element_type=F32) * scale
        s = s.reshape(g, tq, sk) + bias_ref[h * g:(h + 1) * g]
        sink = sink_ref[h * g:(h + 1) * g] if has_sink else None
        p = _masked_softmax(s, mask, sink)
        o = jnp.dot(p.reshape(g * tq, sk).astype(BF16), vh, preferred_element_type=F32)
        for j in range(g):
            o_ref[:, (h * g + j) * hd:(h * g + j + 1) * hd] = o[j * tq:(j + 1) * tq].astype(o_ref.dtype)


def band_attn(q_arr, k_arr, v_arr, bias, sink, *, tq, sk, hd, n_kvh_step, g, window, delta,
              kpos_base, kstride, out_dtype):
    b, t = q_arr.shape[:2]
    tk = k_arr.shape[1]
    n_kv_blocks = k_arr.shape[2] // (n_kvh_step * hd)
    qw = n_kvh_step * g * hd
    has_sink = sink is not None
    if not has_sink:
        sink = jnp.zeros((n_kv_blocks * n_kvh_step * g, 1, 1), F32)
    vmem = 2 * tq * qw * 4 * 2 + 4 * tk * n_kvh_step * hd * 4 + 2 * n_kvh_step * g * tq * sk * 4 + 6 * g * tq * sk * 4
    kern = functools.partial(_band_attn_kernel, tq=tq, sk=sk, hd=hd, n_kvh=n_kvh_step, g=g, window=window,
                             delta=delta, kpos_base=kpos_base, kstride=kstride, has_sink=has_sink)
    return pl.pallas_call(
        kern,
        out_shape=jax.ShapeDtypeStruct((b, t, n_kv_blocks * qw), out_dtype),
        grid=(b, n_kv_blocks, t // tq),
        in_specs=[pl.BlockSpec((None, tq, qw), lambda bb, kv, i: (bb, i, kv)),
                  pl.BlockSpec((None, tk, n_kvh_step * hd), lambda bb, kv, i: (bb, 0, kv)),
                  pl.BlockSpec((None, tk, n_kvh_step * hd), lambda bb, kv, i: (bb, 0, kv)),
                  pl.BlockSpec((n_kvh_step * g, tq, sk), lambda bb, kv, i: (kv, 0, 0)),
                  pl.BlockSpec((n_kvh_step * g, 1, 1), lambda bb, kv, i: (kv, 0, 0))],
        out_specs=pl.BlockSpec((None, tq, qw), lambda bb, kv, i: (bb, i, kv)),
        compiler_params=_cparams(("parallel", "parallel", "arbitrary"), vmem),
        name="band_attn",
    )(q_arr, k_arr, v_arr, bias, sink)


def _toeplitz_bias(rel_bias, tq, sk, delta):
    dist = delta + jnp.arange(tq, dtype=jnp.int32)[:, None] - jnp.arange(sk, dtype=jnp.int32)[None, :]
    return rel_bias.astype(F32)[t5_bucket(dist)].transpose(2, 0, 1)


def _nsa_cmp_kernel(q_ref, kc_ref, vc_ref, oc_ref, sel_ref, *, tq, n_cmp, g, q0):
    hd = HD_C
    qp = q0 + pl.program_id(2) * tq + lax.broadcasted_iota(jnp.int32, (tq, n_cmp), 0)
    blk = lax.broadcasted_iota(jnp.int32, (tq, n_cmp), 1)
    q8 = _stack_heads(q_ref[...], 0, g, hd)
    kc = kc_ref[...]
    vc = vc_ref[...].astype(BF16)
    q_hi = q8.astype(BF16)
    q_lo = (q8 - q_hi.astype(F32)).astype(BF16)
    k_hi = kc.astype(BF16)
    k_lo = (kc - k_hi.astype(F32)).astype(BF16)
    cl = (lax.dot_general(q_hi, k_hi, _NT, preferred_element_type=F32)
          + lax.dot_general(q_hi, k_lo, _NT, preferred_element_type=F32)
          + lax.dot_general(q_lo, k_hi, _NT, preferred_element_type=F32)) * (hd ** -0.5)
    cmask = (blk + 1) * CMP_BLOCK <= qp + 1
    p_c = _masked_softmax(cl.reshape(g, tq, n_cmp), cmask[None])
    o = jnp.dot(p_c.reshape(g * tq, n_cmp).astype(BF16), vc, preferred_element_type=F32)
    for j in range(g):
        oc_ref[:, j * hd:(j + 1) * hd] = o[j * tq:(j + 1) * tq]
    cur = qp >> SEL_SHIFT
    imp = jnp.sum(p_c, axis=0) + jnp.where(blk == 0, FORCE_SCORE, 0.0)
    score = jnp.where(blk < cur, imp, -1.0)
    sel = jnp.where(blk == cur, 1.0, 0.0)
    for _ in range(N_TOP):
        m = jnp.max(score, axis=-1, keepdims=True)
        idx = jnp.min(jnp.where(score == m, blk, n_cmp), axis=-1, keepdims=True)
        hit = blk == idx
        sel = jnp.where(hit & (m >= 0.0), 1.0, sel)
        score = jnp.where(hit, -2.0, score)
    sel_ref[...] = sel


def nsa_cmp(q_arr, kv_c, *, tq, q0):
    b, t = q_arr.shape[:2]
    n_cmp = kv_c.shape[1]
    assert n_cmp >= N_TOP
    qw = G_C * HD_C
    vmem = 4 * tq * qw * 4 + 4 * n_cmp * HD_C * 4 + 8 * G_C * tq * n_cmp * 4
    return pl.pallas_call(
        functools.partial(_nsa_cmp_kernel, tq=tq, n_cmp=n_cmp, g=G_C, q0=q0),
        out_shape=(jax.ShapeDtypeStruct((b, t, D_C), F32),
                   jax.ShapeDtypeStruct((b, KVH_C, t, n_cmp), F32)),
        grid=(b, KVH_C, t // tq),
        in_specs=[pl.BlockSpec((None, tq, qw), lambda bb, kv, i: (bb, i, kv)),
                  pl.BlockSpec((None, n_cmp, HD_C), lambda bb, kv, i: (bb, 0, kv)),
                  pl.BlockSpec((None, n_cmp, HD_C), lambda bb, kv, i: (bb, 0, KVH_C + kv))],
        out_specs=(pl.BlockSpec((None, tq, qw), lambda bb, kv, i: (bb, i, kv)),
                   pl.BlockSpec((None, None, tq, n_cmp), lambda bb, kv, i: (bb, kv, i, 0))),
        compiler_params=_cparams(("parallel", "parallel", "arbitrary"), vmem),
        name="nsa_cmp",
    )(q_arr, kv_c, kv_c)


def _nsa_sel_kernel(q_ref, k_ref, v_ref, sel_ref, bias_ref, o_ref, selk_ref, mx_ref, le_ref, acc_ref, *,
                    tq, t, g, n_far):
    hd = HD_C
    i = pl.program_id(2)
    n_blk = t // SEL_BLOCK
    eb = lax.broadcasted_iota(jnp.int32, (n_blk, t), 0)
    ek = lax.broadcasted_iota(jnp.int32, (n_blk, t), 1)
    expand = jnp.where((ek >> SEL_SHIFT) == eb, 1.0, 0.0).astype(BF16)
    selk_ref[...] = jnp.dot(sel_ref[...].astype(BF16), expand, preferred_element_type=F32)
    q8 = (_stack_heads(q_ref[...], 0, g, hd) * (hd ** -0.5)).astype(BF16)
    r = lax.broadcasted_iota(jnp.int32, (tq, tq), 0)
    c = lax.broadcasted_iota(jnp.int32, (tq, tq), 1)

    def logits(j):
        ks = pl.multiple_of(j * tq, tq)
        kj = k_ref[pl.ds(ks, tq), :].astype(BF16)
        s = lax.dot_general(q8, kj, _NT, preferred_element_type=F32).reshape(g, tq, tq)
        mask = ((selk_ref[:, pl.ds(ks, tq)] > 0.5) & ((j - i) * tq + c <= r))[None]
        return s + jnp.where(mask, bias_ref[jnp.clip(i - j, 0, n_far)], NEG_INF)

    n_pairs = (i + 2) // 2
    mx_ref[...] = jnp.full(mx_ref.shape, NEG_INF, F32)

    def sweep_max(jj, carry):
        mx_ref[...] = jnp.maximum(mx_ref[...], jnp.maximum(logits(2 * jj), logits(2 * jj + 1)))
        return carry

    lax.fori_loop(0, n_pairs, sweep_max, 0)
    mx_ref[...] = jnp.broadcast_to(jnp.max(mx_ref[...], axis=-1, keepdims=True), mx_ref.shape)
    le_ref[...] = jnp.zeros(le_ref.shape, F32)
    acc_ref[...] = jnp.zeros(acc_ref.shape, F32)

    def sweep_acc(jj, carry):
        mx = mx_ref[...]
        e0 = jnp.exp(logits(2 * jj) - mx)
        e1 = jnp.exp(logits(2 * jj + 1) - mx)
        le_ref[...] += e0 + e1
        ks = pl.multiple_of(2 * jj * tq, 2 * tq)
        v0 = v_ref[pl.ds(ks, tq), :].astype(BF16)
        v1 = v_ref[pl.ds(ks + tq, tq), :].astype(BF16)
        acc_ref[...] += (jnp.dot(e0.reshape(g * tq, tq).astype(BF16), v0, preferred_element_type=F32)
                         + jnp.dot(e1.reshape(g * tq, tq).astype(BF16), v1, preferred_element_type=F32))
        return carry

    lax.fori_loop(0, n_pairs, sweep_acc, 0)
    den = jnp.sum(le_ref[...], axis=-1, keepdims=True).reshape(g * tq, 1)
    o = acc_ref[...] / jnp.maximum(den, 1e-30)
    for j in range(g):
        o_ref[:, j * hd:(j + 1) * hd] = o[j * tq:(j + 1) * tq]


def _bias_saturation_offset(tq, t):
    d = np.arange(0, t + tq, dtype=np.float64)
    nf = np.maximum(d, N_BUCKETS // 2)
    large = N_BUCKETS // 2 + np.floor(np.log(nf / (N_BUCKETS // 2)) / math.log(REL_MAX_DIST / (N_BUCKETS // 2))
                                      * (N_BUCKETS - N_BUCKETS // 2) - 1e-3)
    saturated = np.where(d < N_BUCKETS // 2, 0, large) >= N_BUCKETS - 1
    if not saturated.any():
        return t // tq
    first_sat = int(np.argmax(saturated))
    return min(t // tq, -(-(first_sat + tq) // tq))


def sel_bias_tiles(rel_bias, tq, t):
    n_far = _bias_saturation_offset(tq, t)
    dist = (jnp.arange(n_far + 1, dtype=jnp.int32)[:, None, None] * tq
            + jnp.arange(tq, dtype=jnp.int32)[None, :, None] - jnp.arange(tq, dtype=jnp.int32)[None, None, :])
    tiles = rel_bias.astype(F32)[t5_bucket(dist)]
    return tiles.reshape(n_far + 1, tq, tq, KVH_C, G_C).transpose(3, 0, 4, 1, 2)


def nsa_sel_prompt(q_arr, k_arr, v_arr, sel, tiles, *, tq):
    b, t = q_arr.shape[:2]
    assert (t // tq) % 2 == 0
    n_blk = t // SEL_BLOCK
    n_far = tiles.shape[1] - 1
    qw = G_C * HD_C
    vmem = 4 * tq * qw * 4 + 4 * t * HD_C * 4 + 2 * (n_far + 1) * G_C * tq * tq * 4 + tq * t * 4 + 8 * G_C * tq * tq * 4 + n_blk * t * 4
    return pl.pallas_call(
        functools.partial(_nsa_sel_kernel, tq=tq, t=t, g=G_C, n_far=n_far),
        out_shape=jax.ShapeDtypeStruct((b, t, D_C), F32),
        grid=(b, KVH_C, t // tq),
        in_specs=[pl.BlockSpec((None, tq, qw), lambda bb, kv, i: (bb, i, kv)),
                  pl.BlockSpec((None, t, HD_C), lambda bb, kv, i: (bb, 0, kv)),
                  pl.BlockSpec((None, t, HD_C), lambda bb, kv, i: (bb, 0, kv)),
                  pl.BlockSpec((None, None, tq, n_blk), lambda bb, kv, i: (bb, kv, i, 0)),
                  pl.BlockSpec((None, n_far + 1, G_C, tq, tq), lambda bb, kv, i: (kv, 0, 0, 0, 0))],
        out_specs=pl.BlockSpec((None, tq, qw), lambda bb, kv, i: (bb, i, kv)),
        scratch_shapes=[pltpu.VMEM((tq, t), F32), pltpu.VMEM((G_C, tq, tq), F32),
                        pltpu.VMEM((G_C, tq, tq), F32), pltpu.VMEM((G_C * tq, HD_C), F32)],
        compiler_params=_cparams(("parallel", "parallel", "arbitrary"), vmem),
        name="nsa_sel_prompt",
    )(q_arr, k_arr, v_arr, sel, tiles)


def _bdot(a, b, dims=(((1,), (0,)), ((), ()))):
    return lax.dot_general(a.astype(BF16), b.astype(BF16), dims, preferred_element_type=F32)


_NT = (((1,), (1,)), ((), ()))
_TN = (((0,), (0,)), ((), ()))


def _rwkv_kernel(p_ref, shift_ref, s0_ref, mu_ref, w0_ref, w2_ref, a0_ref, a2_ref, g2_ref, kk_ref, ka_ref,
                 rk_ref, lng_ref, lnb_ref, y_ref, sout_ref, carry_ref, state_ref, *, c, t_valid, n_chunks):
    ci = pl.program_id(1)

    @pl.when(ci == 0)
    def _():
        carry_ref[...] = shift_ref[...]
        state_ref[...] = s0_ref[...]

    p = p_ref[...]
    row = lax.broadcasted_iota(jnp.int32, (c, 1), 0)
    prev = jnp.where(row == 0, carry_ref[...], pltpu.roll(p, 1, axis=0))
    carry_ref[...] = p[c - 1:c, :]
    xs = p + (prev - p) * mu_ref[...]
    o = 3 * C_B
    r = xs[:, :C_B]
    k = xs[:, C_B:2 * C_B]
    v = xs[:, 2 * C_B:o]
    wd = xs[:, o:o + LORA_W]
    ad = xs[:, o + LORA_W:o + LORA_W + LORA_A]
    gd = xs[:, o + LORA_W + LORA_A:]
    w_raw = w0_ref[...] + _bdot(jnp.tanh(wd), w2_ref[...])
    logd = -jnp.exp(-jax.nn.softplus(-w_raw) - 0.5)
    a = jax.nn.sigmoid(a0_ref[...] + _bdot(ad, a2_ref[...]))
    gate = _bdot(jax.nn.sigmoid(gd), g2_ref[...])
    kk = k * kk_ref[...]
    k = k * (1.0 + (a - 1.0) * ka_ref[...])
    if t_valid < c:
        valid = row < t_valid
        logd = jnp.where(valid, logd, 0.0)
        r = jnp.where(valid, r, 0.0)
        k = jnp.where(valid, k, 0.0)
        v = jnp.where(valid, v, 0.0)
        kk = jnp.where(valid, kk, 0.0)
    ti = lax.broadcasted_iota(jnp.int32, (c, c), 0)
    si = lax.broadcasted_iota(jnp.int32, (c, c), 1)
    incl = si <= ti
    strict = si < ti
    tri = jnp.where(incl, 1.0, 0.0).astype(BF16)
    hi = logd.astype(BF16)
    rem = logd - hi.astype(F32)
    mid = rem.astype(BF16)
    lo = (rem - mid.astype(F32)).astype(BF16)
    cs = (jnp.dot(tri, hi, preferred_element_type=F32) + jnp.dot(tri, mid, preferred_element_type=F32)
          + jnp.dot(tri, lo, preferred_element_type=F32))
    e_pos = jnp.exp(cs)
    e_prev = jnp.exp(cs - logd)
    e_neg = jnp.exp(-cs)
    rk = r * k * rk_ref[...]
    n_levels = int(math.log2(c))
    assert 2 ** n_levels == c
    heads = range(H_B)
    sls = [slice(h * HD_B, (h + 1) * HD_B) for h in heads]
    kkn = []
    for sl in sls:
        kk_h = kk[:, sl]
        kkn.append(kk_h / jnp.maximum(jnp.sqrt(jnp.sum(kk_h * kk_h, axis=-1, keepdims=True)), 1e-12))
    v_h = [v[:, sl].astype(BF16) for sl in sls]
    lhs = [jnp.concatenate([-kkn[h] * e_prev[:, sls[h]], r[:, sls[h]] * e_pos[:, sls[h]]], axis=0).astype(BF16)
           for h in heads]
    rhs = [jnp.concatenate([kkn[h] * a[:, sls[h]] * e_neg[:, sls[h]], k[:, sls[h]] * e_neg[:, sls[h]]],
                           axis=0).astype(BF16) for h in heads]
    s0 = [state_ref[h] for h in heads]
    mm = [_bdot(lhs[h], rhs[h], _NT) for h in heads]
    ars = [_bdot(lhs[h], s0[h], _NT) for h in heads]
    u = [ars[h][:c] + _bdot(jnp.where(strict, mm[h][:c, c:], 0.0), v_h[h]) for h in heads]
    lp = [jnp.where(strict, mm[h][:c, :c], 0.0) for h in heads]
    for lvl in range(n_levels):
        u = [u[h] + _bdot(lp[h], u[h]) for h in heads]
        if lvl < n_levels - 1:
            lp = [_bdot(lp[h], lp[h]) for h in heads]
    uv = [jnp.concatenate([u[h].astype(BF16), v_h[h]], axis=0) for h in heads]
    t2 = lax.broadcasted_iota(jnp.int32, (c, 2 * c), 0)
    s2 = lax.broadcasted_iota(jnp.int32, (c, 2 * c), 1)
    incl2 = jnp.where(s2 >= c, s2 - c, s2) <= t2
    y = [ars[h][c:] + _bdot(jnp.where(incl2, mm[h][c:], 0.0), uv[h]) for h in heads]
    for h in heads:
        state_ref[h] = (s0[h] + _bdot(uv[h], rhs[h], _TN)) * e_pos[c - 1:c, sls[h]]
    outs = []
    for h in heads:
        sl = sls[h]
        mean = jnp.mean(y[h], axis=-1, keepdims=True)
        var = jnp.mean(jnp.square(y[h] - mean), axis=-1, keepdims=True)
        yn = (y[h] - mean) * lax.rsqrt(var + GN_EPS) * lng_ref[:, sl] + lnb_ref[:, sl]
        bonus = jnp.sum(rk[:, sl], axis=-1, keepdims=True) * v[:, sl]
        outs.append((yn + bonus) * gate[:, sl])
    y_ref[...] = jnp.concatenate(outs, axis=-1).astype(y_ref.dtype)

    @pl.when(ci == n_chunks - 1)
    def _():
        sout_ref[...] = state_ref[...]


def rwkv_mix(p, shift0, s0, w, *, c, t_valid):
    b, t, _ = p.shape
    n_chunks = t // c
    fix2 = lambda bb, ci: (0, 0)
    vec = lambda n: pl.BlockSpec((1, n), fix2)
    vmem = 6 * c * B_COLS * 4 + 4 * H_B * HD_B * HD_B * 4 + 40 * c * C_B * 4 + (LORA_W + LORA_A + LORA_G) * C_B * 4
    return pl.pallas_call(
        functools.partial(_rwkv_kernel, c=c, t_valid=t_valid, n_chunks=n_chunks),
        out_shape=(jax.ShapeDtypeStruct((b, t, C_B), BF16),
                   jax.ShapeDtypeStruct((b, H_B, HD_B, HD_B), F32)),
        grid=(b, n_chunks),
        in_specs=[pl.BlockSpec((None, c, B_COLS), lambda bb, ci: (bb, ci, 0)),
                  pl.BlockSpec((None, 1, B_COLS), lambda bb, ci: (bb, 0, 0)),
                  pl.BlockSpec((None, H_B, HD_B, HD_B), lambda bb, ci: (bb, 0, 0, 0)),
                  vec(B_COLS), vec(C_B), pl.BlockSpec((LORA_W, C_B), fix2),
                  vec(C_B), pl.BlockSpec((LORA_A, C_B), fix2), pl.BlockSpec((LORA_G, C_B), fix2),
                  vec(C_B), vec(C_B), vec(C_B), vec(C_B), vec(C_B)],
        out_specs=(pl.BlockSpec((None, c, C_B), lambda bb, ci: (bb, ci, 0)),
                   pl.BlockSpec((None, H_B, HD_B, HD_B), lambda bb, ci: (bb, 0, 0, 0))),
        scratch_shapes=[pltpu.VMEM((1, B_COLS), F32), pltpu.VMEM((H_B, HD_B, HD_B), F32)],
        compiler_params=_cparams(("parallel", "arbitrary"), vmem),
        name="rwkv_mix",
    )(p, shift0, s0, w["mu"], w["w0"], w["w2"], w["a0"], w["a2"], w["g2"], w["k_k"], w["k_a"],
      w["r_k"], w["ln_g"], w["ln_b"])


PAGE_CH = 2 * KVH_C
PAGE_ROWS = PAGE_SIZE * PAGE_CH


def _stream_pages(make_copies):
    step = pl.program_id(0)
    n_steps = pl.num_programs(0)
    slot = step % 2

    @pl.when(step == 0)
    def _():
        for cp in make_copies(step, slot):
            cp.start()

    @pl.when(step + 1 < n_steps)
    def _():
        for cp in make_copies(step + 1, 1 - slot):
            cp.start()

    for cp in make_copies(step, slot):
        cp.wait()
    return slot


BLOCK_ROWS = CMP_BLOCK * PAGE_CH


def _compress_kernel(table_ref, cache_ref, pe_ref, w1_ref, w2_ref, o_ref, buf_ref, sem_ref, *, pp):
    def make_copies(step, slot):
        out = []
        for k in range(pp):
            page = table_ref[step * pp + k]
            for n in range(2):
                out.append(pltpu.make_async_copy(cache_ref.at[page, pl.ds(n * BLOCK_ROWS, BLOCK_ROWS), :],
                                                 buf_ref.at[slot, :, 2 * k + n, :], sem_ref.at[slot]))
        return out

    slot = _stream_pages(make_copies)
    nr = 2 * pp
    for cc in range(2):
        cols = []
        for pos in range(CMP_BLOCK):
            q0 = pos * PAGE_CH + cc * KVH_C
            cols.append(jnp.concatenate([(buf_ref[slot, q0 + h] + pe_ref[q0 + h:q0 + h + 1, :]).astype(BF16)
                                         for h in range(KVH_C)], axis=0))
        flat = jnp.concatenate(cols, axis=1)
        acc = jnp.dot(flat, w1_ref[cc], preferred_element_type=F32)
        res = jnp.dot(jax.nn.gelu(acc).astype(BF16), w2_ref[cc], preferred_element_type=F32)
        for h in range(KVH_C):
            col = (cc * KVH_C + h) * HD_C
            o_ref[:, col:col + HD_C] = res[h * nr:(h + 1) * nr]


def nsa_compress(cache, table, pe, w1, w2, *, pp):
    n_pages = table.shape[0]
    assert n_pages % pp == 0
    nr = 2 * pp
    grid_spec = pltpu.PrefetchScalarGridSpec(
        num_scalar_prefetch=1,
        grid=(n_pages // pp,),
        in_specs=[pl.BlockSpec(memory_space=pl.ANY),
                  pl.BlockSpec((BLOCK_ROWS, HD_C), lambda s, tbl: (0, 0)),
                  pl.BlockSpec((2, CMP_BLOCK * HD_C, CMP_HIDDEN), lambda s, tbl: (0, 0, 0)),
                  pl.BlockSpec((2, CMP_HIDDEN, HD_C), lambda s, tbl: (0, 0, 0))],
        out_specs=pl.BlockSpec((nr, PAGE_CH * HD_C), lambda s, tbl: (s, 0)),
        scratch_shapes=[pltpu.VMEM((2, BLOCK_ROWS, nr, HD_C), F32), pltpu.SemaphoreType.DMA((2,))],
    )
    vmem = 2 * BLOCK_ROWS * nr * HD_C * 4 + 4 * CMP_BLOCK * HD_C * CMP_HIDDEN * 2 + 8 * nr * 512 * 4
    return pl.pallas_call(
        functools.partial(_compress_kernel, pp=pp),
        out_shape=jax.ShapeDtypeStruct((2 * n_pages, PAGE_CH * HD_C), F32),
        grid_spec=grid_spec,
        compiler_params=_cparams(("arbitrary",), vmem),
        name="nsa_compress",
    )(table, cache, pe, w1, w2)


def _nsa_sel_paged_kernel(table_ref, cache_ref, q_ref, sel_ref, bias_ref, knew_ref, vnew_ref, bnew_ref, o_ref,
                          buf_ref, sem_ref, m_ref, l_ref, acc_ref, *, pp, chunks, tq, g):
    hd = HD_C

    def make_copies(step, slot):
        return [pltpu.make_async_copy(cache_ref.at[table_ref[step * pp + k]], buf_ref.at[slot, k], sem_ref.at[slot])
                for k in range(pp)]

    slot = _stream_pages(make_copies)

    def page_rows(ch):
        return jnp.concatenate([buf_ref[slot, k, pl.ds(ch, PAGE_SIZE, stride=PAGE_CH), :] for k in range(pp)], axis=0)

    chunk = pl.program_id(0) % chunks
    nk = pp * PAGE_SIZE
    n_blk = sel_ref.shape[-1]
    scale = hd ** -0.5

    @pl.when(chunk == 0)
    def _():
        m_ref[...] = jnp.full(m_ref.shape, NEG_INF, F32)
        l_ref[...] = jnp.zeros(l_ref.shape, F32)
        acc_ref[...] = jnp.zeros(acc_ref.shape, F32)

    eb = lax.broadcasted_iota(jnp.int32, (n_blk, nk), 0)
    ek = lax.broadcasted_iota(jnp.int32, (n_blk, nk), 1)
    expand = jnp.where(eb == chunk * (nk // SEL_BLOCK) + (ek >> SEL_SHIFT), 1.0, 0.0).astype(BF16)
    q = q_ref[...]

    def update(h, s, mask, vv):
        lf = jnp.where(mask, s, NEG_INF)
        m_old = m_ref[h]
        m_new = jnp.maximum(m_old, jnp.max(lf, axis=-1, keepdims=True))
        e = jnp.where(mask, jnp.exp(lf - m_new), 0.0)
        alpha = jnp.exp(m_old - m_new)
        l_ref[h] = alpha * l_ref[h] + jnp.sum(e, axis=-1, keepdims=True)
        pv = jnp.dot(e.reshape(g * tq, -1).astype(BF16), vv, preferred_element_type=F32)
        acc_ref[h] = alpha.reshape(g * tq, 1) * acc_ref[h] + pv
        m_ref[h] = m_new

    for h in range(KVH_C):
        q8 = _stack_heads(q, h * g, g, hd).astype(BF16)
        kk = page_rows(h).astype(BF16)
        vv = page_rows(KVH_C + h).astype(BF16)
        s = lax.dot_general(q8, kk, _NT, preferred_element_type=F32) * scale
        s = s.reshape(g, tq, nk) + bias_ref[h * g:(h + 1) * g]
        selk = jnp.dot(sel_ref[h].astype(BF16), expand, preferred_element_type=F32)
        update(h, s, (selk > 0.5)[None], vv)

    @pl.when(chunk == chunks - 1)
    def _():
        r = lax.broadcasted_iota(jnp.int32, (tq, tq), 0)
        c = lax.broadcasted_iota(jnp.int32, (tq, tq), 1)
        for h in range(KVH_C):
            q8 = _stack_heads(q, h * g, g, hd).astype(BF16)
            kn = knew_ref[:, h * hd:(h + 1) * hd].astype(BF16)
            vn = vnew_ref[:, h * hd:(h + 1) * hd].astype(BF16)
            s = lax.dot_general(q8, kn, _NT, preferred_element_type=F32) * scale
            s = s.reshape(g, tq, tq) + bnew_ref[h * g:(h + 1) * g]
            update(h, s, (c <= r)[None], vn)
            o = acc_ref[h] / jnp.maximum(l_ref[h].reshape(g * tq, 1), 1e-30)
            for j in range(g):
                col = (h * g + j) * hd
                o_ref[:, col:col + hd] = o[j * tq:(j + 1) * tq]


def nsa_sel_paged(q_arr, cache, table, sel, bias, k_new, v_new, bias_new, *, pp, tq):
    b = q_arr.shape[0]
    n_pages = table.shape[0] // b
    chunks = n_pages // pp
    n_blk = sel.shape[-1]
    nk = pp * PAGE_SIZE
    kvw = KVH_C * HD_C
    grid_spec = pltpu.PrefetchScalarGridSpec(
        num_scalar_prefetch=1,
        grid=(b * chunks,),
        in_specs=[pl.BlockSpec(memory_space=pl.ANY),
                  pl.BlockSpec((None, tq, D_C), lambda s, tbl: (s // chunks, 0, 0)),
                  pl.BlockSpec((None, KVH_C, tq, n_blk), lambda s, tbl: (s // chunks, 0, 0, 0)),
                  pl.BlockSpec((H_C, tq, nk), lambda s, tbl: (0, 0, s % chunks)),
                  pl.BlockSpec((None, tq, kvw), lambda s, tbl: (s // chunks, 0, 0)),
                  pl.BlockSpec((None, tq, kvw), lambda s, tbl: (s // chunks, 0, 0)),
                  pl.BlockSpec((H_C, tq, tq), lambda s, tbl: (0, 0, 0))],
        out_specs=pl.BlockSpec((None, tq, D_C), lambda s, tbl: (s // chunks, 0, 0)),
        scratch_shapes=[pltpu.VMEM((2, pp, PAGE_ROWS, HD_C), F32), pltpu.SemaphoreType.DMA((2,)),
                        pltpu.VMEM((KVH_C, G_C, tq, 1), F32), pltpu.VMEM((KVH_C, G_C, tq, 1), F32),
                        pltpu.VMEM((KVH_C, G_C * tq, HD_C), F32)],
    )
    vmem = 2 * pp * PAGE_SIZE * 2 * kvw * 4 + 2 * H_C * tq * nk * 4 + 10 * G_C * tq * nk * 4 + n_blk * nk * 4
    return pl.pallas_call(
        functools.partial(_nsa_sel_paged_kernel, pp=pp, chunks=chunks, tq=tq, g=G_C),
        out_shape=jax.ShapeDtypeStruct((b, tq, D_C), F32),
        grid_spec=grid_spec,
        compiler_params=_cparams(("arbitrary",), vmem),
        name="nsa_sel_paged",
    )(table, cache, q_arr, sel, bias, k_new, v_new, bias_new)


def _nsa_combine_kernel(gl_ref, gb_ref, oc_ref, os_ref, ow_ref, o_ref):
    gates = jax.nn.sigmoid(gl_ref[...] + gb_ref[...])
    for h in range(H_C):
        sl = slice(h * HD_C, (h + 1) * HD_C)
        o_ref[:, sl] = (gates[:, h:h + 1] * oc_ref[:, sl] + gates[:, H_C + h:H_C + h + 1] * os_ref[:, sl]
                        + gates[:, 2 * H_C + h:2 * H_C + h + 1] * ow_ref[:, sl]).astype(o_ref.dtype)


def nsa_combine(gate_logits, gate_bias, o_c, o_s, o_w):
    m = o_c.shape[0]
    tm = _row_tile(m, 512)
    row = lambda i: (i, 0)
    vmem = 2 * tm * (128 + 3 * D_C) * 4 + 2 * tm * D_C * 2
    return pl.pallas_call(
        _nsa_combine_kernel,
        out_shape=jax.ShapeDtypeStruct((m, D_C), BF16),
        grid=(m // tm,),
        in_specs=[pl.BlockSpec((tm, 128), row), pl.BlockSpec((1, 128), lambda i: (0, 0)),
                  pl.BlockSpec((tm, D_C), row), pl.BlockSpec((tm, D_C), row), pl.BlockSpec((tm, D_C), row)],
        out_specs=pl.BlockSpec((tm, D_C), row),
        compiler_params=_cparams(("parallel",), vmem),
        name="nsa_combine",
    )(gate_logits, gate_bias, o_c, o_s, o_w)


CMP_PAGES_PER_STEP = 32
SEL_PAGES_PER_STEP = 16
Q_PAD = V7X_SUBLANES


def _pad_rows(a, rows, front=0):
    return jnp.pad(a, ((0, 0), (front, rows - a.shape[1] - front), (0, 0)))


def _memo(W, fn, *args):
    key = (fn.__name__,) + args
    if key not in W["tables"]:
        W["tables"][key] = fn(W["rel_bias"], *args)
    return W["tables"][key]


def _past_bias(rel_bias, tq, past_len):
    dist = past_len + jnp.arange(tq, dtype=jnp.int32)[:, None] - jnp.arange(past_len, dtype=jnp.int32)[None, :]
    return rel_bias.astype(F32)[t5_bucket(dist)].transpose(2, 0, 1)


def _mixer_ab(x2, b, t, W, l, i, st):
    prompt = st is None
    proj3 = mm_norm(x2, W["mix_norm"][i], W["ab_w_in"][l], W["ab_bias"][l]).reshape(b, t, AB_COLS)
    k_new = proj3[:, :, D_A:D_A + KVH_A * HD_A]
    v_new = proj3[:, :, D_A + KVH_A * HD_A:A_COLS]
    p_b = proj3[:, :, A_COLS:]
    sinks = W["swa_sinks"][l].reshape(H_A, 1, 1)
    if prompt:
        nw = -(-WIN_A // QBLK)
        sk = (nw + 1) * QBLK
        bias = _memo(W, _toeplitz_bias,QBLK, sk, nw * QBLK)
        o_a = band_attn(proj3, _pad_rows(k_new, t + nw * QBLK, nw * QBLK), _pad_rows(v_new, t + nw * QBLK, nw * QBLK),
                        bias, sinks, tq=QBLK, sk=sk, hd=HD_A, n_kvh_step=KVH_A, g=G_A, window=WIN_A,
                        delta=nw * QBLK, kpos_base=-nw * QBLK, kstride=QBLK, out_dtype=BF16)
        wb = min(WIN_A, t)
        buf = jnp.concatenate([k_new[:, t - wb:], v_new[:, t - wb:]], axis=-1)
        y_b, s_new = rwkv_mix(p_b, jnp.zeros((b, 1, B_COLS), F32), jnp.zeros((b, H_B, HD_B, HD_B), F32),
                              W["rwkv"][l], c=RWKV_CHUNK, t_valid=RWKV_CHUNK)
    else:
        past_len = st["past_len"]
        old = st["swa"][l].reshape(b, -1, 2 * KVH_A * HD_A)
        wb = old.shape[1]
        kv_all = jnp.concatenate([old, jnp.concatenate([k_new, v_new], axis=-1)], axis=1)
        buf = kv_all[:, t:]
        sk = -(-(wb + t) // V7X_LANES) * V7X_LANES
        kv_pad = _pad_rows(kv_all, sk)
        bias = _memo(W, _toeplitz_bias,Q_PAD, sk, wb)
        o_a = band_attn(_pad_rows(proj3, Q_PAD), kv_pad[:, :, :KVH_A * HD_A], kv_pad[:, :, KVH_A * HD_A:],
                        bias, sinks, tq=Q_PAD, sk=sk, hd=HD_A, n_kvh_step=KVH_A, g=G_A, window=WIN_A,
                        delta=wb, kpos_base=past_len - wb, kstride=0, out_dtype=BF16)[:, :t]
        y_b, s_new = rwkv_mix(_pad_rows(p_b, Q_PAD), st["shift"][l][:, None], st["wkv"][l],
                              W["rwkv"][l], c=Q_PAD, t_valid=t)
        y_b = y_b[:, :t]
    mix_in = jnp.concatenate([o_a, y_b], axis=-1).reshape(b * t, D_MODEL)
    x2 = mm_res(mix_in, W["ab_w_out"][l], x2)
    buf = buf.reshape(b, -1, 2, KVH_A, HD_A)
    return x2, buf, p_b[:, -1], s_new


def _mixer_c(x2, b, t, W, l, i, st):
    prompt = st is None
    kvw = KVH_C * HD_C
    proj3 = mm_norm(x2, W["mix_norm"][i], W["c_w_in"][l], W["c_zero_bias"]).reshape(b, t, C_COLS_PAD)
    kv_cmp_new = proj3[:, :, D_C:D_C + 2 * kvw]
    kv_sel_new = proj3[:, :, D_C + 2 * kvw:D_C + 4 * kvw]
    kv_win_new = proj3[:, :, D_C + 4 * kvw:D_C + 6 * kvw]
    gate_logits = proj3[:, :, D_C + C_KV_COLS:D_C + C_KV_COLS + V7X_LANES].reshape(b * t, V7X_LANES)
    pe, w1, w2 = W["cmp_pe"][l], W["cmp_w1"][l], W["cmp_w2"][l]
    if prompt:
        n_pages = b * t // PAGE_SIZE
        kv_c = nsa_compress(kv_cmp_new.reshape(n_pages, PAGE_ROWS, HD_C), jnp.arange(n_pages, dtype=jnp.int32),
                            pe, w1, w2, pp=min(CMP_PAGES_PER_STEP, n_pages)).reshape(b, t // CMP_BLOCK, 2 * kvw)
        o_c, sel = nsa_cmp(proj3, kv_c, tq=QBLK, q0=0)
        o_s = nsa_sel_prompt(proj3, kv_sel_new[:, :, :kvw], kv_sel_new[:, :, kvw:], sel,
                             _memo(W, sel_bias_tiles, QBLK, t), tq=QBLK)
        nw = -(-WIN_C // QBLK)
        sk = (nw + 1) * QBLK
        bias = _memo(W, _toeplitz_bias,QBLK, sk, nw * QBLK)
        o_w = band_attn(proj3, _pad_rows(kv_win_new[:, :, :kvw], t + nw * QBLK, nw * QBLK),
                        _pad_rows(kv_win_new[:, :, kvw:], t + nw * QBLK, nw * QBLK), bias, None,
                        tq=QBLK, sk=sk, hd=HD_C, n_kvh_step=1, g=G_C, window=WIN_C,
                        delta=nw * QBLK, kpos_base=-nw * QBLK, kstride=QBLK, out_dtype=F32)
        win_buf = kv_win_new[:, t - min(WIN_C, t):]
    else:
        past_len = st["past_len"]
        n_phys = st["cmp"].shape[1]
        table = st["page_table"].reshape(-1) + l * n_phys
        kv_c = nsa_compress(st["cmp"].reshape(-1, PAGE_ROWS, HD_C), table, pe, w1, w2,
                            pp=CMP_PAGES_PER_STEP).reshape(b, past_len // CMP_BLOCK, 2 * kvw)
        q8 = _pad_rows(proj3, Q_PAD)
        o_c, sel = nsa_cmp(q8, kv_c, tq=Q_PAD, q0=past_len)
        bias_past = _memo(W, _past_bias, Q_PAD, past_len)
        bias_new = _memo(W, _toeplitz_bias, Q_PAD, Q_PAD, 0)
        sel_new = _pad_rows(kv_sel_new, Q_PAD)
        o_s = nsa_sel_paged(q8, st["sel"].reshape(-1, PAGE_ROWS, HD_C), table, sel, bias_past,
                            sel_new[:, :, :kvw], sel_new[:, :, kvw:], bias_new, pp=SEL_PAGES_PER_STEP, tq=Q_PAD)
        old = st["win"][l].reshape(b, -1, 2 * kvw)
        wb = old.shape[1]
        kv_all = jnp.concatenate([old, kv_win_new], axis=1)
        win_buf = kv_all[:, t:]
        sk = -(-(wb + t) // V7X_LANES) * V7X_LANES
        kv_pad = _pad_rows(kv_all, sk)
        bias = _memo(W, _toeplitz_bias,Q_PAD, sk, wb)
        o_w = band_attn(q8, kv_pad[:, :, :kvw], kv_pad[:, :, kvw:], bias, None, tq=Q_PAD, sk=sk, hd=HD_C,
                        n_kvh_step=1, g=G_C, window=WIN_C, delta=wb, kpos_base=past_len - wb, kstride=0,
                        out_dtype=F32)
        o_c, o_s, o_w = o_c[:, :t], o_s[:, :t], o_w[:, :t]
    comb = nsa_combine(gate_logits, W["c_gate_b"][l], o_c.reshape(b * t, D_C), o_s.reshape(b * t, D_C),
                       o_w.reshape(b * t, D_C))
    x2 = mm_res(comb, W["c_w_out"][l], x2)
    shape5 = lambda a: a.reshape(b, -1, 2, KVH_C, HD_C)
    return x2, shape5(kv_cmp_new), shape5(kv_sel_new), shape5(win_buf)


def _trunk(x, p, W, st):
    b, t, _ = x.shape
    x2 = x.reshape(b * t, D_MODEL)
    swa_l, shift_l, wkv_l, cmp_l, sel_l, win_l = [], [], [], [], [], []
    for i in range(DEPTH):
        l = i // 2
        if i % 2 == 0:
            x2, buf, shift_new, s_new = _mixer_ab(x2, b, t, W, l, i, st)
            swa_l.append(buf)
            shift_l.append(shift_new)
            wkv_l.append(s_new)
        else:
            x2, cmp_new, sel_new, win_buf = _mixer_c(x2, b, t, W, l, i, st)
            cmp_l.append(cmp_new)
            sel_l.append(sel_new)
            win_l.append(win_buf)
        act = ffn_up(x2, W["ffn_norm"][i], W["ffn_w_gate"][i], W["ffn_w_up"][i])
        x2 = mm_res(act, W["ffn_w_down"][i], x2, tk=D_FF // 4)
        x2 = ple(x2, p[i].reshape(b * t, PLE_DIM), W["ple_gate_norm"][i], W["ple_w_gate"][i], W["ple_w_proj"][i],
                 W["ple_post_norm"][i], W["final_norm"], final=(i == DEPTH - 1))
    y = x2.reshape(b, t, D_MODEL)
    return (y, jnp.stack(swa_l), jnp.stack(shift_l), jnp.stack(wkv_l), jnp.stack(cmp_l), jnp.stack(sel_l),
            jnp.stack(win_l))


def kernel(x_prompt, x_sample, state_swa_kv, state_rwkv_shift, state_rwkv_wkv, cache_nsa_cmp_kv, cache_nsa_sel_kv, state_nsa_win_kv, page_table, p_prompt, p_sample, rel_bias, mix_norm, ab_w_in, ab_b_qkv, swa_sinks, rwkv_mu, rwkv_w0, rwkv_w2, rwkv_a0, rwkv_a2, rwkv_g2, rwkv_k_k, rwkv_k_a, rwkv_r_k, rwkv_ln_g, rwkv_ln_b, ab_w_out, c_w_in, c_gate_b, nsa_cmp_pos, nsa_cmp_w1, nsa_cmp_w2, c_w_out, ffn_norm, ffn_w_gate, ffn_w_up, ffn_w_down, ple_w_proj, ple_gate_norm, ple_w_gate, ple_post_norm, final_norm):
    n_ab, n_c = ab_w_in.shape[0], c_w_in.shape[0]
    bf = lambda a: a.astype(BF16)
    row = lambda a: a.reshape(a.shape[0], 1, -1).astype(F32)
    pe = jnp.broadcast_to(nsa_cmp_pos.transpose(0, 2, 1, 3)[:, :, :, None, :],
                          (n_c, CMP_BLOCK, 2, KVH_C, HD_C)).reshape(n_c, BLOCK_ROWS, HD_C)
    W = dict(
        tables={}, rel_bias=rel_bias,mix_norm=row(mix_norm), ab_w_in=bf(ab_w_in),
        ab_bias=jnp.pad(ab_b_qkv, ((0, 0), (0, AB_COLS - A_COLS))).reshape(n_ab, 1, AB_COLS),
        swa_sinks=swa_sinks, ab_w_out=bf(ab_w_out),
        rwkv=[dict(mu=rwkv_mu[l][None], w0=rwkv_w0[l][None], w2=bf(rwkv_w2[l]), a0=rwkv_a0[l][None], a2=bf(rwkv_a2[l]),
                   g2=bf(rwkv_g2[l]), k_k=rwkv_k_k[l][None], k_a=rwkv_k_a[l][None], r_k=rwkv_r_k[l].reshape(1, C_B),
                   ln_g=rwkv_ln_g[l][None], ln_b=rwkv_ln_b[l][None]) for l in range(n_ab)],
        c_w_in=bf(jnp.pad(c_w_in, ((0, 0), (0, 0), (0, C_COLS_PAD - C_COLS)))),
        c_zero_bias=jnp.zeros((1, C_COLS_PAD), F32),
        c_gate_b=jnp.pad(c_gate_b, ((0, 0), (0, V7X_LANES - 3 * H_C))).reshape(n_c, 1, V7X_LANES),
        cmp_pe=pe, cmp_w1=bf(nsa_cmp_w1), cmp_w2=bf(nsa_cmp_w2), c_w_out=bf(c_w_out),
        ffn_norm=row(ffn_norm), ffn_w_gate=bf(ffn_w_gate), ffn_w_up=bf(ffn_w_up), ffn_w_down=bf(ffn_w_down),
        ple_w_proj=bf(ple_w_proj), ple_gate_norm=row(ple_gate_norm), ple_w_gate=bf(ple_w_gate),
        ple_post_norm=row(ple_post_norm), final_norm=final_norm.reshape(1, D_MODEL),
    )
    st = dict(swa=state_swa_kv, shift=state_rwkv_shift, wkv=state_rwkv_wkv, cmp=cache_nsa_cmp_kv,
              sel=cache_nsa_sel_kv, win=state_nsa_win_kv, page_table=page_table,
              past_len=page_table.shape[1] * PAGE_SIZE)
    y_p, swa_p, shift_p, wkv_p, cmp_p, sel_p, win_p = _trunk(x_prompt, p_prompt, W, None)
    y_s, swa_s, shift_s, wkv_s, cmp_s, sel_s, win_s = _trunk(x_sample, p_sample, W, st)
    return (y_p, y_s, swa_p, swa_s, shift_p, shift_s, wkv_p, wkv_s, cmp_p, cmp_s, sel_p, sel_s, win_p, win_s)
```

```python
import functools
import math

import jax
import jax.numpy as jnp
import numpy as np
from jax import lax
from jax.experimental import pallas as pl
from jax.experimental.pallas import tpu as pltpu

F32 = jnp.float32
BF16 = jnp.bfloat16

D_MODEL = 2048
DEPTH = 4
PAGE_SIZE = 128
PLE_DIM = 256
N_BUCKETS = 32
REL_MAX_DIST = 1024
RMS_EPS = 1e-6
D_FF = 5632
QBLK = 128
HD_A = 64
H_A = 16
KVH_A = 2
G_A = 8
D_A = 1024
WIN_A = 128
A_COLS = D_A + 2 * KVH_A * HD_A
HD_B = 64
C_B = 1024
H_B = 16
LORA_W = 64
LORA_A = 64
LORA_G = 128
B_COLS = 3 * C_B + LORA_W + LORA_A + LORA_G
AB_COLS = A_COLS + B_COLS
GN_EPS = 64e-5
HD_C = 128
H_C = 16
KVH_C = 2
G_C = 8
D_C = 2048
CMP_BLOCK = 64
SEL_BLOCK = 64
SEL_SHIFT = 6
N_TOP = 15
WIN_C = 512
CMP_HIDDEN = 128
C_KV_COLS = 6 * KVH_C * HD_C
C_COLS = D_C + C_KV_COLS + 3 * H_C
NEG_INF = -1e30
FORCE_SCORE = 1e4

V7X_LANES = 128
V7X_SUBLANES = 8
V7X_VMEM_BYTES = 64 * 1024 * 1024
VMEM_LIMIT_CAP = V7X_VMEM_BYTES - 8 * 1024 * 1024

C_COLS_PAD = 4096
RWKV_CHUNK = 64


def _cparams(sem, vmem_bytes):
    limit = int(min(max(2 * vmem_bytes, 32 * 1024 * 1024), VMEM_LIMIT_CAP))
    return pltpu.CompilerParams(dimension_semantics=sem, vmem_limit_bytes=limit)


def _row_tile(m, cap):
    t = min(m, cap)
    assert m % t == 0, (m, t)
    return t


def _rms(x, g):
    return x * lax.rsqrt(jnp.mean(x * x, axis=-1, keepdims=True) + RMS_EPS) * g


def t5_bucket(dist):
    n = jnp.maximum(dist, 0)
    max_exact = N_BUCKETS // 2
    nf = jnp.maximum(n, max_exact).astype(F32)
    large = max_exact + (jnp.log(nf / max_exact) / math.log(REL_MAX_DIST / max_exact) * (N_BUCKETS - max_exact)).astype(jnp.int32)
    return jnp.where(n < max_exact, n, jnp.minimum(large, N_BUCKETS - 1))


def _mm_norm_kernel(x_ref, g_ref, w_ref, b_ref, o_ref, h_ref):
    @pl.when(pl.program_id(1) == 0)
    def _():
        h_ref[...] = _rms(x_ref[...], g_ref[...]).astype(BF16)

    o_ref[...] = jnp.dot(h_ref[...], w_ref[...], preferred_element_type=F32) + b_ref[...]


def mm_norm(x, g, w, b, *, tn=512):
    m, k = x.shape
    n = w.shape[1]
    tm = _row_tile(m, 1024)
    vmem = 2 * tm * k * 4 + tm * k * 2 + 2 * k * tn * 2 + 2 * tm * tn * 4
    return pl.pallas_call(
        _mm_norm_kernel,
        out_shape=jax.ShapeDtypeStruct((m, n), F32),
        grid=(m // tm, n // tn),
        in_specs=[pl.BlockSpec((tm, k), lambda i, j: (i, 0)),
                  pl.BlockSpec((1, k), lambda i, j: (0, 0)),
                  pl.BlockSpec((k, tn), lambda i, j: (0, j)),
                  pl.BlockSpec((1, tn), lambda i, j: (0, j))],
        out_specs=pl.BlockSpec((tm, tn), lambda i, j: (i, j)),
        scratch_shapes=[pltpu.VMEM((tm, k), BF16)],
        compiler_params=_cparams(("parallel", "arbitrary"), vmem),
        name="mm_norm",
    )(x, g, w, b)


def _mm_res_kernel(a_ref, w_ref, r_ref, o_ref, acc_ref, *, nk):
    kk = pl.program_id(2)

    @pl.when(kk == 0)
    def _():
        acc_ref[...] = jnp.zeros_like(acc_ref)

    acc_ref[...] += jnp.dot(a_ref[...], w_ref[...], preferred_element_type=F32)

    @pl.when(kk == nk - 1)
    def _():
        o_ref[...] = r_ref[...] + acc_ref[...]


def mm_res(a, w, r, *, tn=1024, tk=None):
    m, k = a.shape
    n = w.shape[1]
    tm = _row_tile(m, 1024)
    tk = k if tk is None else tk
    nk = k // tk
    vmem = 2 * tm * tk * 2 + 2 * tk * tn * 2 + 5 * tm * tn * 4
    return pl.pallas_call(
        functools.partial(_mm_res_kernel, nk=nk),
        out_shape=jax.ShapeDtypeStruct((m, n), F32),
        grid=(m // tm, n // tn, nk),
        in_specs=[pl.BlockSpec((tm, tk), lambda i, j, q: (i, q)),
                  pl.BlockSpec((tk, tn), lambda i, j, q: (q, j)),
                  pl.BlockSpec((tm, tn), lambda i, j, q: (i, j))],
        out_specs=pl.BlockSpec((tm, tn), lambda i, j, q: (i, j)),
        scratch_shapes=[pltpu.VMEM((tm, tn), F32)],
        compiler_params=_cparams(("parallel", "parallel", "arbitrary"), vmem),
        name="mm_res",
    )(a, w, r)


def _ffn_up_kernel(x_ref, g_ref, wg_ref, wu_ref, o_ref, h_ref):
    @pl.when(pl.program_id(1) == 0)
    def _():
        h_ref[...] = _rms(x_ref[...], g_ref[...]).astype(BF16)

    h = h_ref[...]
    gate = jnp.dot(h, wg_ref[...], preferred_element_type=F32)
    up = jnp.dot(h, wu_ref[...], preferred_element_type=F32)
    o_ref[...] = (jax.nn.silu(gate) * up).astype(BF16)


def ffn_up(x, g, wg, wu, *, tn=512):
    m, k = x.shape
    n = wg.shape[1]
    tm = _row_tile(m, 1024)
    vmem = 2 * tm * k * 4 + tm * k * 2 + 4 * k * tn * 2 + 2 * tm * tn * 2 + 3 * tm * tn * 4
    return pl.pallas_call(
        _ffn_up_kernel,
        out_shape=jax.ShapeDtypeStruct((m, n), BF16),
        grid=(m // tm, n // tn),
        in_specs=[pl.BlockSpec((tm, k), lambda i, j: (i, 0)),
                  pl.BlockSpec((1, k), lambda i, j: (0, 0)),
                  pl.BlockSpec((k, tn), lambda i, j: (0, j)),
                  pl.BlockSpec((k, tn), lambda i, j: (0, j))],
        out_specs=pl.BlockSpec((tm, tn), lambda i, j: (i, j)),
        scratch_shapes=[pltpu.VMEM((tm, k), BF16)],
        compiler_params=_cparams(("parallel", "arbitrary"), vmem),
        name="ffn_up",
    )(x, g, wg, wu)


def _ple_kernel(x_ref, p_ref, gn_ref, wg_ref, wp_ref, pn_ref, fn_ref, o_ref, *, final):
    x = x_ref[...]
    h = _rms(x, gn_ref[...]).astype(BF16)
    gate = jax.nn.sigmoid(jnp.dot(h, wg_ref[...], preferred_element_type=F32))
    e = jnp.dot(p_ref[...].astype(BF16), wp_ref[...], preferred_element_type=F32)
    x = x + _rms(gate * e, pn_ref[...])
    if final:
        x = _rms(x, fn_ref[...])
    o_ref[...] = x


def ple(x, p, gn, wg, wp, pn, fn, *, final):
    m, d = x.shape
    tm = _row_tile(m, 512)
    vmem = 4 * tm * d * 4 + 2 * d * d * 2 + 2 * PLE_DIM * d * 2 + 4 * tm * d * 4
    row = lambda i: (i, 0)
    fix = lambda i: (0, 0)
    return pl.pallas_call(
        functools.partial(_ple_kernel, final=final),
        out_shape=jax.ShapeDtypeStruct((m, d), F32),
        grid=(m // tm,),
        in_specs=[pl.BlockSpec((tm, d), row), pl.BlockSpec((tm, PLE_DIM), row),
                  pl.BlockSpec((1, d), fix), pl.BlockSpec((d, d), fix),
                  pl.BlockSpec((PLE_DIM, d), fix), pl.BlockSpec((1, d), fix),
                  pl.BlockSpec((1, d), fix)],
        out_specs=pl.BlockSpec((tm, d), row),
        compiler_params=_cparams(("parallel",), vmem),
        name="ple",
    )(x, p, gn, wg, wp, pn, fn)


def _masked_softmax(logits, mask, sink=None):
    lf = jnp.where(mask, logits, NEG_INF)
    m = jnp.max(lf, axis=-1, keepdims=True)
    if sink is not None:
        m = jnp.maximum(m, sink)
    e = jnp.where(mask, jnp.exp(lf - m), 0.0)
    den = jnp.sum(e, axis=-1, keepdims=True)
    if sink is not None:
        den = den + jnp.exp(sink - m)
    return e / jnp.maximum(den, 1e-30)


def _stack_heads(q, h0, g, hd):
    return jnp.concatenate([q[:, (h0 + j) * hd:(h0 + j + 1) * hd] for j in range(g)], axis=0)


def _band_attn_kernel(q_ref, k_ref, v_ref, bias_ref, sink_ref, o_ref, *,
                      tq, sk, hd, n_kvh, g, window, delta, kpos_base, kstride, has_sink):
    tl = V7X_LANES
    n_tiles = sk // tl
    ks = pl.multiple_of(pl.program_id(2) * kstride, V7X_SUBLANES)
    kslab = k_ref[pl.ds(ks, sk), :].astype(BF16)
    vslab = v_ref[pl.ds(ks, sk), :].astype(BF16)
    r = lax.broadcasted_iota(jnp.int32, (tq, tl), 0)
    c = lax.broadcasted_iota(jnp.int32, (tq, tl), 1)
    masks = []
    for t in range(n_tiles):
        dist = delta + r - (c + t * tl)
        masks.append(((dist >= 0) & (dist <= window) & (kpos_base + ks + t * tl + c >= 0))[None])
    q = q_ref[...] * (hd ** -0.5)
    chains = range(n_kvh)
    q8 = [_stack_heads(q, h * g, g, hd).astype(BF16) for h in chains]
    lf = [[lax.dot_general(q8[h], kslab[t * tl:(t + 1) * tl, h * hd:(h + 1) * hd], _NT,
                           preferred_element_type=F32).reshape(g, tq, tl)
           + jnp.where(masks[t], bias_ref[h * g:(h + 1) * g, :, t * tl:(t + 1) * tl], NEG_INF)
           for t in range(n_tiles)] for h in chains]
    m = [jnp.max(functools.reduce(jnp.maximum, lf[h]), axis=-1, keepdims=True) for h in chains]
    if has_sink:
        m = [jnp.maximum(m[h], sink_ref[h * g:(h + 1) * g]) for h in chains]
    mfull = [jnp.broadcast_to(m[h], (g, tq, tl)) for h in chains]
    e = [[jnp.exp(lf[h][t] - mfull[h]) for t in range(n_tiles)] for h in chains]
    acc = [functools.reduce(jnp.add, [jnp.dot(e[h][t].reshape(g * tq, tl).astype(BF16),
                                              vslab[t * tl:(t + 1) * tl, h * hd:(h + 1) * hd],
                                              preferred_element_type=F32) for t in range(n_tiles)])
           for h in chains]
    den = [jnp.sum(functools.reduce(jnp.add, e[h]), axis=-1, keepdims=True) for h in chains]
    if has_sink:
        den = [den[h] + jnp.exp(sink_ref[h * g:(h + 1) * g] - m[h]) for h in chains]
    for h in chains:
        o = acc[h] / jnp.maximum(den[h].reshape(g * tq, 1), 1e-30)
        for j in range(g):
            o_ref[:, (h * g + j) * hd:(h * g + j + 1) * hd] = o[j * tq:(j + 1) * tq].astype(o_ref.dtype)


def band_attn(q_arr, k_arr, v_arr, bias, sink, *, tq, sk, hd, n_kvh_step, g, window, delta,
              kpos_base, kstride, out_dtype):
    b, t = q_arr.shape[:2]
    tk = k_arr.shape[1]
    n_kv_blocks = k_arr.shape[2] // (n_kvh_step * hd)
    qw = n_kvh_step * g * hd
    has_sink = sink is not None
    if not has_sink:
        sink = jnp.zeros((n_kv_blocks * n_kvh_step * g, 1, 1), F32)
    vmem = 2 * tq * qw * 4 * 2 + 4 * tk * n_kvh_step * hd * 4 + 2 * n_kvh_step * g * tq * sk * 4 + 6 * g * tq * sk * 4
    kern = functools.partial(_band_attn_kernel, tq=tq, sk=sk, hd=hd, n_kvh=n_kvh_step, g=g, window=window,
                             delta=delta, kpos_base=kpos_base, kstride=kstride, has_sink=has_sink)
    return pl.pallas_call(
        kern,
        out_shape=jax.ShapeDtypeStruct((b, t, n_kv_blocks * qw), out_dtype),
        grid=(b, n_kv_blocks, t // tq),
        in_specs=[pl.BlockSpec((None, tq, qw), lambda bb, kv, i: (bb, i, kv)),
                  pl.BlockSpec((None, tk, n_kvh_step * hd), lambda bb, kv, i: (bb, 0, kv)),
                  pl.BlockSpec((None, tk, n_kvh_step * hd), lambda bb, kv, i: (bb, 0, kv)),
                  pl.BlockSpec((n_kvh_step * g, tq, sk), lambda bb, kv, i: (kv, 0, 0)),
                  pl.BlockSpec((n_kvh_step * g, 1, 1), lambda bb, kv, i: (kv, 0, 0))],
        out_specs=pl.BlockSpec((None, tq, qw), lambda bb, kv, i: (bb, i, kv)),
        compiler_params=_cparams(("parallel", "parallel", "arbitrary"), vmem),
        name="band_attn",
    )(q_arr, k_arr, v_arr, bias, sink)


def _bias_lookup(rel_bias, dist):
    onehot = (t5_bucket(dist)[..., None] == jnp.arange(N_BUCKETS, dtype=jnp.int32)).astype(F32)
    return jnp.einsum("...k,kh->...h", onehot, rel_bias.astype(F32), precision=lax.Precision.HIGHEST)


def _toeplitz_bias(rel_bias, tq, sk, delta):
    dist = delta + jnp.arange(tq, dtype=jnp.int32)[:, None] - jnp.arange(sk, dtype=jnp.int32)[None, :]
    return _bias_lookup(rel_bias, dist).transpose(2, 0, 1)


def _nsa_cmp_kernel(q_ref, kv_ref, oc_ref, sel_ref, *, tq, n_cmp, g, q0):
    hd = HD_C
    qp = q0 + pl.program_id(1) * tq + lax.broadcasted_iota(jnp.int32, (tq, n_cmp), 0)
    blk = lax.broadcasted_iota(jnp.int32, (tq, n_cmp), 1)
    cmask = ((blk + 1) * CMP_BLOCK <= qp + 1)[None]
    cur = qp >> SEL_SHIFT
    groups = range(KVH_C)
    q = q_ref[...]
    scores = []
    for h in groups:
        q8 = _stack_heads(q, h * g, g, hd)
        kc = kv_ref[:, h * hd:(h + 1) * hd]
        vc = kv_ref[:, (KVH_C + h) * hd:(KVH_C + h + 1) * hd].astype(BF16)
        q_hi = q8.astype(BF16)
        q_lo = (q8 - q_hi.astype(F32)).astype(BF16)
        k_hi = kc.astype(BF16)
        k_lo = (kc - k_hi.astype(F32)).astype(BF16)
        cl = (lax.dot_general(q_hi, k_hi, _NT, preferred_element_type=F32)
              + lax.dot_general(q_hi, k_lo, _NT, preferred_element_type=F32)
              + lax.dot_general(q_lo, k_hi, _NT, preferred_element_type=F32)) * (hd ** -0.5)
        p_c = _masked_softmax(cl.reshape(g, tq, n_cmp), cmask)
        o = jnp.dot(p_c.reshape(g * tq, n_cmp).astype(BF16), vc, preferred_element_type=F32)
        for j in range(g):
            oc_ref[:, (h * g + j) * hd:(h * g + j + 1) * hd] = o[j * tq:(j + 1) * tq]
        imp = jnp.sum(p_c, axis=0) + jnp.where(blk == 0, FORCE_SCORE, 0.0)
        scores.append(jnp.where(blk < cur, imp, -1.0))
    sels = [jnp.where(blk == cur, 1.0, 0.0) for _ in groups]
    for _ in range(N_TOP):
        tops = [jnp.max(scores[h], axis=-1, keepdims=True) for h in groups]
        hits = [blk == jnp.min(jnp.where(scores[h] == tops[h], blk, n_cmp), axis=-1, keepdims=True) for h in groups]
        sels = [jnp.where(hits[h] & (scores[h] >= 0.0), 1.0, sels[h]) for h in groups]
        scores = [jnp.where(hits[h], -2.0, scores[h]) for h in groups]
    for h in groups:
        sel_ref[h] = sels[h]


def nsa_cmp(q_arr, kv_c, *, tq, q0):
    b, t = q_arr.shape[:2]
    n_cmp = kv_c.shape[1]
    assert n_cmp >= N_TOP
    kvw = 2 * KVH_C * HD_C
    vmem = 4 * tq * D_C * 4 + 2 * n_cmp * kvw * 4 + 16 * G_C * tq * n_cmp * 4
    return pl.pallas_call(
        functools.partial(_nsa_cmp_kernel, tq=tq, n_cmp=n_cmp, g=G_C, q0=q0),
        out_shape=(jax.ShapeDtypeStruct((b, t, D_C), F32),
                   jax.ShapeDtypeStruct((b, KVH_C, t, n_cmp), F32)),
        grid=(b, t // tq),
        in_specs=[pl.BlockSpec((None, tq, D_C), lambda bb, i: (bb, i, 0)),
                  pl.BlockSpec((None, n_cmp, kvw), lambda bb, i: (bb, 0, 0))],
        out_specs=(pl.BlockSpec((None, tq, D_C), lambda bb, i: (bb, i, 0)),
                   pl.BlockSpec((None, KVH_C, tq, n_cmp), lambda bb, i: (bb, 0, i, 0))),
        compiler_params=_cparams(("parallel", "arbitrary"), vmem),
        name="nsa_cmp",
    )(q_arr, kv_c)


SEL_TILES_PER_TRIP = 4


def _nsa_sel_kernel(q_ref, k_ref, v_ref, sel_ref, bias_ref, o_ref, selk_ref, mx_ref, le_ref, acc_ref, *,
                    tq, t, g, n_far):
    hd = HD_C
    i = pl.program_id(2)
    n_blk = t // SEL_BLOCK
    eb = lax.broadcasted_iota(jnp.int32, (n_blk, t), 0)
    ek = lax.broadcasted_iota(jnp.int32, (n_blk, t), 1)
    expand = jnp.where((ek >> SEL_SHIFT) == eb, 1.0, 0.0).astype(BF16)
    selk_ref[...] = jnp.dot(sel_ref[...].astype(BF16), expand, preferred_element_type=F32)
    q8 = (_stack_heads(q_ref[...], 0, g, hd) * (hd ** -0.5)).astype(BF16)
    r = lax.broadcasted_iota(jnp.int32, (tq, tq), 0)
    c = lax.broadcasted_iota(jnp.int32, (tq, tq), 1)

    last = t // tq - 1

    def tile_start(j):
        return pl.multiple_of(jnp.minimum(j, last) * tq, tq)

    def logits(j):
        ks = tile_start(j)
        kj = k_ref[pl.ds(ks, tq), :].astype(BF16)
        s = lax.dot_general(q8, kj, _NT, preferred_element_type=F32).reshape(g, tq, tq)
        mask = ((selk_ref[:, pl.ds(ks, tq)] > 0.5) & ((j - i) * tq + c <= r))[None]
        return s + jnp.where(mask, bias_ref[jnp.clip(i - j, 0, n_far)], NEG_INF)

    per = SEL_TILES_PER_TRIP
    n_trips = (i + per) // per
    mx_ref[...] = jnp.full(mx_ref.shape, NEG_INF, F32)

    def sweep_max(jj, carry):
        tiles = [logits(per * jj + u) for u in range(per)]
        mx_ref[...] = jnp.maximum(mx_ref[...], functools.reduce(jnp.maximum, tiles))
        return carry

    lax.fori_loop(0, n_trips, sweep_max, 0)
    mx_ref[...] = jnp.broadcast_to(jnp.max(mx_ref[...], axis=-1, keepdims=True), mx_ref.shape)
    le_ref[...] = jnp.zeros(le_ref.shape, F32)
    acc_ref[...] = jnp.zeros(acc_ref.shape, F32)

    def sweep_acc(jj, carry):
        mx = mx_ref[...]
        es = [jnp.exp(logits(per * jj + u) - mx) for u in range(per)]
        le_ref[...] += functools.reduce(jnp.add, es)
        pvs = [jnp.dot(es[u].reshape(g * tq, tq).astype(BF16),
                       v_ref[pl.ds(tile_start(per * jj + u), tq), :].astype(BF16), preferred_element_type=F32)
               for u in range(per)]
        acc_ref[...] += functools.reduce(jnp.add, pvs)
        return carry

    lax.fori_loop(0, n_trips, sweep_acc, 0)
    den = jnp.sum(le_ref[...], axis=-1, keepdims=True).reshape(g * tq, 1)
    o = acc_ref[...] / jnp.maximum(den, 1e-30)
    for j in range(g):
        o_ref[:, j * hd:(j + 1) * hd] = o[j * tq:(j + 1) * tq]


def _bias_saturation_offset(tq, t):
    d = np.arange(0, t + tq, dtype=np.float64)
    nf = np.maximum(d, N_BUCKETS // 2)
    large = N_BUCKETS // 2 + np.floor(np.log(nf / (N_BUCKETS // 2)) / math.log(REL_MAX_DIST / (N_BUCKETS // 2))
                                      * (N_BUCKETS - N_BUCKETS // 2) - 1e-3)
    saturated = np.where(d < N_BUCKETS // 2, 0, large) >= N_BUCKETS - 1
    if not saturated.any():
        return t // tq
    first_sat = int(np.argmax(saturated))
    return min(t // tq, -(-(first_sat + tq) // tq))


def sel_bias_tiles(rel_bias, tq, t):
    n_far = _bias_saturation_offset(tq, t)
    dist = (jnp.arange(n_far + 1, dtype=jnp.int32)[:, None, None] * tq
            + jnp.arange(tq, dtype=jnp.int32)[None, :, None] - jnp.arange(tq, dtype=jnp.int32)[None, None, :])
    tiles = _bias_lookup(rel_bias, dist)
    return tiles.reshape(n_far + 1, tq, tq, KVH_C, G_C).transpose(3, 0, 4, 1, 2)


def nsa_sel_prompt(q_arr, k_arr, v_arr, sel, tiles, *, tq):
    b, t = q_arr.shape[:2]
    n_blk = t // SEL_BLOCK
    n_far = tiles.shape[1] - 1
    qw = G_C * HD_C
    vmem = 4 * tq * qw * 4 + 4 * t * HD_C * 4 + 2 * (n_far + 1) * G_C * tq * tq * 4 + tq * t * 4 + 8 * G_C * tq * tq * 4 + n_blk * t * 4
    return pl.pallas_call(
        functools.partial(_nsa_sel_kernel, tq=tq, t=t, g=G_C, n_far=n_far),
        out_shape=jax.ShapeDtypeStruct((b, t, D_C), F32),
        grid=(b, KVH_C, t // tq),
        in_specs=[pl.BlockSpec((None, tq, qw), lambda bb, kv, i: (bb, i, kv)),
                  pl.BlockSpec((None, t, HD_C), lambda bb, kv, i: (bb, 0, kv)),
                  pl.BlockSpec((None, t, HD_C), lambda bb, kv, i: (bb, 0, kv)),
                  pl.BlockSpec((None, None, tq, n_blk), lambda bb, kv, i: (bb, kv, i, 0)),
                  pl.BlockSpec((None, n_far + 1, G_C, tq, tq), lambda bb, kv, i: (kv, 0, 0, 0, 0))],
        out_specs=pl.BlockSpec((None, tq, qw), lambda bb, kv, i: (bb, i, kv)),
        scratch_shapes=[pltpu.VMEM((tq, t), F32), pltpu.VMEM((G_C, tq, tq), F32),
                        pltpu.VMEM((G_C, tq, tq), F32), pltpu.VMEM((G_C * tq, HD_C), F32)],
        compiler_params=_cparams(("parallel", "parallel", "arbitrary"), vmem),
        name="nsa_sel_prompt",
    )(q_arr, k_arr, v_arr, sel, tiles)


def _bdot(a, b, dims=(((1,), (0,)), ((), ()))):
    return lax.dot_general(a.astype(BF16), b.astype(BF16), dims, preferred_element_type=F32)


_NT = (((1,), (1,)), ((), ()))
_TN = (((0,), (0,)), ((), ()))


def _rwkv_kernel(p_ref, shift_ref, s0_ref, mu_ref, w0_ref, w2_ref, a0_ref, a2_ref, g2_ref, kk_ref, ka_ref,
                 rk_ref, lng_ref, lnb_ref, y_ref, sout_ref, carry_ref, state_ref, *, c, t_valid, n_chunks):
    ci = pl.program_id(1)

    @pl.when(ci == 0)
    def _():
        carry_ref[...] = shift_ref[...]
        state_ref[...] = s0_ref[...]

    p = p_ref[...]
    row = lax.broadcasted_iota(jnp.int32, (c, 1), 0)
    prev = jnp.where(row == 0, carry_ref[...], pltpu.roll(p, 1, axis=0))
    carry_ref[...] = p[c - 1:c, :]
    xs = p + (prev - p) * mu_ref[...]
    o = 3 * C_B
    r = xs[:, :C_B]
    k = xs[:, C_B:2 * C_B]
    v = xs[:, 2 * C_B:o]
    wd = xs[:, o:o + LORA_W]
    ad = xs[:, o + LORA_W:o + LORA_W + LORA_A]
    gd = xs[:, o + LORA_W + LORA_A:]
    w_raw = w0_ref[...] + _bdot(jnp.tanh(wd), w2_ref[...])
    logd = -jnp.exp(-jax.nn.softplus(-w_raw) - 0.5)
    a = jax.nn.sigmoid(a0_ref[...] + _bdot(ad, a2_ref[...]))
    gate = _bdot(jax.nn.sigmoid(gd), g2_ref[...])
    kk = k * kk_ref[...]
    k = k * (1.0 + (a - 1.0) * ka_ref[...])
    if t_valid < c:
        valid = row < t_valid
        logd = jnp.where(valid, logd, 0.0)
        r = jnp.where(valid, r, 0.0)
        k = jnp.where(valid, k, 0.0)
        v = jnp.where(valid, v, 0.0)
        kk = jnp.where(valid, kk, 0.0)
    ti = lax.broadcasted_iota(jnp.int32, (c, c), 0)
    si = lax.broadcasted_iota(jnp.int32, (c, c), 1)
    incl = si <= ti
    strict = si < ti
    tri = jnp.where(incl, 1.0, 0.0).astype(BF16)
    hi = logd.astype(BF16)
    rem = logd - hi.astype(F32)
    mid = rem.astype(BF16)
    lo = (rem - mid.astype(F32)).astype(BF16)
    cs = (jnp.dot(tri, hi, preferred_element_type=F32) + jnp.dot(tri, mid, preferred_element_type=F32)
          + jnp.dot(tri, lo, preferred_element_type=F32))
    e_pos = jnp.exp(cs)
    e_prev = jnp.exp(cs - logd)
    e_neg = jnp.exp(-cs)
    rk = r * k * rk_ref[...]
    n_levels = int(math.log2(c))
    assert 2 ** n_levels == c
    heads = range(H_B)
    sls = [slice(h * HD_B, (h + 1) * HD_B) for h in heads]
    kkn = []
    for sl in sls:
        kk_h = kk[:, sl]
        kkn.append(kk_h / jnp.maximum(jnp.sqrt(jnp.sum(kk_h * kk_h, axis=-1, keepdims=True)), 1e-12))
    v_h = [v[:, sl].astype(BF16) for sl in sls]
    lhs = [jnp.concatenate([-kkn[h] * e_prev[:, sls[h]], r[:, sls[h]] * e_pos[:, sls[h]]], axis=0).astype(BF16)
           for h in heads]
    rhs = [jnp.concatenate([kkn[h] * a[:, sls[h]] * e_neg[:, sls[h]], k[:, sls[h]] * e_neg[:, sls[h]]],
                           axis=0).astype(BF16) for h in heads]
    s0 = [state_ref[h] for h in heads]
    mm = [_bdot(lhs[h], rhs[h], _NT) for h in heads]
    ars = [_bdot(lhs[h], s0[h], _NT) for h in heads]
    u = [ars[h][:c] + _bdot(jnp.where(strict, mm[h][:c, c:], 0.0), v_h[h]) for h in heads]
    lp = [jnp.where(strict, mm[h][:c, :c], 0.0) for h in heads]
    for lvl in range(n_levels):
        u = [u[h] + _bdot(lp[h], u[h]) for h in heads]
        if lvl < n_levels - 1:
            lp = [_bdot(lp[h], lp[h]) for h in heads]
    uv = [jnp.concatenate([u[h].astype(BF16), v_h[h]], axis=0) for h in heads]
    t2 = lax.broadcasted_iota(jnp.int32, (c, 2 * c), 0)
    s2 = lax.broadcasted_iota(jnp.int32, (c, 2 * c), 1)
    incl2 = jnp.where(s2 >= c, s2 - c, s2) <= t2
    y = [ars[h][c:] + _bdot(jnp.where(incl2, mm[h][c:], 0.0), uv[h]) for h in heads]
    for h in heads:
        state_ref[h] = (s0[h] + _bdot(uv[h], rhs[h], _TN)) * e_pos[c - 1:c, sls[h]]
    outs = []
    for h in heads:
        sl = sls[h]
        mean = jnp.mean(y[h], axis=-1, keepdims=True)
        var = jnp.mean(jnp.square(y[h] - mean), axis=-1, keepdims=True)
        yn = (y[h] - mean) * lax.rsqrt(var + GN_EPS) * lng_ref[:, sl] + lnb_ref[:, sl]
        bonus = jnp.sum(rk[:, sl], axis=-1, keepdims=True) * v[:, sl]
        outs.append((yn + bonus) * gate[:, sl])
    y_ref[...] = jnp.concatenate(outs, axis=-1).astype(y_ref.dtype)

    @pl.when(ci == n_chunks - 1)
    def _():
        sout_ref[...] = state_ref[...]


def rwkv_mix(p, shift0, s0, w, *, c, t_valid):
    b, t, _ = p.shape
    n_chunks = t // c
    fix2 = lambda bb, ci: (0, 0)
    vec = lambda n: pl.BlockSpec((1, n), fix2)
    vmem = 6 * c * B_COLS * 4 + 4 * H_B * HD_B * HD_B * 4 + 40 * c * C_B * 4 + (LORA_W + LORA_A + LORA_G) * C_B * 4
    return pl.pallas_call(
        functools.partial(_rwkv_kernel, c=c, t_valid=t_valid, n_chunks=n_chunks),
        out_shape=(jax.ShapeDtypeStruct((b, t, C_B), BF16),
                   jax.ShapeDtypeStruct((b, H_B, HD_B, HD_B), F32)),
        grid=(b, n_chunks),
        in_specs=[pl.BlockSpec((None, c, B_COLS), lambda bb, ci: (bb, ci, 0)),
                  pl.BlockSpec((None, 1, B_COLS), lambda bb, ci: (bb, 0, 0)),
                  pl.BlockSpec((None, H_B, HD_B, HD_B), lambda bb, ci: (bb, 0, 0, 0)),
                  vec(B_COLS), vec(C_B), pl.BlockSpec((LORA_W, C_B), fix2),
                  vec(C_B), pl.BlockSpec((LORA_A, C_B), fix2), pl.BlockSpec((LORA_G, C_B), fix2),
                  vec(C_B), vec(C_B), vec(C_B), vec(C_B), vec(C_B)],
        out_specs=(pl.BlockSpec((None, c, C_B), lambda bb, ci: (bb, ci, 0)),
                   pl.BlockSpec((None, H_B, HD_B, HD_B), lambda bb, ci: (bb, 0, 0, 0))),
        scratch_shapes=[pltpu.VMEM((1, B_COLS), F32), pltpu.VMEM((H_B, HD_B, HD_B), F32)],
        compiler_params=_cparams(("parallel", "arbitrary"), vmem),
        name="rwkv_mix",
    )(p, shift0, s0, w["mu"], w["w0"], w["w2"], w["a0"], w["a2"], w["g2"], w["k_k"], w["k_a"],
      w["r_k"], w["ln_g"], w["ln_b"])


PAGE_CH = 2 * KVH_C
PAGE_ROWS = PAGE_SIZE * PAGE_CH


def _stream_pages(make_copies):
    step = pl.program_id(0)
    n_steps = pl.num_programs(0)
    slot = step % 2

    @pl.when(step == 0)
    def _():
        for cp in make_copies(step, slot):
            cp.start()

    @pl.when(step + 1 < n_steps)
    def _():
        for cp in make_copies(step + 1, 1 - slot):
            cp.start()

    for cp in make_copies(step, slot):
        cp.wait()
    return slot


BLOCK_ROWS = CMP_BLOCK * PAGE_CH


def _compress_kernel(table_ref, cache_ref, pe_ref, w1_ref, w2_ref, o_ref, buf_ref, sem_ref, *, pp):
    def make_copies(step, slot):
        out = []
        for k in range(pp):
            page = table_ref[step * pp + k]
            for n in range(2):
                out.append(pltpu.make_async_copy(cache_ref.at[page, pl.ds(n * BLOCK_ROWS, BLOCK_ROWS), :],
                                                 buf_ref.at[slot, :, 2 * k + n, :], sem_ref.at[slot]))
        return out

    slot = _stream_pages(make_copies)
    nr = 2 * pp
    for cc in range(2):
        cols = []
        for pos in range(CMP_BLOCK):
            q0 = pos * PAGE_CH + cc * KVH_C
            cols.append(jnp.concatenate([(buf_ref[slot, q0 + h] + pe_ref[q0 + h:q0 + h + 1, :]).astype(BF16)
                                         for h in range(KVH_C)], axis=0))
        flat = jnp.concatenate(cols, axis=1)
        acc = jnp.dot(flat, w1_ref[cc], preferred_element_type=F32)
        res = jnp.dot(jax.nn.gelu(acc).astype(BF16), w2_ref[cc], preferred_element_type=F32)
        for h in range(KVH_C):
            col = (cc * KVH_C + h) * HD_C
            o_ref[:, col:col + HD_C] = res[h * nr:(h + 1) * nr]


def nsa_compress(cache, table, pe, w1, w2, *, pp):
    n_pages = table.shape[0]
    assert n_pages % pp == 0
    nr = 2 * pp
    grid_spec = pltpu.PrefetchScalarGridSpec(
        num_scalar_prefetch=1,
        grid=(n_pages // pp,),
        in_specs=[pl.BlockSpec(memory_space=pl.ANY),
                  pl.BlockSpec((BLOCK_ROWS, HD_C), lambda s, tbl: (0, 0)),
                  pl.BlockSpec((2, CMP_BLOCK * HD_C, CMP_HIDDEN), lambda s, tbl: (0, 0, 0)),
                  pl.BlockSpec((2, CMP_HIDDEN, HD_C), lambda s, tbl: (0, 0, 0))],
        out_specs=pl.BlockSpec((nr, PAGE_CH * HD_C), lambda s, tbl: (s, 0)),
        scratch_shapes=[pltpu.VMEM((2, BLOCK_ROWS, nr, HD_C), F32), pltpu.SemaphoreType.DMA((2,))],
    )
    vmem = 2 * BLOCK_ROWS * nr * HD_C * 4 + 4 * CMP_BLOCK * HD_C * CMP_HIDDEN * 2 + 8 * nr * 512 * 4
    return pl.pallas_call(
        functools.partial(_compress_kernel, pp=pp),
        out_shape=jax.ShapeDtypeStruct((2 * n_pages, PAGE_CH * HD_C), F32),
        grid_spec=grid_spec,
        compiler_params=_cparams(("arbitrary",), vmem),
        name="nsa_compress",
    )(table, cache, pe, w1, w2)


def _nsa_sel_paged_kernel(table_ref, cache_ref, q_ref, sel_ref, bias_ref, knew_ref, vnew_ref, bnew_ref, o_ref,
                          buf_ref, sem_ref, m_ref, l_ref, acc_ref, *, pp, chunks, tq, g):
    hd = HD_C

    def make_copies(step, slot):
        return [pltpu.make_async_copy(cache_ref.at[table_ref[step * pp + k]], buf_ref.at[slot, k], sem_ref.at[slot])
                for k in range(pp)]

    slot = _stream_pages(make_copies)

    def page_rows(ch):
        return jnp.concatenate([buf_ref[slot, k, pl.ds(ch, PAGE_SIZE, stride=PAGE_CH), :] for k in range(pp)], axis=0)

    chunk = pl.program_id(0) % chunks
    nk = pp * PAGE_SIZE
    n_blk = sel_ref.shape[-1]
    scale = hd ** -0.5

    @pl.when(chunk == 0)
    def _():
        m_ref[...] = jnp.full(m_ref.shape, NEG_INF, F32)
        l_ref[...] = jnp.zeros(l_ref.shape, F32)
        acc_ref[...] = jnp.zeros(acc_ref.shape, F32)

    eb = lax.broadcasted_iota(jnp.int32, (n_blk, nk), 0)
    ek = lax.broadcasted_iota(jnp.int32, (n_blk, nk), 1)
    expand = jnp.where(eb == chunk * (nk // SEL_BLOCK) + (ek >> SEL_SHIFT), 1.0, 0.0).astype(BF16)
    q = q_ref[...]

    groups = range(KVH_C)
    q8 = [(_stack_heads(q, h * g, g, hd) * scale).astype(BF16) for h in groups]

    def update(lf, vv):
        m_old = [m_ref[h] for h in groups]
        m_new = [jnp.maximum(m_old[h], jnp.max(lf[h], axis=-1, keepdims=True)) for h in groups]
        e = [jnp.exp(lf[h] - m_new[h]) for h in groups]
        alpha = [jnp.exp(m_old[h] - m_new[h]) for h in groups]
        pv = [jnp.dot(e[h].reshape(g * tq, -1).astype(BF16), vv[h], preferred_element_type=F32) for h in groups]
        for h in groups:
            l_ref[h] = alpha[h] * l_ref[h] + jnp.sum(e[h], axis=-1, keepdims=True)
            acc_ref[h] = alpha[h].reshape(g * tq, 1) * acc_ref[h] + pv[h]
            m_ref[h] = m_new[h]

    kk = [page_rows(h).astype(BF16) for h in groups]
    vv = [page_rows(KVH_C + h).astype(BF16) for h in groups]
    selk = [jnp.dot(sel_ref[h].astype(BF16), expand, preferred_element_type=F32) for h in groups]
    lf = [lax.dot_general(q8[h], kk[h], _NT, preferred_element_type=F32).reshape(g, tq, nk)
          + jnp.where((selk[h] > 0.5)[None], bias_ref[h * g:(h + 1) * g], NEG_INF) for h in groups]
    update(lf, vv)

    @pl.when(chunk == chunks - 1)
    def _():
        r = lax.broadcasted_iota(jnp.int32, (tq, tq), 0)
        c = lax.broadcasted_iota(jnp.int32, (tq, tq), 1)
        kn = [knew_ref[:, h * hd:(h + 1) * hd].astype(BF16) for h in groups]
        vn = [vnew_ref[:, h * hd:(h + 1) * hd].astype(BF16) for h in groups]
        lf_new = [lax.dot_general(q8[h], kn[h], _NT, preferred_element_type=F32).reshape(g, tq, tq)
                  + jnp.where((c <= r)[None], bnew_ref[h * g:(h + 1) * g], NEG_INF) for h in groups]
        update(lf_new, vn)
        for h in groups:
            o = acc_ref[h] / jnp.maximum(l_ref[h].reshape(g * tq, 1), 1e-30)
            for j in range(g):
                col = (h * g + j) * hd
                o_ref[:, col:col + hd] = o[j * tq:(j + 1) * tq]


def nsa_sel_paged(q_arr, cache, table, sel, bias, k_new, v_new, bias_new, *, pp, tq):
    b = q_arr.shape[0]
    n_pages = table.shape[0] // b
    chunks = n_pages // pp
    n_blk = sel.shape[-1]
    nk = pp * PAGE_SIZE
    kvw = KVH_C * HD_C
    grid_spec = pltpu.PrefetchScalarGridSpec(
        num_scalar_prefetch=1,
        grid=(b * chunks,),
        in_specs=[pl.BlockSpec(memory_space=pl.ANY),
                  pl.BlockSpec((None, tq, D_C), lambda s, tbl: (s // chunks, 0, 0)),
                  pl.BlockSpec((None, KVH_C, tq, n_blk), lambda s, tbl: (s // chunks, 0, 0, 0)),
                  pl.BlockSpec((H_C, tq, nk), lambda s, tbl: (0, 0, s % chunks)),
                  pl.BlockSpec((None, tq, kvw), lambda s, tbl: (s // chunks, 0, 0)),
                  pl.BlockSpec((None, tq, kvw), lambda s, tbl: (s // chunks, 0, 0)),
                  pl.BlockSpec((H_C, tq, tq), lambda s, tbl: (0, 0, 0))],
        out_specs=pl.BlockSpec((None, tq, D_C), lambda s, tbl: (s // chunks, 0, 0)),
        scratch_shapes=[pltpu.VMEM((2, pp, PAGE_ROWS, HD_C), F32), pltpu.SemaphoreType.DMA((2,)),
                        pltpu.VMEM((KVH_C, G_C, tq, 1), F32), pltpu.VMEM((KVH_C, G_C, tq, 1), F32),
                        pltpu.VMEM((KVH_C, G_C * tq, HD_C), F32)],
    )
    vmem = 2 * pp * PAGE_SIZE * 2 * kvw * 4 + 2 * H_C * tq * nk * 4 + 10 * G_C * tq * nk * 4 + n_blk * nk * 4
    return pl.pallas_call(
        functools.partial(_nsa_sel_paged_kernel, pp=pp, chunks=chunks, tq=tq, g=G_C),
        out_shape=jax.ShapeDtypeStruct((b, tq, D_C), F32),
        grid_spec=grid_spec,
        compiler_params=_cparams(("arbitrary",), vmem),
        name="nsa_sel_paged",
    )(table, cache, q_arr, sel, bias, k_new, v_new, bias_new)


def _nsa_combine_kernel(gl_ref, gb_ref, oc_ref, os_ref, ow_ref, o_ref):
    gates = jax.nn.sigmoid(gl_ref[...] + gb_ref[...])
    for h in range(H_C):
        sl = slice(h * HD_C, (h + 1) * HD_C)
        o_ref[:, sl] = (gates[:, h:h + 1] * oc_ref[:, sl] + gates[:, H_C + h:H_C + h + 1] * os_ref[:, sl]
                        + gates[:, 2 * H_C + h:2 * H_C + h + 1] * ow_ref[:, sl]).astype(o_ref.dtype)


def nsa_combine(gate_logits, gate_bias, o_c, o_s, o_w):
    m = o_c.shape[0]
    tm = _row_tile(m, 512)
    row = lambda i: (i, 0)
    vmem = 2 * tm * (128 + 3 * D_C) * 4 + 2 * tm * D_C * 2
    return pl.pallas_call(
        _nsa_combine_kernel,
        out_shape=jax.ShapeDtypeStruct((m, D_C), BF16),
        grid=(m // tm,),
        in_specs=[pl.BlockSpec((tm, 128), row), pl.BlockSpec((1, 128), lambda i: (0, 0)),
                  pl.BlockSpec((tm, D_C), row), pl.BlockSpec((tm, D_C), row), pl.BlockSpec((tm, D_C), row)],
        out_specs=pl.BlockSpec((tm, D_C), row),
        compiler_params=_cparams(("parallel",), vmem),
        name="nsa_combine",
    )(gate_logits, gate_bias, o_c, o_s, o_w)


CMP_PAGES_PER_STEP = 32
SEL_PAGES_PER_STEP = 16
Q_PAD = V7X_SUBLANES


def _pad_rows(a, rows, front=0):
    return jnp.pad(a, ((0, 0), (front, rows - a.shape[1] - front), (0, 0)))


def _memo(W, fn, *args):
    key = (fn.__name__,) + args
    if key not in W["tables"]:
        W["tables"][key] = fn(W["rel_bias"], *args)
    return W["tables"][key]


def _past_bias(rel_bias, tq, past_len):
    dist = past_len + jnp.arange(tq, dtype=jnp.int32)[:, None] - jnp.arange(past_len, dtype=jnp.int32)[None, :]
    return _bias_lookup(rel_bias, dist).transpose(2, 0, 1)


def _mixer_ab(x2, b, t, W, l, i, st):
    prompt = st is None
    proj3 = mm_norm(x2, W["mix_norm"][i], W["ab_w_in"][l], W["ab_bias"][l], tn=1536).reshape(b, t, AB_COLS)
    k_new = proj3[:, :, D_A:D_A + KVH_A * HD_A]
    v_new = proj3[:, :, D_A + KVH_A * HD_A:A_COLS]
    p_b = proj3[:, :, A_COLS:]
    sinks = W["swa_sinks"][l].reshape(H_A, 1, 1)
    if prompt:
        nw = -(-WIN_A // QBLK)
        sk = (nw + 1) * QBLK
        bias = _memo(W, _toeplitz_bias,QBLK, sk, nw * QBLK)
        o_a = band_attn(proj3, _pad_rows(k_new, t + nw * QBLK, nw * QBLK), _pad_rows(v_new, t + nw * QBLK, nw * QBLK),
                        bias, sinks, tq=QBLK, sk=sk, hd=HD_A, n_kvh_step=KVH_A, g=G_A, window=WIN_A,
                        delta=nw * QBLK, kpos_base=-nw * QBLK, kstride=QBLK, out_dtype=BF16)
        wb = min(WIN_A, t)
        buf = jnp.concatenate([k_new[:, t - wb:], v_new[:, t - wb:]], axis=-1)
        y_b, s_new = rwkv_mix(p_b, jnp.zeros((b, 1, B_COLS), F32), jnp.zeros((b, H_B, HD_B, HD_B), F32),
                              W["rwkv"][l], c=RWKV_CHUNK, t_valid=RWKV_CHUNK)
    else:
        past_len = st["past_len"]
        old = st["swa"][l].reshape(b, -1, 2 * KVH_A * HD_A)
        wb = old.shape[1]
        kv_all = jnp.concatenate([old, jnp.concatenate([k_new, v_new], axis=-1)], axis=1)
        buf = kv_all[:, t:]
        sk = -(-(wb + t) // V7X_LANES) * V7X_LANES
        kv_pad = _pad_rows(kv_all, sk)
        bias = _memo(W, _toeplitz_bias,Q_PAD, sk, wb)
        o_a = band_attn(_pad_rows(proj3, Q_PAD), kv_pad[:, :, :KVH_A * HD_A], kv_pad[:, :, KVH_A * HD_A:],
                        bias, sinks, tq=Q_PAD, sk=sk, hd=HD_A, n_kvh_step=KVH_A, g=G_A, window=WIN_A,
                        delta=wb, kpos_base=past_len - wb, kstride=0, out_dtype=BF16)[:, :t]
        y_b, s_new = rwkv_mix(_pad_rows(p_b, Q_PAD), st["shift"][l][:, None], st["wkv"][l],
                              W["rwkv"][l], c=Q_PAD, t_valid=t)
        y_b = y_b[:, :t]
    mix_in = jnp.concatenate([o_a, y_b], axis=-1).reshape(b * t, D_MODEL)
    x2 = mm_res(mix_in, W["ab_w_out"][l], x2)
    buf = buf.reshape(b, -1, 2, KVH_A, HD_A)
    return x2, buf, p_b[:, -1], s_new


def _mixer_c(x2, b, t, W, l, i, st):
    prompt = st is None
    kvw = KVH_C * HD_C
    proj3 = mm_norm(x2, W["mix_norm"][i], W["c_w_in"][l], W["c_zero_bias"], tn=1024).reshape(b, t, C_COLS_PAD)
    kv_cmp_new = proj3[:, :, D_C:D_C + 2 * kvw]
    kv_sel_new = proj3[:, :, D_C + 2 * kvw:D_C + 4 * kvw]
    kv_win_new = proj3[:, :, D_C + 4 * kvw:D_C + 6 * kvw]
    gate_logits = proj3[:, :, D_C + C_KV_COLS:D_C + C_KV_COLS + V7X_LANES].reshape(b * t, V7X_LANES)
    pe, w1, w2 = W["cmp_pe"][l], W["cmp_w1"][l], W["cmp_w2"][l]
    if prompt:
        n_pages = b * t // PAGE_SIZE
        kv_c = nsa_compress(kv_cmp_new.reshape(n_pages, PAGE_ROWS, HD_C), jnp.arange(n_pages, dtype=jnp.int32),
                            pe, w1, w2, pp=min(CMP_PAGES_PER_STEP, n_pages)).reshape(b, t // CMP_BLOCK, 2 * kvw)
        o_c, sel = nsa_cmp(proj3, kv_c, tq=QBLK, q0=0)
        o_s = nsa_sel_prompt(proj3, kv_sel_new[:, :, :kvw], kv_sel_new[:, :, kvw:], sel,
                             _memo(W, sel_bias_tiles, QBLK, t), tq=QBLK)
        nw = -(-WIN_C // QBLK)
        sk = (nw + 1) * QBLK
        bias = _memo(W, _toeplitz_bias,QBLK, sk, nw * QBLK)
        o_w = band_attn(proj3, _pad_rows(kv_win_new[:, :, :kvw], t + nw * QBLK, nw * QBLK),
                        _pad_rows(kv_win_new[:, :, kvw:], t + nw * QBLK, nw * QBLK), bias, None,
                        tq=QBLK, sk=sk, hd=HD_C, n_kvh_step=1, g=G_C, window=WIN_C,
                        delta=nw * QBLK, kpos_base=-nw * QBLK, kstride=QBLK, out_dtype=F32)
        win_buf = kv_win_new[:, t - min(WIN_C, t):]
    else:
        past_len = st["past_len"]
        n_phys = st["cmp"].shape[1]
        table = st["page_table"].reshape(-1) + l * n_phys
        kv_c = nsa_compress(st["cmp"].reshape(-1, PAGE_ROWS, HD_C), table, pe, w1, w2,
                            pp=CMP_PAGES_PER_STEP).reshape(b, past_len // CMP_BLOCK, 2 * kvw)
        q8 = _pad_rows(proj3, Q_PAD)
        o_c, sel = nsa_cmp(q8, kv_c, tq=Q_PAD, q0=past_len)
        bias_past = _memo(W, _past_bias, Q_PAD, past_len)
        bias_new = _memo(W, _toeplitz_bias, Q_PAD, Q_PAD, 0)
        sel_new = _pad_rows(kv_sel_new, Q_PAD)
        o_s = nsa_sel_paged(q8, st["sel"].reshape(-1, PAGE_ROWS, HD_C), table, sel, bias_past,
                            sel_new[:, :, :kvw], sel_new[:, :, kvw:], bias_new, pp=SEL_PAGES_PER_STEP, tq=Q_PAD)
        old = st["win"][l].reshape(b, -1, 2 * kvw)
        wb = old.shape[1]
        kv_all = jnp.concatenate([old, kv_win_new], axis=1)
        win_buf = kv_all[:, t:]
        sk = -(-(wb + t) // V7X_LANES) * V7X_LANES
        kv_pad = _pad_rows(kv_all, sk)
        bias = _memo(W, _toeplitz_bias,Q_PAD, sk, wb)
        o_w = band_attn(q8, kv_pad[:, :, :kvw], kv_pad[:, :, kvw:], bias, None, tq=Q_PAD, sk=sk, hd=HD_C,
                        n_kvh_step=1, g=G_C, window=WIN_C, delta=wb, kpos_base=past_len - wb, kstride=0,
                        out_dtype=F32)
        o_c, o_s, o_w = o_c[:, :t], o_s[:, :t], o_w[:, :t]
    comb = nsa_combine(gate_logits, W["c_gate_b"][l], o_c.reshape(b * t, D_C), o_s.reshape(b * t, D_C),
                       o_w.reshape(b * t, D_C))
    x2 = mm_res(comb, W["c_w_out"][l], x2)
    shape5 = lambda a: a.reshape(b, -1, 2, KVH_C, HD_C)
    return x2, shape5(kv_cmp_new), shape5(kv_sel_new), shape5(win_buf)


def _trunk(x, p, W, st):
    b, t, _ = x.shape
    x2 = x.reshape(b * t, D_MODEL)
    swa_l, shift_l, wkv_l, cmp_l, sel_l, win_l = [], [], [], [], [], []
    for i in range(DEPTH):
        l = i // 2
        if i % 2 == 0:
            x2, buf, shift_new, s_new = _mixer_ab(x2, b, t, W, l, i, st)
            swa_l.append(buf)
            shift_l.append(shift_new)
            wkv_l.append(s_new)
        else:
            x2, cmp_new, sel_new, win_buf = _mixer_c(x2, b, t, W, l, i, st)
            cmp_l.append(cmp_new)
            sel_l.append(sel_new)
            win_l.append(win_buf)
        act = ffn_up(x2, W["ffn_norm"][i], W["ffn_w_gate"][i], W["ffn_w_up"][i])
        x2 = mm_res(act, W["ffn_w_down"][i], x2, tk=D_FF // 4)
        x2 = ple(x2, p[i].reshape(b * t, PLE_DIM), W["ple_gate_norm"][i], W["ple_w_gate"][i], W["ple_w_proj"][i],
                 W["ple_post_norm"][i], W["final_norm"], final=(i == DEPTH - 1))
    y = x2.reshape(b, t, D_MODEL)
    return (y, jnp.stack(swa_l), jnp.stack(shift_l), jnp.stack(wkv_l), jnp.stack(cmp_l), jnp.stack(sel_l),
            jnp.stack(win_l))


def kernel(x_prompt, x_sample, state_swa_kv, state_rwkv_shift, state_rwkv_wkv, cache_nsa_cmp_kv, cache_nsa_sel_kv, state_nsa_win_kv, page_table, p_prompt, p_sample, rel_bias, mix_norm, ab_w_in, ab_b_qkv, swa_sinks, rwkv_mu, rwkv_w0, rwkv_w2, rwkv_a0, rwkv_a2, rwkv_g2, rwkv_k_k, rwkv_k_a, rwkv_r_k, rwkv_ln_g, rwkv_ln_b, ab_w_out, c_w_in, c_gate_b, nsa_cmp_pos, nsa_cmp_w1, nsa_cmp_w2, c_w_out, ffn_norm, ffn_w_gate, ffn_w_up, ffn_w_down, ple_w_proj, ple_gate_norm, ple_w_gate, ple_post_norm, final_norm):
    n_ab, n_c = ab_w_in.shape[0], c_w_in.shape[0]
    bf = lambda a: a.astype(BF16)
    row = lambda a: a.reshape(a.shape[0], 1, -1).astype(F32)
    pe = jnp.broadcast_to(nsa_cmp_pos.transpose(0, 2, 1, 3)[:, :, :, None, :],
                          (n_c, CMP_BLOCK, 2, KVH_C, HD_C)).reshape(n_c, BLOCK_ROWS, HD_C)
    W = dict(
        tables={}, rel_bias=rel_bias,mix_norm=row(mix_norm), ab_w_in=bf(ab_w_in),
        ab_bias=jnp.pad(ab_b_qkv, ((0, 0), (0, AB_COLS - A_COLS))).reshape(n_ab, 1, AB_COLS),
        swa_sinks=swa_sinks, ab_w_out=bf(ab_w_out),
        rwkv=[dict(mu=rwkv_mu[l][None], w0=rwkv_w0[l][None], w2=bf(rwkv_w2[l]), a0=rwkv_a0[l][None], a2=bf(rwkv_a2[l]),
                   g2=bf(rwkv_g2[l]), k_k=rwkv_k_k[l][None], k_a=rwkv_k_a[l][None], r_k=rwkv_r_k[l].reshape(1, C_B),
                   ln_g=rwkv_ln_g[l][None], ln_b=rwkv_ln_b[l][None]) for l in range(n_ab)],
        c_w_in=bf(jnp.pad(c_w_in, ((0, 0), (0, 0), (0, C_COLS_PAD - C_COLS)))),
        c_zero_bias=jnp.zeros((1, C_COLS_PAD), F32),
        c_gate_b=jnp.pad(c_gate_b, ((0, 0), (0, V7X_LANES - 3 * H_C))).reshape(n_c, 1, V7X_LANES),
        cmp_pe=pe, cmp_w1=bf(nsa_cmp_w1), cmp_w2=bf(nsa_cmp_w2), c_w_out=bf(c_w_out),
        ffn_norm=row(ffn_norm), ffn_w_gate=bf(ffn_w_gate), ffn_w_up=bf(ffn_w_up), ffn_w_down=bf(ffn_w_down),
        ple_w_proj=bf(ple_w_proj), ple_gate_norm=row(ple_gate_norm), ple_w_gate=bf(ple_w_gate),
        ple_post_norm=row(ple_post_norm), final_norm=final_norm.reshape(1, D_MODEL),
    )
    st = dict(swa=state_swa_kv, shift=state_rwkv_shift, wkv=state_rwkv_wkv, cmp=cache_nsa_cmp_kv,
              sel=cache_nsa_sel_kv, win=state_nsa_win_kv, page_table=page_table,
              past_len=page_table.shape[1] * PAGE_SIZE)
    y_p, swa_p, shift_p, wkv_p, cmp_p, sel_p, win_p = _trunk(x_prompt, p_prompt, W, None)
    y_s, swa_s, shift_s, wkv_s, cmp_s, sel_s, win_s = _trunk(x_sample, p_sample, W, st)
    return (y_p, y_s, swa_p, swa_s, shift_p, shift_s, wkv_p, wkv_s, cmp_p, cmp_s, sel_p, sel_s, win_p, win_s)
```

```python
import functools
import math

import jax
import jax.numpy as jnp
import numpy as np
from jax import lax
from jax.experimental import pallas as pl
from jax.experimental.pallas import tpu as pltpu

F32 = jnp.float32
BF16 = jnp.bfloat16

D_MODEL = 2048
DEPTH = 4
PAGE_SIZE = 128
PLE_DIM = 256
N_BUCKETS = 32
REL_MAX_DIST = 1024
RMS_EPS = 1e-6
D_FF = 5632
QBLK = 128
HD_A = 64
H_A = 16
KVH_A = 2
G_A = 8
D_A = 1024
WIN_A = 128
A_COLS = D_A + 2 * KVH_A * HD_A
HD_B = 64
C_B = 1024
H_B = 16
LORA_W = 64
LORA_A = 64
LORA_G = 128
B_COLS = 3 * C_B + LORA_W + LORA_A + LORA_G
AB_COLS = A_COLS + B_COLS
GN_EPS = 64e-5
HD_C = 128
H_C = 16
KVH_C = 2
G_C = 8
D_C = 2048
CMP_BLOCK = 64
SEL_BLOCK = 64
SEL_SHIFT = 6
N_TOP = 15
WIN_C = 512
CMP_HIDDEN = 128
C_KV_COLS = 6 * KVH_C * HD_C
C_COLS = D_C + C_KV_COLS + 3 * H_C
NEG_INF = -1e30
FORCE_SCORE = 1e4

V7X_LANES = 128
V7X_SUBLANES = 8
V7X_VMEM_BYTES = 64 * 1024 * 1024
VMEM_LIMIT_CAP = V7X_VMEM_BYTES - 8 * 1024 * 1024

C_COLS_PAD = 4096
RWKV_CHUNK = 64


def _cparams(sem, vmem_bytes):
    limit = int(min(max(2 * vmem_bytes, 32 * 1024 * 1024), VMEM_LIMIT_CAP))
    return pltpu.CompilerParams(dimension_semantics=sem, vmem_limit_bytes=limit)


def _row_tile(m, cap):
    t = min(m, cap)
    assert m % t == 0, (m, t)
    return t


def _rms(x, g):
    return x * lax.rsqrt(jnp.mean(x * x, axis=-1, keepdims=True) + RMS_EPS) * g


def t5_bucket(dist):
    n = jnp.maximum(dist, 0)
    max_exact = N_BUCKETS // 2
    nf = jnp.maximum(n, max_exact).astype(F32)
    large = max_exact + (jnp.log(nf / max_exact) / math.log(REL_MAX_DIST / max_exact) * (N_BUCKETS - max_exact)).astype(jnp.int32)
    return jnp.where(n < max_exact, n, jnp.minimum(large, N_BUCKETS - 1))


def _mm_norm_kernel(x_ref, g_ref, w_ref, b_ref, o_ref, h_ref):
    @pl.when(pl.program_id(1) == 0)
    def _():
        h_ref[...] = _rms(x_ref[...], g_ref[...]).astype(BF16)

    o_ref[...] = jnp.dot(h_ref[...], w_ref[...], preferred_element_type=F32) + b_ref[...]


def mm_norm(x, g, w, b, *, tn=512):
    m, k = x.shape
    n = w.shape[1]
    tm = _row_tile(m, 1024)
    vmem = 2 * tm * k * 4 + tm * k * 2 + 2 * k * tn * 2 + 2 * tm * tn * 4
    return pl.pallas_call(
        _mm_norm_kernel,
        out_shape=jax.ShapeDtypeStruct((m, n), F32),
        grid=(m // tm, n // tn),
        in_specs=[pl.BlockSpec((tm, k), lambda i, j: (i, 0)),
                  pl.BlockSpec((1, k), lambda i, j: (0, 0)),
                  pl.BlockSpec((k, tn), lambda i, j: (0, j)),
                  pl.BlockSpec((1, tn), lambda i, j: (0, j))],
        out_specs=pl.BlockSpec((tm, tn), lambda i, j: (i, j)),
        scratch_shapes=[pltpu.VMEM((tm, k), BF16)],
        compiler_params=_cparams(("parallel", "arbitrary"), vmem),
        name="mm_norm",
    )(x, g, w, b)


def _mm_res_kernel(a_ref, w_ref, r_ref, o_ref, acc_ref, *, nk):
    kk = pl.program_id(2)

    @pl.when(kk == 0)
    def _():
        acc_ref[...] = jnp.zeros_like(acc_ref)

    acc_ref[...] += jnp.dot(a_ref[...], w_ref[...], preferred_element_type=F32)

    @pl.when(kk == nk - 1)
    def _():
        o_ref[...] = r_ref[...] + acc_ref[...]


def mm_res(a, w, r, *, tn=1024, tk=None):
    m, k = a.shape
    n = w.shape[1]
    tm = _row_tile(m, 1024)
    tk = k if tk is None else tk
    nk = k // tk
    vmem = 2 * tm * tk * 2 + 2 * tk * tn * 2 + 5 * tm * tn * 4
    return pl.pallas_call(
        functools.partial(_mm_res_kernel, nk=nk),
        out_shape=jax.ShapeDtypeStruct((m, n), F32),
        grid=(m // tm, n // tn, nk),
        in_specs=[pl.BlockSpec((tm, tk), lambda i, j, q: (i, q)),
                  pl.BlockSpec((tk, tn), lambda i, j, q: (q, j)),
                  pl.BlockSpec((tm, tn), lambda i, j, q: (i, j))],
        out_specs=pl.BlockSpec((tm, tn), lambda i, j, q: (i, j)),
        scratch_shapes=[pltpu.VMEM((tm, tn), F32)],
        compiler_params=_cparams(("parallel", "parallel", "arbitrary"), vmem),
        name="mm_res",
    )(a, w, r)


def _ffn_up_kernel(x_ref, g_ref, wg_ref, wu_ref, o_ref, h_ref):
    @pl.when(pl.program_id(1) == 0)
    def _():
        h_ref[...] = _rms(x_ref[...], g_ref[...]).astype(BF16)

    h = h_ref[...]
    gate = jnp.dot(h, wg_ref[...], preferred_element_type=F32)
    up = jnp.dot(h, wu_ref[...], preferred_element_type=F32)
    o_ref[...] = (jax.nn.silu(gate) * up).astype(BF16)


def ffn_up(x, g, wg, wu, *, tn=512):
    m, k = x.shape
    n = wg.shape[1]
    tm = _row_tile(m, 1024)
    vmem = 2 * tm * k * 4 + tm * k * 2 + 4 * k * tn * 2 + 2 * tm * tn * 2 + 3 * tm * tn * 4
    return pl.pallas_call(
        _ffn_up_kernel,
        out_shape=jax.ShapeDtypeStruct((m, n), BF16),
        grid=(m // tm, n // tn),
        in_specs=[pl.BlockSpec((tm, k), lambda i, j: (i, 0)),
                  pl.BlockSpec((1, k), lambda i, j: (0, 0)),
                  pl.BlockSpec((k, tn), lambda i, j: (0, j)),
                  pl.BlockSpec((k, tn), lambda i, j: (0, j))],
        out_specs=pl.BlockSpec((tm, tn), lambda i, j: (i, j)),
        scratch_shapes=[pltpu.VMEM((tm, k), BF16)],
        compiler_params=_cparams(("parallel", "arbitrary"), vmem),
        name="ffn_up",
    )(x, g, wg, wu)


def _ple_kernel(x_ref, p_ref, gn_ref, wg_ref, wp_ref, pn_ref, fn_ref, o_ref, *, final):
    x = x_ref[...]
    h = _rms(x, gn_ref[...]).astype(BF16)
    gate = jax.nn.sigmoid(jnp.dot(h, wg_ref[...], preferred_element_type=F32))
    e = jnp.dot(p_ref[...].astype(BF16), wp_ref[...], preferred_element_type=F32)
    x = x + _rms(gate * e, pn_ref[...])
    if final:
        x = _rms(x, fn_ref[...])
    o_ref[...] = x


def ple(x, p, gn, wg, wp, pn, fn, *, final):
    m, d = x.shape
    tm = _row_tile(m, 512)
    vmem = 4 * tm * d * 4 + 2 * d * d * 2 + 2 * PLE_DIM * d * 2 + 4 * tm * d * 4
    row = lambda i: (i, 0)
    fix = lambda i: (0, 0)
    return pl.pallas_call(
        functools.partial(_ple_kernel, final=final),
        out_shape=jax.ShapeDtypeStruct((m, d), F32),
        grid=(m // tm,),
        in_specs=[pl.BlockSpec((tm, d), row), pl.BlockSpec((tm, PLE_DIM), row),
                  pl.BlockSpec((1, d), fix), pl.BlockSpec((d, d), fix),
                  pl.BlockSpec((PLE_DIM, d), fix), pl.BlockSpec((1, d), fix),
                  pl.BlockSpec((1, d), fix)],
        out_specs=pl.BlockSpec((tm, d), row),
        compiler_params=_cparams(("parallel",), vmem),
        name="ple",
    )(x, p, gn, wg, wp, pn, fn)


def _masked_softmax(logits, mask, sink=None):
    lf = jnp.where(mask, logits, NEG_INF)
    m = jnp.max(lf, axis=-1, keepdims=True)
    if sink is not None:
        m = jnp.maximum(m, sink)
    e = jnp.where(mask, jnp.exp(lf - m), 0.0)
    den = jnp.sum(e, axis=-1, keepdims=True)
    if sink is not None:
        den = den + jnp.exp(sink - m)
    return e / jnp.maximum(den, 1e-30)


def _stack_heads(q, h0, g, hd):
    return jnp.concatenate([q[:, (h0 + j) * hd:(h0 + j + 1) * hd] for j in range(g)], axis=0)


def _band_attn_kernel(q_ref, k_ref, v_ref, bias_ref, sink_ref, o_ref, *,
                      tq, sk, hd, n_kvh, g, window, delta, kpos_base, kstride, has_sink):
    tl = V7X_LANES
    n_tiles = sk // tl
    ks = pl.multiple_of(pl.program_id(2) * kstride, V7X_SUBLANES)
    kslab = k_ref[pl.ds(ks, sk), :].astype(BF16)
    vslab = v_ref[pl.ds(ks, sk), :].astype(BF16)
    r = lax.broadcasted_iota(jnp.int32, (tq, tl), 0)
    c = lax.broadcasted_iota(jnp.int32, (tq, tl), 1)
    masks = []
    for t in range(n_tiles):
        dist = delta + r - (c + t * tl)
        masks.append(((dist >= 0) & (dist <= window) & (kpos_base + ks + t * tl + c >= 0))[None])
    q = q_ref[...] * (hd ** -0.5)
    chains = range(n_kvh)
    q8 = [_stack_heads(q, h * g, g, hd).astype(BF16) for h in chains]
    lf = [[lax.dot_general(q8[h], kslab[t * tl:(t + 1) * tl, h * hd:(h + 1) * hd], _NT,
                           preferred_element_type=F32).reshape(g, tq, tl)
           + jnp.where(masks[t], bias_ref[h * g:(h + 1) * g, :, t * tl:(t + 1) * tl], NEG_INF)
           for t in range(n_tiles)] for h in chains]
    m = [jnp.max(functools.reduce(jnp.maximum, lf[h]), axis=-1, keepdims=True) for h in chains]
    if has_sink:
        m = [jnp.maximum(m[h], sink_ref[h * g:(h + 1) * g]) for h in chains]
    mfull = [jnp.broadcast_to(m[h], (g, tq, tl)) for h in chains]
    e = [[jnp.exp(lf[h][t] - mfull[h]) for t in range(n_tiles)] for h in chains]
    acc = [functools.reduce(jnp.add, [jnp.dot(e[h][t].reshape(g * tq, tl).astype(BF16),
                                              vslab[t * tl:(t + 1) * tl, h * hd:(h + 1) * hd],
                                              preferred_element_type=F32) for t in range(n_tiles)])
           for h in chains]
    den = [jnp.sum(functools.reduce(jnp.add, e[h]), axis=-1, keepdims=True) for h in chains]
    if has_sink:
        den = [den[h] + jnp.exp(sink_ref[h * g:(h + 1) * g] - m[h]) for h in chains]
    for h in chains:
        o = acc[h] / jnp.maximum(den[h].reshape(g * tq, 1), 1e-30)
        for j in range(g):
            o_ref[:, (h * g + j) * hd:(h * g + j + 1) * hd] = o[j * tq:(j + 1) * tq].astype(o_ref.dtype)


def band_attn(q_arr, kv_arr, bias, sink, *, tq, sk, hd, n_kvh_step, g, window, delta,
              kpos_base, kstride, out_dtype):
    b, t = q_arr.shape[:2]
    tk = kv_arr.shape[1]
    n_kv_blocks = kv_arr.shape[2] // (2 * n_kvh_step * hd)
    qw = n_kvh_step * g * hd
    has_sink = sink is not None
    if not has_sink:
        sink = jnp.zeros((n_kv_blocks * n_kvh_step * g, 1, 1), F32)
    vmem = 2 * tq * qw * 4 * 2 + 4 * tk * n_kvh_step * hd * 4 + 2 * n_kvh_step * g * tq * sk * 4 + 6 * g * tq * sk * 4
    kern = functools.partial(_band_attn_kernel, tq=tq, sk=sk, hd=hd, n_kvh=n_kvh_step, g=g, window=window,
                             delta=delta, kpos_base=kpos_base, kstride=kstride, has_sink=has_sink)
    return pl.pallas_call(
        kern,
        out_shape=jax.ShapeDtypeStruct((b, t, n_kv_blocks * qw), out_dtype),
        grid=(b, n_kv_blocks, t // tq),
        in_specs=[pl.BlockSpec((None, tq, qw), lambda bb, kv, i: (bb, i, kv)),
                  pl.BlockSpec((None, tk, n_kvh_step * hd), lambda bb, kv, i: (bb, 0, kv)),
                  pl.BlockSpec((None, tk, n_kvh_step * hd), lambda bb, kv, i: (bb, 0, n_kv_blocks + kv)),
                  pl.BlockSpec((n_kvh_step * g, tq, sk), lambda bb, kv, i: (kv, 0, 0)),
                  pl.BlockSpec((n_kvh_step * g, 1, 1), lambda bb, kv, i: (kv, 0, 0))],
        out_specs=pl.BlockSpec((None, tq, qw), lambda bb, kv, i: (bb, i, kv)),
        compiler_params=_cparams(("parallel", "parallel", "arbitrary"), vmem),
        name="band_attn",
    )(q_arr, kv_arr, kv_arr, bias, sink)


def _bias_lookup(rel_bias, dist):
    onehot = (t5_bucket(dist)[..., None] == jnp.arange(N_BUCKETS, dtype=jnp.int32)).astype(F32)
    return jnp.einsum("...k,kh->...h", onehot, rel_bias.astype(F32), precision=lax.Precision.HIGHEST)


def _toeplitz_bias(rel_bias, tq, sk, delta):
    dist = delta + jnp.arange(tq, dtype=jnp.int32)[:, None] - jnp.arange(sk, dtype=jnp.int32)[None, :]
    return _bias_lookup(rel_bias, dist).transpose(2, 0, 1)


def _nsa_cmp_kernel(q_ref, kv_ref, oc_ref, sel_ref, *, tq, n_cmp, g, q0):
    hd = HD_C
    qp = q0 + pl.program_id(1) * tq + lax.broadcasted_iota(jnp.int32, (tq, n_cmp), 0)
    blk = lax.broadcasted_iota(jnp.int32, (tq, n_cmp), 1)
    cmask = ((blk + 1) * CMP_BLOCK <= qp + 1)[None]
    cur = qp >> SEL_SHIFT
    groups = range(KVH_C)
    q = q_ref[...]
    scores = []
    for h in groups:
        q8 = _stack_heads(q, h * g, g, hd)
        kc = kv_ref[:, h * hd:(h + 1) * hd]
        vc = kv_ref[:, (KVH_C + h) * hd:(KVH_C + h + 1) * hd].astype(BF16)
        q_hi = q8.astype(BF16)
        q_lo = (q8 - q_hi.astype(F32)).astype(BF16)
        k_hi = kc.astype(BF16)
        k_lo = (kc - k_hi.astype(F32)).astype(BF16)
        cl = (lax.dot_general(q_hi, k_hi, _NT, preferred_element_type=F32)
              + lax.dot_general(q_hi, k_lo, _NT, preferred_element_type=F32)
              + lax.dot_general(q_lo, k_hi, _NT, preferred_element_type=F32)) * (hd ** -0.5)
        p_c = _masked_softmax(cl.reshape(g, tq, n_cmp), cmask)
        o = jnp.dot(p_c.reshape(g * tq, n_cmp).astype(BF16), vc, preferred_element_type=F32)
        for j in range(g):
            oc_ref[:, (h * g + j) * hd:(h * g + j + 1) * hd] = o[j * tq:(j + 1) * tq]
        imp = jnp.sum(p_c, axis=0) + jnp.where(blk == 0, FORCE_SCORE, 0.0)
        scores.append(jnp.where(blk < cur, imp, -1.0))
    sels = [jnp.where(blk == cur, 1.0, 0.0) for _ in groups]
    for _ in range(N_TOP):
        tops = [jnp.max(scores[h], axis=-1, keepdims=True) for h in groups]
        hits = [blk == jnp.min(jnp.where(scores[h] == tops[h], blk, n_cmp), axis=-1, keepdims=True) for h in groups]
        sels = [jnp.where(hits[h] & (scores[h] >= 0.0), 1.0, sels[h]) for h in groups]
        scores = [jnp.where(hits[h], -2.0, scores[h]) for h in groups]
    for h in groups:
        sel_ref[h] = sels[h]


def nsa_cmp(q_arr, kv_c, *, tq, q0):
    b, t = q_arr.shape[:2]
    n_cmp = kv_c.shape[1]
    assert n_cmp >= N_TOP
    kvw = 2 * KVH_C * HD_C
    vmem = 4 * tq * D_C * 4 + 2 * n_cmp * kvw * 4 + 16 * G_C * tq * n_cmp * 4
    return pl.pallas_call(
        functools.partial(_nsa_cmp_kernel, tq=tq, n_cmp=n_cmp, g=G_C, q0=q0),
        out_shape=(jax.ShapeDtypeStruct((b, t, D_C), F32),
                   jax.ShapeDtypeStruct((b, KVH_C, t, n_cmp), F32)),
        grid=(b, t // tq),
        in_specs=[pl.BlockSpec((None, tq, D_C), lambda bb, i: (bb, i, 0)),
                  pl.BlockSpec((None, n_cmp, kvw), lambda bb, i: (bb, 0, 0))],
        out_specs=(pl.BlockSpec((None, tq, D_C), lambda bb, i: (bb, i, 0)),
                   pl.BlockSpec((None, KVH_C, tq, n_cmp), lambda bb, i: (bb, 0, i, 0))),
        compiler_params=_cparams(("parallel", "arbitrary"), vmem),
        name="nsa_cmp",
    )(q_arr, kv_c)


SEL_TILES_PER_TRIP = 4


def _nsa_sel_kernel(q_ref, k_ref, v_ref, sel_ref, bias_ref, o_ref, selk_ref, mx_ref, le_ref, acc_ref, lf_ref, *,
                    tq, t, g, n_far):
    hd = HD_C
    i = pl.program_id(2)
    n_blk = t // SEL_BLOCK
    eb = lax.broadcasted_iota(jnp.int32, (n_blk, t), 0)
    ek = lax.broadcasted_iota(jnp.int32, (n_blk, t), 1)
    expand = jnp.where((ek >> SEL_SHIFT) == eb, 1.0, 0.0).astype(BF16)
    selk_ref[...] = jnp.dot(sel_ref[...].astype(BF16), expand, preferred_element_type=F32)
    q8 = (_stack_heads(q_ref[...], 0, g, hd) * (hd ** -0.5)).astype(BF16)
    r = lax.broadcasted_iota(jnp.int32, (tq, tq), 0)
    c = lax.broadcasted_iota(jnp.int32, (tq, tq), 1)

    last = t // tq - 1

    def tile_start(j):
        return pl.multiple_of(jnp.minimum(j, last) * tq, tq)

    def logits(j):
        ks = tile_start(j)
        kj = k_ref[pl.ds(ks, tq), :].astype(BF16)
        s = lax.dot_general(q8, kj, _NT, preferred_element_type=F32).reshape(g, tq, tq)
        mask = ((selk_ref[:, pl.ds(ks, tq)] > 0.5) & ((j - i) * tq + c <= r))[None]
        return s + jnp.where(mask, bias_ref[jnp.clip(i - j, 0, n_far)], NEG_INF)

    per = SEL_TILES_PER_TRIP
    n_trips = (i + per) // per
    mx_ref[...] = jnp.full(mx_ref.shape, NEG_INF, F32)

    def sweep_max(jj, carry):
        tiles = [logits(per * jj + u) for u in range(per)]
        for u in range(per):
            lf_ref[jnp.minimum(per * jj + u, last)] = tiles[u]
        mx_ref[...] = jnp.maximum(mx_ref[...], functools.reduce(jnp.maximum, tiles))
        return carry

    lax.fori_loop(0, n_trips, sweep_max, 0)
    mx_ref[...] = jnp.broadcast_to(jnp.max(mx_ref[...], axis=-1, keepdims=True), mx_ref.shape)
    le_ref[...] = jnp.zeros(le_ref.shape, F32)
    acc_ref[...] = jnp.zeros(acc_ref.shape, F32)

    def sweep_acc(jj, carry):
        mx = mx_ref[...]
        es = [jnp.exp(lf_ref[jnp.minimum(per * jj + u, last)] - mx) for u in range(per)]
        le_ref[...] += functools.reduce(jnp.add, es)
        pvs = [jnp.dot(es[u].reshape(g * tq, tq).astype(BF16),
                       v_ref[pl.ds(tile_start(per * jj + u), tq), :].astype(BF16), preferred_element_type=F32)
               for u in range(per)]
        acc_ref[...] += functools.reduce(jnp.add, pvs)
        return carry

    lax.fori_loop(0, n_trips, sweep_acc, 0)
    den = jnp.sum(le_ref[...], axis=-1, keepdims=True).reshape(g * tq, 1)
    o = acc_ref[...] / jnp.maximum(den, 1e-30)
    for j in range(g):
        o_ref[:, j * hd:(j + 1) * hd] = o[j * tq:(j + 1) * tq]


def _bias_saturation_offset(tq, t):
    d = np.arange(0, t + tq, dtype=np.float64)
    nf = np.maximum(d, N_BUCKETS // 2)
    large = N_BUCKETS // 2 + np.floor(np.log(nf / (N_BUCKETS // 2)) / math.log(REL_MAX_DIST / (N_BUCKETS // 2))
                                      * (N_BUCKETS - N_BUCKETS // 2) - 1e-3)
    saturated = np.where(d < N_BUCKETS // 2, 0, large) >= N_BUCKETS - 1
    if not saturated.any():
        return t // tq
    first_sat = int(np.argmax(saturated))
    return min(t // tq, -(-(first_sat + tq) // tq))


def sel_bias_tiles(rel_bias, tq, t):
    n_far = _bias_saturation_offset(tq, t)
    dist = (jnp.arange(n_far + 1, dtype=jnp.int32)[:, None, None] * tq
            + jnp.arange(tq, dtype=jnp.int32)[None, :, None] - jnp.arange(tq, dtype=jnp.int32)[None, None, :])
    tiles = _bias_lookup(rel_bias, dist)
    return tiles.reshape(n_far + 1, tq, tq, KVH_C, G_C).transpose(3, 0, 4, 1, 2)


def nsa_sel_prompt(q_arr, k_arr, v_arr, sel, tiles, *, tq):
    b, t = q_arr.shape[:2]
    n_blk = t // SEL_BLOCK
    n_far = tiles.shape[1] - 1
    n_tiles = t // tq
    assert n_tiles % SEL_TILES_PER_TRIP == 0
    qw = G_C * HD_C
    lf_bytes = n_tiles * G_C * tq * tq * 4
    vmem = (4 * tq * qw * 4 + 4 * t * HD_C * 4 + 2 * (n_far + 1) * G_C * tq * tq * 4 + tq * t * 4
            + 8 * G_C * tq * tq * 4 + n_blk * t * 4 + lf_bytes)
    return pl.pallas_call(
        functools.partial(_nsa_sel_kernel, tq=tq, t=t, g=G_C, n_far=n_far),
        out_shape=jax.ShapeDtypeStruct((b, t, D_C), F32),
        grid=(b, KVH_C, t // tq),
        in_specs=[pl.BlockSpec((None, tq, qw), lambda bb, kv, i: (bb, i, kv)),
                  pl.BlockSpec((None, t, HD_C), lambda bb, kv, i: (bb, 0, kv)),
                  pl.BlockSpec((None, t, HD_C), lambda bb, kv, i: (bb, 0, kv)),
                  pl.BlockSpec((None, None, tq, n_blk), lambda bb, kv, i: (bb, kv, i, 0)),
                  pl.BlockSpec((None, n_far + 1, G_C, tq, tq), lambda bb, kv, i: (kv, 0, 0, 0, 0))],
        out_specs=pl.BlockSpec((None, tq, qw), lambda bb, kv, i: (bb, i, kv)),
        scratch_shapes=[pltpu.VMEM((tq, t), F32), pltpu.VMEM((G_C, tq, tq), F32),
                        pltpu.VMEM((G_C, tq, tq), F32), pltpu.VMEM((G_C * tq, HD_C), F32),
                        pltpu.VMEM((n_tiles, G_C, tq, tq), F32)],
        compiler_params=_cparams(("parallel", "parallel", "arbitrary"), vmem),
        name="nsa_sel_prompt",
    )(q_arr, k_arr, v_arr, sel, tiles)


def _bdot(a, b, dims=(((1,), (0,)), ((), ()))):
    return lax.dot_general(a.astype(BF16), b.astype(BF16), dims, preferred_element_type=F32)


_NT = (((1,), (1,)), ((), ()))
_TN = (((0,), (0,)), ((), ()))


def _rwkv_kernel(p_ref, shift_ref, s0_ref, mu_ref, w0_ref, w2_ref, a0_ref, a2_ref, g2_ref, kk_ref, ka_ref,
                 rk_ref, lng_ref, lnb_ref, y_ref, sout_ref, carry_ref, state_ref, *, c, t_valid, n_chunks):
    ci = pl.program_id(1)

    @pl.when(ci == 0)
    def _():
        carry_ref[...] = shift_ref[...]
        state_ref[...] = s0_ref[...]

    p = p_ref[...]
    row = lax.broadcasted_iota(jnp.int32, (c, 1), 0)
    prev = jnp.where(row == 0, carry_ref[...], pltpu.roll(p, 1, axis=0))
    carry_ref[...] = p[c - 1:c, :]
    xs = p + (prev - p) * mu_ref[...]
    o = 3 * C_B
    r = xs[:, :C_B]
    k = xs[:, C_B:2 * C_B]
    v = xs[:, 2 * C_B:o]
    wd = xs[:, o:o + LORA_W]
    ad = xs[:, o + LORA_W:o + LORA_W + LORA_A]
    gd = xs[:, o + LORA_W + LORA_A:]
    w_raw = w0_ref[...] + _bdot(jnp.tanh(wd), w2_ref[...])
    logd = -jnp.exp(-jax.nn.softplus(-w_raw) - 0.5)
    a = jax.nn.sigmoid(a0_ref[...] + _bdot(ad, a2_ref[...]))
    gate = _bdot(jax.nn.sigmoid(gd), g2_ref[...])
    kk = k * kk_ref[...]
    k = k * (1.0 + (a - 1.0) * ka_ref[...])
    if t_valid < c:
        valid = row < t_valid
        logd = jnp.where(valid, logd, 0.0)
        r = jnp.where(valid, r, 0.0)
        k = jnp.where(valid, k, 0.0)
        v = jnp.where(valid, v, 0.0)
        kk = jnp.where(valid, kk, 0.0)
    ti = lax.broadcasted_iota(jnp.int32, (c, c), 0)
    si = lax.broadcasted_iota(jnp.int32, (c, c), 1)
    incl = si <= ti
    strict = si < ti
    tri = jnp.where(incl, 1.0, 0.0).astype(BF16)
    hi = logd.astype(BF16)
    rem = logd - hi.astype(F32)
    mid = rem.astype(BF16)
    lo = (rem - mid.astype(F32)).astype(BF16)
    cs = (jnp.dot(tri, hi, preferred_element_type=F32) + jnp.dot(tri, mid, preferred_element_type=F32)
          + jnp.dot(tri, lo, preferred_element_type=F32))
    e_pos = jnp.exp(cs)
    e_prev = jnp.exp(cs - logd)
    e_neg = jnp.exp(-cs)
    rk = r * k * rk_ref[...]
    n_levels = int(math.log2(c))
    assert 2 ** n_levels == c
    heads = range(H_B)
    sls = [slice(h * HD_B, (h + 1) * HD_B) for h in heads]
    kkn = []
    for sl in sls:
        kk_h = kk[:, sl]
        kkn.append(kk_h / jnp.maximum(jnp.sqrt(jnp.sum(kk_h * kk_h, axis=-1, keepdims=True)), 1e-12))
    v_h = [v[:, sl].astype(BF16) for sl in sls]
    lhs = [jnp.concatenate([-kkn[h] * e_prev[:, sls[h]], r[:, sls[h]] * e_pos[:, sls[h]]], axis=0).astype(BF16)
           for h in heads]
    rhs = [jnp.concatenate([kkn[h] * a[:, sls[h]] * e_neg[:, sls[h]], k[:, sls[h]] * e_neg[:, sls[h]]],
                           axis=0).astype(BF16) for h in heads]
    s0 = [state_ref[h] for h in heads]
    mm = [_bdot(lhs[h], rhs[h], _NT) for h in heads]
    ars = [_bdot(lhs[h], s0[h], _NT) for h in heads]
    u = [ars[h][:c] + _bdot(jnp.where(strict, mm[h][:c, c:], 0.0), v_h[h]) for h in heads]
    lp = [jnp.where(strict, mm[h][:c, :c], 0.0) for h in heads]
    for lvl in range(n_levels):
        u = [u[h] + _bdot(lp[h], u[h]) for h in heads]
        if lvl < n_levels - 1:
            lp = [_bdot(lp[h], lp[h]) for h in heads]
    uv = [jnp.concatenate([u[h].astype(BF16), v_h[h]], axis=0) for h in heads]
    t2 = lax.broadcasted_iota(jnp.int32, (c, 2 * c), 0)
    s2 = lax.broadcasted_iota(jnp.int32, (c, 2 * c), 1)
    incl2 = jnp.where(s2 >= c, s2 - c, s2) <= t2
    y = [ars[h][c:] + _bdot(jnp.where(incl2, mm[h][c:], 0.0), uv[h]) for h in heads]
    for h in heads:
        state_ref[h] = (s0[h] + _bdot(uv[h], rhs[h], _TN)) * e_pos[c - 1:c, sls[h]]
    outs = []
    for h in heads:
        sl = sls[h]
        mean = jnp.mean(y[h], axis=-1, keepdims=True)
        var = jnp.mean(jnp.square(y[h] - mean), axis=-1, keepdims=True)
        yn = (y[h] - mean) * lax.rsqrt(var + GN_EPS) * lng_ref[:, sl] + lnb_ref[:, sl]
        bonus = jnp.sum(rk[:, sl], axis=-1, keepdims=True) * v[:, sl]
        outs.append((yn + bonus) * gate[:, sl])
    y_ref[...] = jnp.concatenate(outs, axis=-1).astype(y_ref.dtype)

    @pl.when(ci == n_chunks - 1)
    def _():
        sout_ref[...] = state_ref[...]


def rwkv_mix(p, shift0, s0, w, *, c, t_valid):
    b, t, _ = p.shape
    n_chunks = t // c
    fix2 = lambda bb, ci: (0, 0)
    vec = lambda n: pl.BlockSpec((1, n), fix2)
    vmem = 6 * c * B_COLS * 4 + 4 * H_B * HD_B * HD_B * 4 + 40 * c * C_B * 4 + (LORA_W + LORA_A + LORA_G) * C_B * 4
    return pl.pallas_call(
        functools.partial(_rwkv_kernel, c=c, t_valid=t_valid, n_chunks=n_chunks),
        out_shape=(jax.ShapeDtypeStruct((b, t, C_B), BF16),
                   jax.ShapeDtypeStruct((b, H_B, HD_B, HD_B), F32)),
        grid=(b, n_chunks),
        in_specs=[pl.BlockSpec((None, c, B_COLS), lambda bb, ci: (bb, ci, 0)),
                  pl.BlockSpec((None, 1, B_COLS), lambda bb, ci: (bb, 0, 0)),
                  pl.BlockSpec((None, H_B, HD_B, HD_B), lambda bb, ci: (bb, 0, 0, 0)),
                  vec(B_COLS), vec(C_B), pl.BlockSpec((LORA_W, C_B), fix2),
                  vec(C_B), pl.BlockSpec((LORA_A, C_B), fix2), pl.BlockSpec((LORA_G, C_B), fix2),
                  vec(C_B), vec(C_B), vec(C_B), vec(C_B), vec(C_B)],
        out_specs=(pl.BlockSpec((None, c, C_B), lambda bb, ci: (bb, ci, 0)),
                   pl.BlockSpec((None, H_B, HD_B, HD_B), lambda bb, ci: (bb, 0, 0, 0))),
        scratch_shapes=[pltpu.VMEM((1, B_COLS), F32), pltpu.VMEM((H_B, HD_B, HD_B), F32)],
        compiler_params=_cparams(("parallel", "arbitrary"), vmem),
        name="rwkv_mix",
    )(p, shift0, s0, w["mu"], w["w0"], w["w2"], w["a0"], w["a2"], w["g2"], w["k_k"], w["k_a"],
      w["r_k"], w["ln_g"], w["ln_b"])


PAGE_CH = 2 * KVH_C
PAGE_ROWS = PAGE_SIZE * PAGE_CH


def _stream_pages(make_copies):
    step = pl.program_id(0)
    n_steps = pl.num_programs(0)
    slot = step % 2

    @pl.when(step == 0)
    def _():
        for cp in make_copies(step, slot):
            cp.start()

    @pl.when(step + 1 < n_steps)
    def _():
        for cp in make_copies(step + 1, 1 - slot):
            cp.start()

    for cp in make_copies(step, slot):
        cp.wait()
    return slot


BLOCK_ROWS = CMP_BLOCK * PAGE_CH


def _compress_kernel(table_ref, cache_ref, pe_ref, w1_ref, w2_ref, o_ref, buf_ref, sem_ref, *, pp):
    def make_copies(step, slot):
        out = []
        for k in range(pp):
            page = table_ref[step * pp + k]
            for n in range(2):
                out.append(pltpu.make_async_copy(cache_ref.at[page, pl.ds(n * BLOCK_ROWS, BLOCK_ROWS), :],
                                                 buf_ref.at[slot, :, 2 * k + n, :], sem_ref.at[slot]))
        return out

    slot = _stream_pages(make_copies)
    nr = 2 * pp
    for cc in range(2):
        cols = []
        for pos in range(CMP_BLOCK):
            q0 = pos * PAGE_CH + cc * KVH_C
            cols.append(jnp.concatenate([(buf_ref[slot, q0 + h] + pe_ref[q0 + h:q0 + h + 1, :]).astype(BF16)
                                         for h in range(KVH_C)], axis=0))
        flat = jnp.concatenate(cols, axis=1)
        acc = jnp.dot(flat, w1_ref[cc], preferred_element_type=F32)
        res = jnp.dot(jax.nn.gelu(acc).astype(BF16), w2_ref[cc], preferred_element_type=F32)
        for h in range(KVH_C):
            col = (cc * KVH_C + h) * HD_C
            o_ref[:, col:col + HD_C] = res[h * nr:(h + 1) * nr]


def nsa_compress(cache, table, pe, w1, w2, *, pp):
    n_pages = table.shape[0]
    assert n_pages % pp == 0
    nr = 2 * pp
    grid_spec = pltpu.PrefetchScalarGridSpec(
        num_scalar_prefetch=1,
        grid=(n_pages // pp,),
        in_specs=[pl.BlockSpec(memory_space=pl.ANY),
                  pl.BlockSpec((BLOCK_ROWS, HD_C), lambda s, tbl: (0, 0)),
                  pl.BlockSpec((2, CMP_BLOCK * HD_C, CMP_HIDDEN), lambda s, tbl: (0, 0, 0)),
                  pl.BlockSpec((2, CMP_HIDDEN, HD_C), lambda s, tbl: (0, 0, 0))],
        out_specs=pl.BlockSpec((nr, PAGE_CH * HD_C), lambda s, tbl: (s, 0)),
        scratch_shapes=[pltpu.VMEM((2, BLOCK_ROWS, nr, HD_C), F32), pltpu.SemaphoreType.DMA((2,))],
    )
    vmem = 2 * BLOCK_ROWS * nr * HD_C * 4 + 4 * CMP_BLOCK * HD_C * CMP_HIDDEN * 2 + 8 * nr * 512 * 4
    return pl.pallas_call(
        functools.partial(_compress_kernel, pp=pp),
        out_shape=jax.ShapeDtypeStruct((2 * n_pages, PAGE_CH * HD_C), F32),
        grid_spec=grid_spec,
        compiler_params=_cparams(("arbitrary",), vmem),
        name="nsa_compress",
    )(table, cache, pe, w1, w2)


def _nsa_sel_paged_kernel(table_ref, cache_ref, q_ref, sel_ref, bias_ref, knew_ref, vnew_ref, bnew_ref, o_ref,
                          buf_ref, sem_ref, m_ref, l_ref, acc_ref, *, pp, chunks, tq, g):
    hd = HD_C

    def make_copies(step, slot):
        return [pltpu.make_async_copy(cache_ref.at[table_ref[step * pp + k]], buf_ref.at[slot, k], sem_ref.at[slot])
                for k in range(pp)]

    slot = _stream_pages(make_copies)

    def page_rows(ch):
        return jnp.concatenate([buf_ref[slot, k, pl.ds(ch, PAGE_SIZE, stride=PAGE_CH), :] for k in range(pp)], axis=0)

    chunk = pl.program_id(0) % chunks
    nk = pp * PAGE_SIZE
    n_blk = sel_ref.shape[-1]
    scale = hd ** -0.5

    @pl.when(chunk == 0)
    def _():
        m_ref[...] = jnp.full(m_ref.shape, NEG_INF, F32)
        l_ref[...] = jnp.zeros(l_ref.shape, F32)
        acc_ref[...] = jnp.zeros(acc_ref.shape, F32)

    eb = lax.broadcasted_iota(jnp.int32, (n_blk, nk), 0)
    ek = lax.broadcasted_iota(jnp.int32, (n_blk, nk), 1)
    expand = jnp.where(eb == (ek >> SEL_SHIFT), 1.0, 0.0).astype(BF16)
    q = q_ref[...]

    groups = range(KVH_C)
    q8 = [(_stack_heads(q, h * g, g, hd) * scale).astype(BF16) for h in groups]

    def update(lf, vv):
        m_old = [m_ref[h] for h in groups]
        m_new = [jnp.maximum(m_old[h], jnp.max(lf[h], axis=-1, keepdims=True)) for h in groups]
        e = [jnp.exp(lf[h] - m_new[h]) for h in groups]
        alpha = [jnp.exp(m_old[h] - m_new[h]) for h in groups]
        pv = [jnp.dot(e[h].reshape(g * tq, -1).astype(BF16), vv[h], preferred_element_type=F32) for h in groups]
        for h in groups:
            l_ref[h] = alpha[h] * l_ref[h] + jnp.sum(e[h], axis=-1, keepdims=True)
            acc_ref[h] = alpha[h].reshape(g * tq, 1) * acc_ref[h] + pv[h]
            m_ref[h] = m_new[h]

    kk = [page_rows(h).astype(BF16) for h in groups]
    vv = [page_rows(KVH_C + h).astype(BF16) for h in groups]
    selk = [jnp.dot(sel_ref[h].astype(BF16), expand, preferred_element_type=F32) for h in groups]
    lf = [lax.dot_general(q8[h], kk[h], _NT, preferred_element_type=F32).reshape(g, tq, nk)
          + jnp.where((selk[h] > 0.5)[None], bias_ref[h * g:(h + 1) * g], NEG_INF) for h in groups]
    update(lf, vv)

    @pl.when(chunk == chunks - 1)
    def _():
        r = lax.broadcasted_iota(jnp.int32, (tq, tq), 0)
        c = lax.broadcasted_iota(jnp.int32, (tq, tq), 1)
        kn = [knew_ref[:, h * hd:(h + 1) * hd].astype(BF16) for h in groups]
        vn = [vnew_ref[:, h * hd:(h + 1) * hd].astype(BF16) for h in groups]
        lf_new = [lax.dot_general(q8[h], kn[h], _NT, preferred_element_type=F32).reshape(g, tq, tq)
                  + jnp.where((c <= r)[None], bnew_ref[h * g:(h + 1) * g], NEG_INF) for h in groups]
        update(lf_new, vn)
        for h in groups:
            o = acc_ref[h] / jnp.maximum(l_ref[h].reshape(g * tq, 1), 1e-30)
            for j in range(g):
                col = (h * g + j) * hd
                o_ref[:, col:col + hd] = o[j * tq:(j + 1) * tq]


def nsa_sel_paged(q_arr, cache, table, sel, bias, k_new, v_new, bias_new, *, pp, tq):
    b = q_arr.shape[0]
    n_pages = table.shape[0] // b
    chunks = n_pages // pp
    nk = pp * PAGE_SIZE
    n_blk = nk // SEL_BLOCK
    kvw = KVH_C * HD_C
    sel = sel.reshape(b, KVH_C, tq, chunks, n_blk).transpose(0, 3, 1, 2, 4)
    grid_spec = pltpu.PrefetchScalarGridSpec(
        num_scalar_prefetch=1,
        grid=(b * chunks,),
        in_specs=[pl.BlockSpec(memory_space=pl.ANY),
                  pl.BlockSpec((None, tq, D_C), lambda s, tbl: (s // chunks, 0, 0)),
                  pl.BlockSpec((None, None, KVH_C, tq, n_blk), lambda s, tbl: (s // chunks, s % chunks, 0, 0, 0)),
                  pl.BlockSpec((H_C, tq, nk), lambda s, tbl: (0, 0, s % chunks)),
                  pl.BlockSpec((None, tq, kvw), lambda s, tbl: (s // chunks, 0, 0)),
                  pl.BlockSpec((None, tq, kvw), lambda s, tbl: (s // chunks, 0, 0)),
                  pl.BlockSpec((H_C, tq, tq), lambda s, tbl: (0, 0, 0))],
        out_specs=pl.BlockSpec((None, tq, D_C), lambda s, tbl: (s // chunks, 0, 0)),
        scratch_shapes=[pltpu.VMEM((2, pp, PAGE_ROWS, HD_C), F32), pltpu.SemaphoreType.DMA((2,)),
                        pltpu.VMEM((KVH_C, G_C, tq, 1), F32), pltpu.VMEM((KVH_C, G_C, tq, 1), F32),
                        pltpu.VMEM((KVH_C, G_C * tq, HD_C), F32)],
    )
    vmem = 2 * pp * PAGE_SIZE * 2 * kvw * 4 + 2 * H_C * tq * nk * 4 + 10 * G_C * tq * nk * 4 + n_blk * nk * 4
    return pl.pallas_call(
        functools.partial(_nsa_sel_paged_kernel, pp=pp, chunks=chunks, tq=tq, g=G_C),
        out_shape=jax.ShapeDtypeStruct((b, tq, D_C), F32),
        grid_spec=grid_spec,
        compiler_params=_cparams(("arbitrary",), vmem),
        name="nsa_sel_paged",
    )(table, cache, q_arr, sel, bias, k_new, v_new, bias_new)


def _nsa_combine_kernel(gl_ref, gb_ref, oc_ref, os_ref, ow_ref, o_ref):
    gates = jax.nn.sigmoid(gl_ref[...] + gb_ref[...])
    for h in range(H_C):
        sl = slice(h * HD_C, (h + 1) * HD_C)
        o_ref[:, sl] = (gates[:, h:h + 1] * oc_ref[:, sl] + gates[:, H_C + h:H_C + h + 1] * os_ref[:, sl]
                        + gates[:, 2 * H_C + h:2 * H_C + h + 1] * ow_ref[:, sl]).astype(o_ref.dtype)


def nsa_combine(gate_logits, gate_bias, o_c, o_s, o_w):
    m = o_c.shape[0]
    tm = _row_tile(m, 512)
    row = lambda i: (i, 0)
    vmem = 2 * tm * (128 + 3 * D_C) * 4 + 2 * tm * D_C * 2
    return pl.pallas_call(
        _nsa_combine_kernel,
        out_shape=jax.ShapeDtypeStruct((m, D_C), BF16),
        grid=(m // tm,),
        in_specs=[pl.BlockSpec((tm, 128), row), pl.BlockSpec((1, 128), lambda i: (0, 0)),
                  pl.BlockSpec((tm, D_C), row), pl.BlockSpec((tm, D_C), row), pl.BlockSpec((tm, D_C), row)],
        out_specs=pl.BlockSpec((tm, D_C), row),
        compiler_params=_cparams(("parallel",), vmem),
        name="nsa_combine",
    )(gate_logits, gate_bias, o_c, o_s, o_w)


CMP_PAGES_PER_STEP = 32
SEL_PAGES_PER_STEP = 16
Q_PAD = V7X_SUBLANES


def _pad_rows(a, rows, front=0):
    return jnp.pad(a, ((0, 0), (front, rows - a.shape[1] - front), (0, 0)))


def _memo(W, fn, *args):
    key = (fn.__name__,) + args
    if key not in W["tables"]:
        W["tables"][key] = fn(W["rel_bias"], *args)
    return W["tables"][key]


def _past_bias(rel_bias, tq, past_len):
    dist = past_len + jnp.arange(tq, dtype=jnp.int32)[:, None] - jnp.arange(past_len, dtype=jnp.int32)[None, :]
    return _bias_lookup(rel_bias, dist).transpose(2, 0, 1)


def _mixer_ab(x2, b, t, W, l, i, st):
    prompt = st is None
    proj3 = mm_norm(x2, W["mix_norm"][i], W["ab_w_in"][l], W["ab_bias"][l], tn=1536).reshape(b, t, AB_COLS)
    kv_new = proj3[:, :, D_A:A_COLS]
    p_b = proj3[:, :, A_COLS:]
    sinks = W["swa_sinks"][l].reshape(H_A, 1, 1)
    if prompt:
        nw = -(-WIN_A // QBLK)
        sk = (nw + 1) * QBLK
        bias = _memo(W, _toeplitz_bias, QBLK, sk, nw * QBLK)
        o_a = band_attn(proj3, _pad_rows(kv_new, t + nw * QBLK, nw * QBLK),
                        bias, sinks, tq=QBLK, sk=sk, hd=HD_A, n_kvh_step=KVH_A, g=G_A, window=WIN_A,
                        delta=nw * QBLK, kpos_base=-nw * QBLK, kstride=QBLK, out_dtype=BF16)
        buf = kv_new[:, t - min(WIN_A, t):]
        y_b, s_new = rwkv_mix(p_b, jnp.zeros((b, 1, B_COLS), F32), jnp.zeros((b, H_B, HD_B, HD_B), F32),
                              W["rwkv"][l], c=RWKV_CHUNK, t_valid=RWKV_CHUNK)
    else:
        past_len = st["past_len"]
        old = st["swa"][l].reshape(b, -1, 2 * KVH_A * HD_A)
        wb = old.shape[1]
        kv_all = jnp.concatenate([old, kv_new], axis=1)
        buf = kv_all[:, t:]
        sk = -(-(wb + t) // V7X_LANES) * V7X_LANES
        bias = _memo(W, _toeplitz_bias, Q_PAD, sk, wb)
        o_a = band_attn(_pad_rows(proj3, Q_PAD), _pad_rows(kv_all, sk),
                        bias, sinks, tq=Q_PAD, sk=sk, hd=HD_A, n_kvh_step=KVH_A, g=G_A, window=WIN_A,
                        delta=wb, kpos_base=past_len - wb, kstride=0, out_dtype=BF16)[:, :t]
        y_b, s_new = rwkv_mix(_pad_rows(p_b, Q_PAD), st["shift"][l][:, None], st["wkv"][l],
                              W["rwkv"][l], c=Q_PAD, t_valid=t)
        y_b = y_b[:, :t]
    mix_in = jnp.concatenate([o_a, y_b], axis=-1).reshape(b * t, D_MODEL)
    x2 = mm_res(mix_in, W["ab_w_out"][l], x2)
    buf = buf.reshape(b, -1, 2, KVH_A, HD_A)
    return x2, buf, p_b[:, -1], s_new


def _mixer_c(x2, b, t, W, l, i, st):
    prompt = st is None
    kvw = KVH_C * HD_C
    proj3 = mm_norm(x2, W["mix_norm"][i], W["c_w_in"][l], W["c_zero_bias"], tn=1024).reshape(b, t, C_COLS_PAD)
    kv_cmp_new = proj3[:, :, D_C:D_C + 2 * kvw]
    kv_sel_new = proj3[:, :, D_C + 2 * kvw:D_C + 4 * kvw]
    kv_win_new = proj3[:, :, D_C + 4 * kvw:D_C + 6 * kvw]
    gate_logits = proj3[:, :, D_C + C_KV_COLS:D_C + C_KV_COLS + V7X_LANES].reshape(b * t, V7X_LANES)
    pe, w1, w2 = W["cmp_pe"][l], W["cmp_w1"][l], W["cmp_w2"][l]
    if prompt:
        n_pages = b * t // PAGE_SIZE
        kv_c = nsa_compress(kv_cmp_new.reshape(n_pages, PAGE_ROWS, HD_C), jnp.arange(n_pages, dtype=jnp.int32),
                            pe, w1, w2, pp=min(CMP_PAGES_PER_STEP, n_pages)).reshape(b, t // CMP_BLOCK, 2 * kvw)
        o_c, sel = nsa_cmp(proj3, kv_c, tq=QBLK, q0=0)
        o_s = nsa_sel_prompt(proj3, kv_sel_new[:, :, :kvw], kv_sel_new[:, :, kvw:], sel,
                             _memo(W, sel_bias_tiles, QBLK, t), tq=QBLK)
        nw = -(-WIN_C // QBLK)
        sk = (nw + 1) * QBLK
        bias = _memo(W, _toeplitz_bias, QBLK, sk, nw * QBLK)
        o_w = band_attn(proj3, _pad_rows(kv_win_new, t + nw * QBLK, nw * QBLK), bias, None,
                        tq=QBLK, sk=sk, hd=HD_C, n_kvh_step=1, g=G_C, window=WIN_C,
                        delta=nw * QBLK, kpos_base=-nw * QBLK, kstride=QBLK, out_dtype=F32)
        win_buf = kv_win_new[:, t - min(WIN_C, t):]
    else:
        past_len = st["past_len"]
        n_phys = st["cmp"].shape[1]
        table = st["page_table"].reshape(-1) + l * n_phys
        kv_c = nsa_compress(st["cmp"].reshape(-1, PAGE_ROWS, HD_C), table, pe, w1, w2,
                            pp=CMP_PAGES_PER_STEP).reshape(b, past_len // CMP_BLOCK, 2 * kvw)
        q8 = _pad_rows(proj3, Q_PAD)
        o_c, sel = nsa_cmp(q8, kv_c, tq=Q_PAD, q0=past_len)
        bias_past = _memo(W, _past_bias, Q_PAD, past_len)
        bias_new = _memo(W, _toeplitz_bias, Q_PAD, Q_PAD, 0)
        sel_new = _pad_rows(kv_sel_new, Q_PAD)
        o_s = nsa_sel_paged(q8, st["sel"].reshape(-1, PAGE_ROWS, HD_C), table, sel, bias_past,
                            sel_new[:, :, :kvw], sel_new[:, :, kvw:], bias_new, pp=SEL_PAGES_PER_STEP, tq=Q_PAD)
        old = st["win"][l].reshape(b, -1, 2 * kvw)
        wb = old.shape[1]
        kv_all = jnp.concatenate([old, kv_win_new], axis=1)
        win_buf = kv_all[:, t:]
        sk = -(-(wb + t) // V7X_LANES) * V7X_LANES
        bias = _memo(W, _toeplitz_bias, Q_PAD, sk, wb)
        o_w = band_attn(q8, _pad_rows(kv_all, sk), bias, None, tq=Q_PAD, sk=sk, hd=HD_C,
                        n_kvh_step=1, g=G_C, window=WIN_C, delta=wb, kpos_base=past_len - wb, kstride=0,
                        out_dtype=F32)
        o_c, o_s, o_w = o_c[:, :t], o_s[:, :t], o_w[:, :t]
    comb = nsa_combine(gate_logits, W["c_gate_b"][l], o_c.reshape(b * t, D_C), o_s.reshape(b * t, D_C),
                       o_w.reshape(b * t, D_C))
    x2 = mm_res(comb, W["c_w_out"][l], x2)
    shape5 = lambda a: a.reshape(b, -1, 2, KVH_C, HD_C)
    return x2, shape5(kv_cmp_new), shape5(kv_sel_new), shape5(win_buf)


def _trunk(x, p, W, st):
    b, t, _ = x.shape
    x2 = x.reshape(b * t, D_MODEL)
    swa_l, shift_l, wkv_l, cmp_l, sel_l, win_l = [], [], [], [], [], []
    for i in range(DEPTH):
        l = i // 2
        if i % 2 == 0:
            x2, buf, shift_new, s_new = _mixer_ab(x2, b, t, W, l, i, st)
            swa_l.append(buf)
            shift_l.append(shift_new)
            wkv_l.append(s_new)
        else:
            x2, cmp_new, sel_new, win_buf = _mixer_c(x2, b, t, W, l, i, st)
            cmp_l.append(cmp_new)
            sel_l.append(sel_new)
            win_l.append(win_buf)
        act = ffn_up(x2, W["ffn_norm"][i], W["ffn_w_gate"][i], W["ffn_w_up"][i])
        x2 = mm_res(act, W["ffn_w_down"][i], x2, tk=D_FF // 2)
        x2 = ple(x2, p[i].reshape(b * t, PLE_DIM), W["ple_gate_norm"][i], W["ple_w_gate"][i], W["ple_w_proj"][i],
                 W["ple_post_norm"][i], W["final_norm"], final=(i == DEPTH - 1))
    y = x2.reshape(b, t, D_MODEL)
    return (y, jnp.stack(swa_l), jnp.stack(shift_l), jnp.stack(wkv_l), jnp.stack(cmp_l), jnp.stack(sel_l),
            jnp.stack(win_l))


def kernel(x_prompt, x_sample, state_swa_kv, state_rwkv_shift, state_rwkv_wkv, cache_nsa_cmp_kv, cache_nsa_sel_kv, state_nsa_win_kv, page_table, p_prompt, p_sample, rel_bias, mix_norm, ab_w_in, ab_b_qkv, swa_sinks, rwkv_mu, rwkv_w0, rwkv_w2, rwkv_a0, rwkv_a2, rwkv_g2, rwkv_k_k, rwkv_k_a, rwkv_r_k, rwkv_ln_g, rwkv_ln_b, ab_w_out, c_w_in, c_gate_b, nsa_cmp_pos, nsa_cmp_w1, nsa_cmp_w2, c_w_out, ffn_norm, ffn_w_gate, ffn_w_up, ffn_w_down, ple_w_proj, ple_gate_norm, ple_w_gate, ple_post_norm, final_norm):
    n_ab, n_c = ab_w_in.shape[0], c_w_in.shape[0]
    bf = lambda a: a.astype(BF16)
    row = lambda a: a.reshape(a.shape[0], 1, -1).astype(F32)
    pe = jnp.broadcast_to(nsa_cmp_pos.transpose(0, 2, 1, 3)[:, :, :, None, :],
                          (n_c, CMP_BLOCK, 2, KVH_C, HD_C)).reshape(n_c, BLOCK_ROWS, HD_C)
    W = dict(
        tables={}, rel_bias=rel_bias,mix_norm=row(mix_norm), ab_w_in=bf(ab_w_in),
        ab_bias=jnp.pad(ab_b_qkv, ((0, 0), (0, AB_COLS - A_COLS))).reshape(n_ab, 1, AB_COLS),
        swa_sinks=swa_sinks, ab_w_out=bf(ab_w_out),
        rwkv=[dict(mu=rwkv_mu[l][None], w0=rwkv_w0[l][None], w2=bf(rwkv_w2[l]), a0=rwkv_a0[l][None], a2=bf(rwkv_a2[l]),
                   g2=bf(rwkv_g2[l]), k_k=rwkv_k_k[l][None], k_a=rwkv_k_a[l][None], r_k=rwkv_r_k[l].reshape(1, C_B),
                   ln_g=rwkv_ln_g[l][None], ln_b=rwkv_ln_b[l][None]) for l in range(n_ab)],
        c_w_in=bf(jnp.pad(c_w_in, ((0, 0), (0, 0), (0, C_COLS_PAD - C_COLS)))),
        c_zero_bias=jnp.zeros((1, C_COLS_PAD), F32),
        c_gate_b=jnp.pad(c_gate_b, ((0, 0), (0, V7X_LANES - 3 * H_C))).reshape(n_c, 1, V7X_LANES),
        cmp_pe=pe, cmp_w1=bf(nsa_cmp_w1), cmp_w2=bf(nsa_cmp_w2), c_w_out=bf(c_w_out),
        ffn_norm=row(ffn_norm), ffn_w_gate=bf(ffn_w_gate), ffn_w_up=bf(ffn_w_up), ffn_w_down=bf(ffn_w_down),
        ple_w_proj=bf(ple_w_proj), ple_gate_norm=row(ple_gate_norm), ple_w_gate=bf(ple_w_gate),
        ple_post_norm=row(ple_post_norm), final_norm=final_norm.reshape(1, D_MODEL),
    )
    st = dict(swa=state_swa_kv, shift=state_rwkv_shift, wkv=state_rwkv_wkv, cmp=cache_nsa_cmp_kv,
              sel=cache_nsa_sel_kv, win=state_nsa_win_kv, page_table=page_table,
              past_len=page_table.shape[1] * PAGE_SIZE)
    y_p, swa_p, shift_p, wkv_p, cmp_p, sel_p, win_p = _trunk(x_prompt, p_prompt, W, None)
    y_s, swa_s, shift_s, wkv_s, cmp_s, sel_s, win_s = _trunk(x_sample, p_sample, W, st)
    return (y_p, y_s, swa_p, swa_s, shift_p, shift_s, wkv_p, wkv_s, cmp_p, cmp_s, sel_p, sel_s, win_p, win_s)
```

```python
import functools
import math

import jax
import jax.numpy as jnp
import numpy as np
from jax import lax
from jax.experimental import pallas as pl
from jax.experimental.pallas import tpu as pltpu

F32 = jnp.float32
BF16 = jnp.bfloat16

D_MODEL = 2048
DEPTH = 4
PAGE_SIZE = 128
PLE_DIM = 256
N_BUCKETS = 32
REL_MAX_DIST = 1024
RMS_EPS = 1e-6
D_FF = 5632
QBLK = 128
HD_A = 64
H_A = 16
KVH_A = 2
G_A = 8
D_A = 1024
WIN_A = 128
A_COLS = D_A + 2 * KVH_A * HD_A
HD_B = 64
C_B = 1024
H_B = 16
LORA_W = 64
LORA_A = 64
LORA_G = 128
B_COLS = 3 * C_B + LORA_W + LORA_A + LORA_G
AB_COLS = A_COLS + B_COLS
GN_EPS = 64e-5
HD_C = 128
H_C = 16
KVH_C = 2
G_C = 8
D_C = 2048
CMP_BLOCK = 64
SEL_BLOCK = 64
SEL_SHIFT = 6
N_TOP = 15
WIN_C = 512
CMP_HIDDEN = 128
C_KV_COLS = 6 * KVH_C * HD_C
C_COLS = D_C + C_KV_COLS + 3 * H_C
NEG_INF = -1e30
FORCE_SCORE = 1e4

V7X_LANES = 128
V7X_SUBLANES = 8
V7X_VMEM_BYTES = 64 * 1024 * 1024
VMEM_LIMIT_CAP = V7X_VMEM_BYTES - 8 * 1024 * 1024

C_COLS_PAD = 4096
RWKV_CHUNK = 64


def _cparams(sem, vmem_bytes):
    limit = int(min(max(2 * vmem_bytes, 32 * 1024 * 1024), VMEM_LIMIT_CAP))
    return pltpu.CompilerParams(dimension_semantics=sem, vmem_limit_bytes=limit)


def _row_tile(m, cap):
    t = min(m, cap)
    assert m % t == 0, (m, t)
    return t


def _rms(x, g):
    return x * lax.rsqrt(jnp.mean(x * x, axis=-1, keepdims=True) + RMS_EPS) * g


def t5_bucket(dist):
    n = jnp.maximum(dist, 0)
    max_exact = N_BUCKETS // 2
    nf = jnp.maximum(n, max_exact).astype(F32)
    large = max_exact + (jnp.log(nf / max_exact) / math.log(REL_MAX_DIST / max_exact) * (N_BUCKETS - max_exact)).astype(jnp.int32)
    return jnp.where(n < max_exact, n, jnp.minimum(large, N_BUCKETS - 1))


def _mm_norm_kernel(x_ref, g_ref, w_ref, b_ref, o_ref, h_ref):
    @pl.when(pl.program_id(1) == 0)
    def _():
        h_ref[...] = _rms(x_ref[...], g_ref[...]).astype(BF16)

    o_ref[...] = jnp.dot(h_ref[...], w_ref[...], preferred_element_type=F32) + b_ref[...]


def mm_norm(x, g, w, b, *, tn=512):
    m, k = x.shape
    n = w.shape[1]
    tm = _row_tile(m, 1024)
    vmem = 2 * tm * k * 4 + tm * k * 2 + 2 * k * tn * 2 + 2 * tm * tn * 4
    return pl.pallas_call(
        _mm_norm_kernel,
        out_shape=jax.ShapeDtypeStruct((m, n), F32),
        grid=(m // tm, n // tn),
        in_specs=[pl.BlockSpec((tm, k), lambda i, j: (i, 0)),
                  pl.BlockSpec((1, k), lambda i, j: (0, 0)),
                  pl.BlockSpec((k, tn), lambda i, j: (0, j)),
                  pl.BlockSpec((1, tn), lambda i, j: (0, j))],
        out_specs=pl.BlockSpec((tm, tn), lambda i, j: (i, j)),
        scratch_shapes=[pltpu.VMEM((tm, k), BF16)],
        compiler_params=_cparams(("parallel", "arbitrary"), vmem),
        name="mm_norm",
    )(x, g, w, b)


def _mm_res_kernel(a_ref, w_ref, r_ref, o_ref, acc_ref, *, nk):
    kk = pl.program_id(2)

    @pl.when(kk == 0)
    def _():
        acc_ref[...] = jnp.zeros_like(acc_ref)

    acc_ref[...] += jnp.dot(a_ref[...], w_ref[...], preferred_element_type=F32)

    @pl.when(kk == nk - 1)
    def _():
        o_ref[...] = r_ref[...] + acc_ref[...]


def mm_res(a, w, r, *, tn=1024, tk=None):
    m, k = a.shape
    n = w.shape[1]
    tm = _row_tile(m, 1024)
    tk = k if tk is None else tk
    nk = k // tk
    vmem = 2 * tm * tk * 2 + 2 * tk * tn * 2 + 5 * tm * tn * 4
    return pl.pallas_call(
        functools.partial(_mm_res_kernel, nk=nk),
        out_shape=jax.ShapeDtypeStruct((m, n), F32),
        grid=(m // tm, n // tn, nk),
        in_specs=[pl.BlockSpec((tm, tk), lambda i, j, q: (i, q)),
                  pl.BlockSpec((tk, tn), lambda i, j, q: (q, j)),
                  pl.BlockSpec((tm, tn), lambda i, j, q: (i, j))],
        out_specs=pl.BlockSpec((tm, tn), lambda i, j, q: (i, j)),
        scratch_shapes=[pltpu.VMEM((tm, tn), F32)],
        compiler_params=_cparams(("parallel", "parallel", "arbitrary"), vmem),
        name="mm_res",
    )(a, w, r)


def _ffn_up_kernel(x_ref, g_ref, wg_ref, wu_ref, o_ref, h_ref):
    @pl.when(pl.program_id(1) == 0)
    def _():
        h_ref[...] = _rms(x_ref[...], g_ref[...]).astype(BF16)

    h = h_ref[...]
    gate = jnp.dot(h, wg_ref[...], preferred_element_type=F32)
    up = jnp.dot(h, wu_ref[...], preferred_element_type=F32)
    o_ref[...] = (jax.nn.silu(gate) * up).astype(BF16)


def ffn_up(x, g, wg, wu, *, tn=512):
    m, k = x.shape
    n = wg.shape[1]
    tm = _row_tile(m, 1024)
    vmem = 2 * tm * k * 4 + tm * k * 2 + 4 * k * tn * 2 + 2 * tm * tn * 2 + 3 * tm * tn * 4
    return pl.pallas_call(
        _ffn_up_kernel,
        out_shape=jax.ShapeDtypeStruct((m, n), BF16),
        grid=(m // tm, n // tn),
        in_specs=[pl.BlockSpec((tm, k), lambda i, j: (i, 0)),
                  pl.BlockSpec((1, k), lambda i, j: (0, 0)),
                  pl.BlockSpec((k, tn), lambda i, j: (0, j)),
                  pl.BlockSpec((k, tn), lambda i, j: (0, j))],
        out_specs=pl.BlockSpec((tm, tn), lambda i, j: (i, j)),
        scratch_shapes=[pltpu.VMEM((tm, k), BF16)],
        compiler_params=_cparams(("parallel", "arbitrary"), vmem),
        name="ffn_up",
    )(x, g, wg, wu)


def _ple_kernel(x_ref, p_ref, gn_ref, wg_ref, wp_ref, pn_ref, fn_ref, o_ref, *, final):
    x = x_ref[...]
    h = _rms(x, gn_ref[...]).astype(BF16)
    gate = jax.nn.sigmoid(jnp.dot(h, wg_ref[...], preferred_element_type=F32))
    e = jnp.dot(p_ref[...].astype(BF16), wp_ref[...], preferred_element_type=F32)
    x = x + _rms(gate * e, pn_ref[...])
    if final:
        x = _rms(x, fn_ref[...])
    o_ref[...] = x


def ple(x, p, gn, wg, wp, pn, fn, *, final):
    m, d = x.shape
    tm = _row_tile(m, 512)
    vmem = 4 * tm * d * 4 + 2 * d * d * 2 + 2 * PLE_DIM * d * 2 + 4 * tm * d * 4
    row = lambda i: (i, 0)
    fix = lambda i: (0, 0)
    return pl.pallas_call(
        functools.partial(_ple_kernel, final=final),
        out_shape=jax.ShapeDtypeStruct((m, d), F32),
        grid=(m // tm,),
        in_specs=[pl.BlockSpec((tm, d), row), pl.BlockSpec((tm, PLE_DIM), row),
                  pl.BlockSpec((1, d), fix), pl.BlockSpec((d, d), fix),
                  pl.BlockSpec((PLE_DIM, d), fix), pl.BlockSpec((1, d), fix),
                  pl.BlockSpec((1, d), fix)],
        out_specs=pl.BlockSpec((tm, d), row),
        compiler_params=_cparams(("parallel",), vmem),
        name="ple",
    )(x, p, gn, wg, wp, pn, fn)


def _masked_softmax(logits, mask, sink=None):
    lf = jnp.where(mask, logits, NEG_INF)
    m = jnp.max(lf, axis=-1, keepdims=True)
    if sink is not None:
        m = jnp.maximum(m, sink)
    e = jnp.where(mask, jnp.exp(lf - m), 0.0)
    den = jnp.sum(e, axis=-1, keepdims=True)
    if sink is not None:
        den = den + jnp.exp(sink - m)
    return e / jnp.maximum(den, 1e-30)


def _stack_heads(q, h0, g, hd):
    return jnp.concatenate([q[:, (h0 + j) * hd:(h0 + j + 1) * hd] for j in range(g)], axis=0)


def _band_attn_kernel(q_ref, k_ref, v_ref, bias_ref, sink_ref, o_ref, *,
                      tq, sk, hd, n_kvh, g, window, delta, kpos_base, kstride, has_sink):
    tl = V7X_LANES
    n_tiles = sk // tl
    ks = pl.multiple_of(pl.program_id(2) * kstride, V7X_SUBLANES)
    kslab = k_ref[pl.ds(ks, sk), :].astype(BF16)
    vslab = v_ref[pl.ds(ks, sk), :].astype(BF16)
    r = lax.broadcasted_iota(jnp.int32, (tq, tl), 0)
    c = lax.broadcasted_iota(jnp.int32, (tq, tl), 1)
    masks = []
    for t in range(n_tiles):
        dist = delta + r - (c + t * tl)
        masks.append(((dist >= 0) & (dist <= window) & (kpos_base + ks + t * tl + c >= 0))[None])
    q = q_ref[...] * (hd ** -0.5)
    chains = range(n_kvh)
    q8 = [_stack_heads(q, h * g, g, hd).astype(BF16) for h in chains]
    lf = [[lax.dot_general(q8[h], kslab[t * tl:(t + 1) * tl, h * hd:(h + 1) * hd], _NT,
                           preferred_element_type=F32).reshape(g, tq, tl)
           + jnp.where(masks[t], bias_ref[h * g:(h + 1) * g, :, t * tl:(t + 1) * tl], NEG_INF)
           for t in range(n_tiles)] for h in chains]
    m = [jnp.max(functools.reduce(jnp.maximum, lf[h]), axis=-1, keepdims=True) for h in chains]
    if has_sink:
        m = [jnp.maximum(m[h], sink_ref[h * g:(h + 1) * g]) for h in chains]
    mfull = [jnp.broadcast_to(m[h], (g, tq, tl)) for h in chains]
    e = [[jnp.exp(lf[h][t] - mfull[h]) for t in range(n_tiles)] for h in chains]
    acc = [functools.reduce(jnp.add, [jnp.dot(e[h][t].reshape(g * tq, tl).astype(BF16),
                                              vslab[t * tl:(t + 1) * tl, h * hd:(h + 1) * hd],
                                              preferred_element_type=F32) for t in range(n_tiles)])
           for h in chains]
    den = [jnp.sum(functools.reduce(jnp.add, e[h]), axis=-1, keepdims=True) for h in chains]
    if has_sink:
        den = [den[h] + jnp.exp(sink_ref[h * g:(h + 1) * g] - m[h]) for h in chains]
    for h in chains:
        o = acc[h] / jnp.maximum(den[h].reshape(g * tq, 1), 1e-30)
        for j in range(g):
            o_ref[:, (h * g + j) * hd:(h * g + j + 1) * hd] = o[j * tq:(j + 1) * tq].astype(o_ref.dtype)


def band_attn(q_arr, kv_arr, bias, sink, *, tq, sk, hd, n_kvh_step, g, window, delta,
              kpos_base, kstride, out_dtype):
    b, t = q_arr.shape[:2]
    tk = kv_arr.shape[1]
    n_kv_blocks = kv_arr.shape[2] // (2 * n_kvh_step * hd)
    qw = n_kvh_step * g * hd
    has_sink = sink is not None
    if not has_sink:
        sink = jnp.zeros((n_kv_blocks * n_kvh_step * g, 1, 1), F32)
    vmem = 2 * tq * qw * 4 * 2 + 4 * tk * n_kvh_step * hd * 4 + 2 * n_kvh_step * g * tq * sk * 4 + 6 * g * tq * sk * 4
    kern = functools.partial(_band_attn_kernel, tq=tq, sk=sk, hd=hd, n_kvh=n_kvh_step, g=g, window=window,
                             delta=delta, kpos_base=kpos_base, kstride=kstride, has_sink=has_sink)
    return pl.pallas_call(
        kern,
        out_shape=jax.ShapeDtypeStruct((b, t, n_kv_blocks * qw), out_dtype),
        grid=(b, n_kv_blocks, t // tq),
        in_specs=[pl.BlockSpec((None, tq, qw), lambda bb, kv, i: (bb, i, kv)),
                  pl.BlockSpec((None, tk, n_kvh_step * hd), lambda bb, kv, i: (bb, 0, kv)),
                  pl.BlockSpec((None, tk, n_kvh_step * hd), lambda bb, kv, i: (bb, 0, n_kv_blocks + kv)),
                  pl.BlockSpec((n_kvh_step * g, tq, sk), lambda bb, kv, i: (kv, 0, 0)),
                  pl.BlockSpec((n_kvh_step * g, 1, 1), lambda bb, kv, i: (kv, 0, 0))],
        out_specs=pl.BlockSpec((None, tq, qw), lambda bb, kv, i: (bb, i, kv)),
        compiler_params=_cparams(("parallel", "parallel", "arbitrary"), vmem),
        name="band_attn",
    )(q_arr, kv_arr, kv_arr, bias, sink)


def _bias_lookup(rel_bias, dist):
    onehot = (t5_bucket(dist)[..., None] == jnp.arange(N_BUCKETS, dtype=jnp.int32)).astype(F32)
    return jnp.einsum("...k,kh->...h", onehot, rel_bias.astype(F32), precision=lax.Precision.HIGHEST)


def _toeplitz_bias(rel_bias, tq, sk, delta):
    dist = delta + jnp.arange(tq, dtype=jnp.int32)[:, None] - jnp.arange(sk, dtype=jnp.int32)[None, :]
    return _bias_lookup(rel_bias, dist).transpose(2, 0, 1)


def _nsa_cmp_kernel(q_ref, kv_ref, oc_ref, sel_ref, *, tq, n_cmp, g, q0):
    hd = HD_C
    qp = q0 + pl.program_id(1) * tq + lax.broadcasted_iota(jnp.int32, (tq, n_cmp), 0)
    blk = lax.broadcasted_iota(jnp.int32, (tq, n_cmp), 1)
    cmask = ((blk + 1) * CMP_BLOCK <= qp + 1)[None]
    cur = qp >> SEL_SHIFT
    groups = range(KVH_C)
    q = q_ref[...]
    scores = []
    for h in groups:
        q8 = _stack_heads(q, h * g, g, hd)
        kc = kv_ref[:, h * hd:(h + 1) * hd]
        vc = kv_ref[:, (KVH_C + h) * hd:(KVH_C + h + 1) * hd].astype(BF16)
        q_hi = q8.astype(BF16)
        q_lo = (q8 - q_hi.astype(F32)).astype(BF16)
        k_hi = kc.astype(BF16)
        k_lo = (kc - k_hi.astype(F32)).astype(BF16)
        cl = (lax.dot_general(q_hi, k_hi, _NT, preferred_element_type=F32)
              + lax.dot_general(q_hi, k_lo, _NT, preferred_element_type=F32)
              + lax.dot_general(q_lo, k_hi, _NT, preferred_element_type=F32)) * (hd ** -0.5)
        p_c = _masked_softmax(cl.reshape(g, tq, n_cmp), cmask)
        o = jnp.dot(p_c.reshape(g * tq, n_cmp).astype(BF16), vc, preferred_element_type=F32)
        for j in range(g):
            oc_ref[:, (h * g + j) * hd:(h * g + j + 1) * hd] = o[j * tq:(j + 1) * tq]
        imp = jnp.sum(p_c, axis=0) + jnp.where(blk == 0, FORCE_SCORE, 0.0)
        scores.append(jnp.where(blk < cur, imp, -1.0))
    transposed = tq % V7X_LANES == 0
    if transposed:
        work = [scores[h].T for h in groups]
        idx = lax.broadcasted_iota(jnp.int32, (n_cmp, tq), 0)
    else:
        work = scores
        idx = blk
    ranks = [jnp.zeros(work[0].shape, F32) for _ in groups]
    for k in range(n_cmp):
        for h in groups:
            other = work[h][k:k + 1, :] if transposed else work[h][:, k:k + 1]
            ranks[h] = ranks[h] + jnp.where(idx > k, jnp.where(other >= work[h], 1.0, 0.0),
                                            jnp.where(other > work[h], 1.0, 0.0))
    if transposed:
        ranks = [ranks[h].T for h in groups]
    for h in groups:
        sel_ref[h] = jnp.where(((ranks[h] < N_TOP) & (scores[h] >= 0.0)) | (blk == cur), 1.0, 0.0)


def nsa_cmp(q_arr, kv_c, *, tq, q0):
    b, t = q_arr.shape[:2]
    n_cmp = kv_c.shape[1]
    assert n_cmp >= N_TOP
    kvw = 2 * KVH_C * HD_C
    vmem = 4 * tq * D_C * 4 + 2 * n_cmp * kvw * 4 + 16 * G_C * tq * n_cmp * 4
    return pl.pallas_call(
        functools.partial(_nsa_cmp_kernel, tq=tq, n_cmp=n_cmp, g=G_C, q0=q0),
        out_shape=(jax.ShapeDtypeStruct((b, t, D_C), F32),
                   jax.ShapeDtypeStruct((b, KVH_C, t, n_cmp), F32)),
        grid=(b, t // tq),
        in_specs=[pl.BlockSpec((None, tq, D_C), lambda bb, i: (bb, i, 0)),
                  pl.BlockSpec((None, n_cmp, kvw), lambda bb, i: (bb, 0, 0))],
        out_specs=(pl.BlockSpec((None, tq, D_C), lambda bb, i: (bb, i, 0)),
                   pl.BlockSpec((None, KVH_C, tq, n_cmp), lambda bb, i: (bb, 0, i, 0))),
        compiler_params=_cparams(("parallel", "arbitrary"), vmem),
        name="nsa_cmp",
    )(q_arr, kv_c)


SEL_TILES_PER_TRIP = 4


def _nsa_sel_kernel(q_ref, k_ref, v_ref, sel_ref, bias_ref, o_ref, selk_ref, mx_ref, le_ref, acc_ref, lf_ref, *,
                    tq, t, g, n_far):
    hd = HD_C
    i = pl.program_id(2)
    n_blk = t // SEL_BLOCK
    eb = lax.broadcasted_iota(jnp.int32, (n_blk, t), 0)
    ek = lax.broadcasted_iota(jnp.int32, (n_blk, t), 1)
    expand = jnp.where((ek >> SEL_SHIFT) == eb, 1.0, 0.0).astype(BF16)
    selk_ref[...] = jnp.dot(sel_ref[...].astype(BF16), expand, preferred_element_type=F32)
    q8 = (_stack_heads(q_ref[...], 0, g, hd) * (hd ** -0.5)).astype(BF16)
    r = lax.broadcasted_iota(jnp.int32, (tq, tq), 0)
    c = lax.broadcasted_iota(jnp.int32, (tq, tq), 1)

    last = t // tq - 1

    def tile_start(j):
        return pl.multiple_of(jnp.minimum(j, last) * tq, tq)

    def logits(j):
        ks = tile_start(j)
        kj = k_ref[pl.ds(ks, tq), :].astype(BF16)
        s = lax.dot_general(q8, kj, _NT, preferred_element_type=F32).reshape(g, tq, tq)
        mask = ((selk_ref[:, pl.ds(ks, tq)] > 0.5) & ((j - i) * tq + c <= r))[None]
        return s + jnp.where(mask, bias_ref[jnp.clip(i - j, 0, n_far)], NEG_INF)

    per = SEL_TILES_PER_TRIP
    n_trips = (i + per) // per
    mx_ref[...] = jnp.full(mx_ref.shape, NEG_INF, F32)

    def sweep_max(jj, carry):
        tiles = [logits(per * jj + u) for u in range(per)]
        for u in range(per):
            lf_ref[jnp.minimum(per * jj + u, last)] = tiles[u]
        mx_ref[...] = jnp.maximum(mx_ref[...], functools.reduce(jnp.maximum, tiles))
        return carry

    lax.fori_loop(0, n_trips, sweep_max, 0)
    mx_ref[...] = jnp.broadcast_to(jnp.max(mx_ref[...], axis=-1, keepdims=True), mx_ref.shape)
    le_ref[...] = jnp.zeros(le_ref.shape, F32)
    acc_ref[...] = jnp.zeros(acc_ref.shape, F32)

    def sweep_acc(jj, carry):
        mx = mx_ref[...]
        es = [jnp.exp(lf_ref[jnp.minimum(per * jj + u, last)] - mx) for u in range(per)]
        le_ref[...] += functools.reduce(jnp.add, es)
        pvs = [jnp.dot(es[u].reshape(g * tq, tq).astype(BF16),
                       v_ref[pl.ds(tile_start(per * jj + u), tq), :].astype(BF16), preferred_element_type=F32)
               for u in range(per)]
        acc_ref[...] += functools.reduce(jnp.add, pvs)
        return carry

    lax.fori_loop(0, n_trips, sweep_acc, 0)
    den = jnp.sum(le_ref[...], axis=-1, keepdims=True).reshape(g * tq, 1)
    o = acc_ref[...] / jnp.maximum(den, 1e-30)
    for j in range(g):
        o_ref[:, j * hd:(j + 1) * hd] = o[j * tq:(j + 1) * tq]


def _bias_saturation_offset(tq, t):
    d = np.arange(0, t + tq, dtype=np.float64)
    nf = np.maximum(d, N_BUCKETS // 2)
    large = N_BUCKETS // 2 + np.floor(np.log(nf / (N_BUCKETS // 2)) / math.log(REL_MAX_DIST / (N_BUCKETS // 2))
                                      * (N_BUCKETS - N_BUCKETS // 2) - 1e-3)
    saturated = np.where(d < N_BUCKETS // 2, 0, large) >= N_BUCKETS - 1
    if not saturated.any():
        return t // tq
    first_sat = int(np.argmax(saturated))
    return min(t // tq, -(-(first_sat + tq) // tq))


def sel_bias_tiles(rel_bias, tq, t):
    n_far = _bias_saturation_offset(tq, t)
    dist = (jnp.arange(n_far + 1, dtype=jnp.int32)[:, None, None] * tq
            + jnp.arange(tq, dtype=jnp.int32)[None, :, None] - jnp.arange(tq, dtype=jnp.int32)[None, None, :])
    tiles = _bias_lookup(rel_bias, dist)
    return tiles.reshape(n_far + 1, tq, tq, KVH_C, G_C).transpose(3, 0, 4, 1, 2)


def nsa_sel_prompt(q_arr, k_arr, v_arr, sel, tiles, *, tq):
    b, t = q_arr.shape[:2]
    n_blk = t // SEL_BLOCK
    n_far = tiles.shape[1] - 1
    n_tiles = t // tq
    assert n_tiles % SEL_TILES_PER_TRIP == 0
    qw = G_C * HD_C
    lf_bytes = n_tiles * G_C * tq * tq * 4
    vmem = (4 * tq * qw * 4 + 4 * t * HD_C * 4 + 2 * (n_far + 1) * G_C * tq * tq * 4 + tq * t * 4
            + 8 * G_C * tq * tq * 4 + n_blk * t * 4 + lf_bytes)
    return pl.pallas_call(
        functools.partial(_nsa_sel_kernel, tq=tq, t=t, g=G_C, n_far=n_far),
        out_shape=jax.ShapeDtypeStruct((b, t, D_C), F32),
        grid=(b, KVH_C, t // tq),
        in_specs=[pl.BlockSpec((None, tq, qw), lambda bb, kv, i: (bb, i, kv)),
                  pl.BlockSpec((None, t, HD_C), lambda bb, kv, i: (bb, 0, kv)),
                  pl.BlockSpec((None, t, HD_C), lambda bb, kv, i: (bb, 0, kv)),
                  pl.BlockSpec((None, None, tq, n_blk), lambda bb, kv, i: (bb, kv, i, 0)),
                  pl.BlockSpec((None, n_far + 1, G_C, tq, tq), lambda bb, kv, i: (kv, 0, 0, 0, 0))],
        out_specs=pl.BlockSpec((None, tq, qw), lambda bb, kv, i: (bb, i, kv)),
        scratch_shapes=[pltpu.VMEM((tq, t), F32), pltpu.VMEM((G_C, tq, tq), F32),
                        pltpu.VMEM((G_C, tq, tq), F32), pltpu.VMEM((G_C * tq, HD_C), F32),
                        pltpu.VMEM((n_tiles, G_C, tq, tq), F32)],
        compiler_params=_cparams(("parallel", "parallel", "arbitrary"), vmem),
        name="nsa_sel_prompt",
    )(q_arr, k_arr, v_arr, sel, tiles)


def _bdot(a, b, dims=(((1,), (0,)), ((), ()))):
    return lax.dot_general(a.astype(BF16), b.astype(BF16), dims, preferred_element_type=F32)


_NT = (((1,), (1,)), ((), ()))
_TN = (((0,), (0,)), ((), ()))


def _rwkv_kernel(p_ref, shift_ref, s0_ref, mu_ref, w0_ref, w2_ref, a0_ref, a2_ref, g2_ref, kk_ref, ka_ref,
                 rk_ref, lng_ref, lnb_ref, y_ref, sout_ref, carry_ref, state_ref, *, c, t_valid, n_chunks):
    ci = pl.program_id(1)

    @pl.when(ci == 0)
    def _():
        carry_ref[...] = shift_ref[...]
        state_ref[...] = s0_ref[...]

    p = p_ref[...]
    row = lax.broadcasted_iota(jnp.int32, (c, 1), 0)
    prev = jnp.where(row == 0, carry_ref[...], pltpu.roll(p, 1, axis=0))
    carry_ref[...] = p[c - 1:c, :]
    xs = p + (prev - p) * mu_ref[...]
    o = 3 * C_B
    r = xs[:, :C_B]
    k = xs[:, C_B:2 * C_B]
    v = xs[:, 2 * C_B:o]
    wd = xs[:, o:o + LORA_W]
    ad = xs[:, o + LORA_W:o + LORA_W + LORA_A]
    gd = xs[:, o + LORA_W + LORA_A:]
    w_raw = w0_ref[...] + _bdot(jnp.tanh(wd), w2_ref[...])
    logd = -jnp.exp(-jax.nn.softplus(-w_raw) - 0.5)
    a = jax.nn.sigmoid(a0_ref[...] + _bdot(ad, a2_ref[...]))
    gate = _bdot(jax.nn.sigmoid(gd), g2_ref[...])
    kk = k * kk_ref[...]
    k = k * (1.0 + (a - 1.0) * ka_ref[...])
    if t_valid < c:
        valid = row < t_valid
        logd = jnp.where(valid, logd, 0.0)
        r = jnp.where(valid, r, 0.0)
        k = jnp.where(valid, k, 0.0)
        v = jnp.where(valid, v, 0.0)
        kk = jnp.where(valid, kk, 0.0)
    ti = lax.broadcasted_iota(jnp.int32, (c, c), 0)
    si = lax.broadcasted_iota(jnp.int32, (c, c), 1)
    incl = si <= ti
    strict = si < ti
    tri = jnp.where(incl, 1.0, 0.0).astype(BF16)
    hi = logd.astype(BF16)
    rem = logd - hi.astype(F32)
    mid = rem.astype(BF16)
    lo = (rem - mid.astype(F32)).astype(BF16)
    cs = (jnp.dot(tri, hi, preferred_element_type=F32) + jnp.dot(tri, mid, preferred_element_type=F32)
          + jnp.dot(tri, lo, preferred_element_type=F32))
    e_pos = jnp.exp(cs)
    e_prev = jnp.exp(cs - logd)
    e_neg = jnp.exp(-cs)
    rk = r * k * rk_ref[...]
    n_levels = int(math.log2(c))
    assert 2 ** n_levels == c
    heads = range(H_B)
    sls = [slice(h * HD_B, (h + 1) * HD_B) for h in heads]
    kkn = []
    for sl in sls:
        kk_h = kk[:, sl]
        kkn.append(kk_h / jnp.maximum(jnp.sqrt(jnp.sum(kk_h * kk_h, axis=-1, keepdims=True)), 1e-12))
    v_h = [v[:, sl].astype(BF16) for sl in sls]
    lhs = [jnp.concatenate([-kkn[h] * e_prev[:, sls[h]], r[:, sls[h]] * e_pos[:, sls[h]]], axis=0).astype(BF16)
           for h in heads]
    rhs = [jnp.concatenate([kkn[h] * a[:, sls[h]] * e_neg[:, sls[h]], k[:, sls[h]] * e_neg[:, sls[h]]],
                           axis=0).astype(BF16) for h in heads]
    s0 = [state_ref[h] for h in heads]
    mm = [_bdot(lhs[h], rhs[h], _NT) for h in heads]
    ars = [_bdot(lhs[h], s0[h], _NT) for h in heads]
    u = [ars[h][:c] + _bdot(jnp.where(strict, mm[h][:c, c:], 0.0), v_h[h]) for h in heads]
    lp = [jnp.where(strict, mm[h][:c, :c], 0.0) for h in heads]
    for lvl in range(n_levels):
        u = [u[h] + _bdot(lp[h], u[h]) for h in heads]
        if lvl < n_levels - 1:
            lp = [_bdot(lp[h], lp[h]) for h in heads]
    uv = [jnp.concatenate([u[h].astype(BF16), v_h[h]], axis=0) for h in heads]
    t2 = lax.broadcasted_iota(jnp.int32, (c, 2 * c), 0)
    s2 = lax.broadcasted_iota(jnp.int32, (c, 2 * c), 1)
    incl2 = jnp.where(s2 >= c, s2 - c, s2) <= t2
    y = [ars[h][c:] + _bdot(jnp.where(incl2, mm[h][c:], 0.0), uv[h]) for h in heads]
    for h in heads:
        state_ref[h] = (s0[h] + _bdot(uv[h], rhs[h], _TN)) * e_pos[c - 1:c, sls[h]]
    outs = []
    for h in heads:
        sl = sls[h]
        mean = jnp.mean(y[h], axis=-1, keepdims=True)
        var = jnp.mean(jnp.square(y[h] - mean), axis=-1, keepdims=True)
        yn = (y[h] - mean) * lax.rsqrt(var + GN_EPS) * lng_ref[:, sl] + lnb_ref[:, sl]
        bonus = jnp.sum(rk[:, sl], axis=-1, keepdims=True) * v[:, sl]
        outs.append((yn + bonus) * gate[:, sl])
    y_ref[...] = jnp.concatenate(outs, axis=-1).astype(y_ref.dtype)

    @pl.when(ci == n_chunks - 1)
    def _():
        sout_ref[...] = state_ref[...]


def rwkv_mix(p, shift0, s0, w, *, c, t_valid):
    b, t, _ = p.shape
    n_chunks = t // c
    fix2 = lambda bb, ci: (0, 0)
    vec = lambda n: pl.BlockSpec((1, n), fix2)
    vmem = 6 * c * B_COLS * 4 + 4 * H_B * HD_B * HD_B * 4 + 40 * c * C_B * 4 + (LORA_W + LORA_A + LORA_G) * C_B * 4
    return pl.pallas_call(
        functools.partial(_rwkv_kernel, c=c, t_valid=t_valid, n_chunks=n_chunks),
        out_shape=(jax.ShapeDtypeStruct((b, t, C_B), BF16),
                   jax.ShapeDtypeStruct((b, H_B, HD_B, HD_B), F32)),
        grid=(b, n_chunks),
        in_specs=[pl.BlockSpec((None, c, B_COLS), lambda bb, ci: (bb, ci, 0)),
                  pl.BlockSpec((None, 1, B_COLS), lambda bb, ci: (bb, 0, 0)),
                  pl.BlockSpec((None, H_B, HD_B, HD_B), lambda bb, ci: (bb, 0, 0, 0)),
                  vec(B_COLS), vec(C_B), pl.BlockSpec((LORA_W, C_B), fix2),
                  vec(C_B), pl.BlockSpec((LORA_A, C_B), fix2), pl.BlockSpec((LORA_G, C_B), fix2),
                  vec(C_B), vec(C_B), vec(C_B), vec(C_B), vec(C_B)],
        out_specs=(pl.BlockSpec((None, c, C_B), lambda bb, ci: (bb, ci, 0)),
                   pl.BlockSpec((None, H_B, HD_B, HD_B), lambda bb, ci: (bb, 0, 0, 0))),
        scratch_shapes=[pltpu.VMEM((1, B_COLS), F32), pltpu.VMEM((H_B, HD_B, HD_B), F32)],
        compiler_params=_cparams(("parallel", "arbitrary"), vmem),
        name="rwkv_mix",
    )(p, shift0, s0, w["mu"], w["w0"], w["w2"], w["a0"], w["a2"], w["g2"], w["k_k"], w["k_a"],
      w["r_k"], w["ln_g"], w["ln_b"])


PAGE_CH = 2 * KVH_C
PAGE_ROWS = PAGE_SIZE * PAGE_CH


def _stream_pages(make_copies):
    step = pl.program_id(0)
    n_steps = pl.num_programs(0)
    slot = step % 2

    @pl.when(step == 0)
    def _():
        for cp in make_copies(step, slot):
            cp.start()

    @pl.when(step + 1 < n_steps)
    def _():
        for cp in make_copies(step + 1, 1 - slot):
            cp.start()

    for cp in make_copies(step, slot):
        cp.wait()
    return slot


BLOCK_ROWS = CMP_BLOCK * PAGE_CH


def _compress_kernel(table_ref, cache_ref, pe_ref, w1_ref, w2_ref, o_ref, buf_ref, sem_ref, *, pp):
    def make_copies(step, slot):
        out = []
        for k in range(pp):
            page = table_ref[step * pp + k]
            for n in range(2):
                out.append(pltpu.make_async_copy(cache_ref.at[page, pl.ds(n * BLOCK_ROWS, BLOCK_ROWS), :],
                                                 buf_ref.at[slot, :, 2 * k + n, :], sem_ref.at[slot]))
        return out

    slot = _stream_pages(make_copies)
    nr = 2 * pp
    for cc in range(2):
        cols = []
        for pos in range(CMP_BLOCK):
            q0 = pos * PAGE_CH + cc * KVH_C
            cols.append(jnp.concatenate([(buf_ref[slot, q0 + h] + pe_ref[q0 + h:q0 + h + 1, :]).astype(BF16)
                                         for h in range(KVH_C)], axis=0))
        flat = jnp.concatenate(cols, axis=1)
        acc = jnp.dot(flat, w1_ref[cc], preferred_element_type=F32)
        res = jnp.dot(jax.nn.gelu(acc).astype(BF16), w2_ref[cc], preferred_element_type=F32)
        for h in range(KVH_C):
            col = (cc * KVH_C + h) * HD_C
            o_ref[:, col:col + HD_C] = res[h * nr:(h + 1) * nr]


def nsa_compress(cache, table, pe, w1, w2, *, pp):
    n_pages = table.shape[0]
    assert n_pages % pp == 0
    nr = 2 * pp
    grid_spec = pltpu.PrefetchScalarGridSpec(
        num_scalar_prefetch=1,
        grid=(n_pages // pp,),
        in_specs=[pl.BlockSpec(memory_space=pl.ANY),
                  pl.BlockSpec((BLOCK_ROWS, HD_C), lambda s, tbl: (0, 0)),
                  pl.BlockSpec((2, CMP_BLOCK * HD_C, CMP_HIDDEN), lambda s, tbl: (0, 0, 0)),
                  pl.BlockSpec((2, CMP_HIDDEN, HD_C), lambda s, tbl: (0, 0, 0))],
        out_specs=pl.BlockSpec((nr, PAGE_CH * HD_C), lambda s, tbl: (s, 0)),
        scratch_shapes=[pltpu.VMEM((2, BLOCK_ROWS, nr, HD_C), F32), pltpu.SemaphoreType.DMA((2,))],
    )
    vmem = 2 * BLOCK_ROWS * nr * HD_C * 4 + 4 * CMP_BLOCK * HD_C * CMP_HIDDEN * 2 + 8 * nr * 512 * 4
    return pl.pallas_call(
        functools.partial(_compress_kernel, pp=pp),
        out_shape=jax.ShapeDtypeStruct((2 * n_pages, PAGE_CH * HD_C), F32),
        grid_spec=grid_spec,
        compiler_params=_cparams(("arbitrary",), vmem),
        name="nsa_compress",
    )(table, cache, pe, w1, w2)


def _nsa_sel_paged_kernel(table_ref, cache_ref, q_ref, sel_ref, bias_ref, knew_ref, vnew_ref, bnew_ref, o_ref,
                          buf_ref, sem_ref, m_ref, l_ref, acc_ref, *, pp, chunks, tq, g):
    hd = HD_C

    def make_copies(step, slot):
        return [pltpu.make_async_copy(cache_ref.at[table_ref[step * pp + k]], buf_ref.at[slot, k], sem_ref.at[slot])
                for k in range(pp)]

    slot = _stream_pages(make_copies)

    def page_rows(ch):
        return jnp.concatenate([buf_ref[slot, k, pl.ds(ch, PAGE_SIZE, stride=PAGE_CH), :] for k in range(pp)], axis=0)

    chunk = pl.program_id(0) % chunks
    nk = pp * PAGE_SIZE
    n_blk = sel_ref.shape[-1]
    scale = hd ** -0.5

    @pl.when(chunk == 0)
    def _():
        m_ref[...] = jnp.full(m_ref.shape, NEG_INF, F32)
        l_ref[...] = jnp.zeros(l_ref.shape, F32)
        acc_ref[...] = jnp.zeros(acc_ref.shape, F32)

    eb = lax.broadcasted_iota(jnp.int32, (n_blk, nk), 0)
    ek = lax.broadcasted_iota(jnp.int32, (n_blk, nk), 1)
    expand = jnp.where(eb == (ek >> SEL_SHIFT), 1.0, 0.0).astype(BF16)
    q = q_ref[...]

    groups = range(KVH_C)
    q8 = [(_stack_heads(q, h * g, g, hd) * scale).astype(BF16) for h in groups]

    def update(lf, vv):
        m_old = [m_ref[h] for h in groups]
        m_new = [jnp.maximum(m_old[h], jnp.max(lf[h], axis=-1, keepdims=True)) for h in groups]
        e = [jnp.exp(lf[h] - m_new[h]) for h in groups]
        alpha = [jnp.exp(m_old[h] - m_new[h]) for h in groups]
        pv = [jnp.dot(e[h].reshape(g * tq, -1).astype(BF16), vv[h], preferred_element_type=F32) for h in groups]
        for h in groups:
            l_ref[h] = alpha[h] * l_ref[h] + jnp.sum(e[h], axis=-1, keepdims=True)
            acc_ref[h] = alpha[h].reshape(g * tq, 1) * acc_ref[h] + pv[h]
            m_ref[h] = m_new[h]

    kk = [page_rows(h).astype(BF16) for h in groups]
    vv = [page_rows(KVH_C + h).astype(BF16) for h in groups]
    selk = [jnp.dot(sel_ref[h].astype(BF16), expand, preferred_element_type=F32) for h in groups]
    lf = [lax.dot_general(q8[h], kk[h], _NT, preferred_element_type=F32).reshape(g, tq, nk)
          + jnp.where((selk[h] > 0.5)[None], bias_ref[h * g:(h + 1) * g], NEG_INF) for h in groups]
    update(lf, vv)

    @pl.when(chunk == chunks - 1)
    def _():
        r = lax.broadcasted_iota(jnp.int32, (tq, tq), 0)
        c = lax.broadcasted_iota(jnp.int32, (tq, tq), 1)
        kn = [knew_ref[:, h * hd:(h + 1) * hd].astype(BF16) for h in groups]
        vn = [vnew_ref[:, h * hd:(h + 1) * hd].astype(BF16) for h in groups]
        lf_new = [lax.dot_general(q8[h], kn[h], _NT, preferred_element_type=F32).reshape(g, tq, tq)
                  + jnp.where((c <= r)[None], bnew_ref[h * g:(h + 1) * g], NEG_INF) for h in groups]
        update(lf_new, vn)
        for h in groups:
            o = acc_ref[h] / jnp.maximum(l_ref[h].reshape(g * tq, 1), 1e-30)
            for j in range(g):
                col = (h * g + j) * hd
                o_ref[:, col:col + hd] = o[j * tq:(j + 1) * tq]


def nsa_sel_paged(q_arr, cache, table, sel, bias, k_new, v_new, bias_new, *, pp, tq):
    b = q_arr.shape[0]
    n_pages = table.shape[0] // b
    chunks = n_pages // pp
    nk = pp * PAGE_SIZE
    n_blk = nk // SEL_BLOCK
    kvw = KVH_C * HD_C
    sel = sel.reshape(b, KVH_C, tq, chunks, n_blk).transpose(0, 3, 1, 2, 4)
    grid_spec = pltpu.PrefetchScalarGridSpec(
        num_scalar_prefetch=1,
        grid=(b * chunks,),
        in_specs=[pl.BlockSpec(memory_space=pl.ANY),
                  pl.BlockSpec((None, tq, D_C), lambda s, tbl: (s // chunks, 0, 0)),
                  pl.BlockSpec((None, None, KVH_C, tq, n_blk), lambda s, tbl: (s // chunks, s % chunks, 0, 0, 0)),
                  pl.BlockSpec((H_C, tq, nk), lambda s, tbl: (0, 0, s % chunks)),
                  pl.BlockSpec((None, tq, kvw), lambda s, tbl: (s // chunks, 0, 0)),
                  pl.BlockSpec((None, tq, kvw), lambda s, tbl: (s // chunks, 0, 0)),
                  pl.BlockSpec((H_C, tq, tq), lambda s, tbl: (0, 0, 0))],
        out_specs=pl.BlockSpec((None, tq, D_C), lambda s, tbl: (s // chunks, 0, 0)),
        scratch_shapes=[pltpu.VMEM((2, pp, PAGE_ROWS, HD_C), F32), pltpu.SemaphoreType.DMA((2,)),
                        pltpu.VMEM((KVH_C, G_C, tq, 1), F32), pltpu.VMEM((KVH_C, G_C, tq, 1), F32),
                        pltpu.VMEM((KVH_C, G_C * tq, HD_C), F32)],
    )
    vmem = 2 * pp * PAGE_SIZE * 2 * kvw * 4 + 2 * H_C * tq * nk * 4 + 10 * G_C * tq * nk * 4 + n_blk * nk * 4
    return pl.pallas_call(
        functools.partial(_nsa_sel_paged_kernel, pp=pp, chunks=chunks, tq=tq, g=G_C),
        out_shape=jax.ShapeDtypeStruct((b, tq, D_C), F32),
        grid_spec=grid_spec,
        compiler_params=_cparams(("arbitrary",), vmem),
        name="nsa_sel_paged",
    )(table, cache, q_arr, sel, bias, k_new, v_new, bias_new)


def _nsa_combine_kernel(gl_ref, gb_ref, oc_ref, os_ref, ow_ref, o_ref):
    gates = jax.nn.sigmoid(gl_ref[...] + gb_ref[...])
    for h in range(H_C):
        sl = slice(h * HD_C, (h + 1) * HD_C)
        o_ref[:, sl] = (gates[:, h:h + 1] * oc_ref[:, sl] + gates[:, H_C + h:H_C + h + 1] * os_ref[:, sl]
                        + gates[:, 2 * H_C + h:2 * H_C + h + 1] * ow_ref[:, sl]).astype(o_ref.dtype)


def nsa_combine(gate_logits, gate_bias, o_c, o_s, o_w):
    m = o_c.shape[0]
    tm = _row_tile(m, 512)
    row = lambda i: (i, 0)
    vmem = 2 * tm * (128 + 3 * D_C) * 4 + 2 * tm * D_C * 2
    return pl.pallas_call(
        _nsa_combine_kernel,
        out_shape=jax.ShapeDtypeStruct((m, D_C), BF16),
        grid=(m // tm,),
        in_specs=[pl.BlockSpec((tm, 128), row), pl.BlockSpec((1, 128), lambda i: (0, 0)),
                  pl.BlockSpec((tm, D_C), row), pl.BlockSpec((tm, D_C), row), pl.BlockSpec((tm, D_C), row)],
        out_specs=pl.BlockSpec((tm, D_C), row),
        compiler_params=_cparams(("parallel",), vmem),
        name="nsa_combine",
    )(gate_logits, gate_bias, o_c, o_s, o_w)


CMP_PAGES_PER_STEP = 32
SEL_PAGES_PER_STEP = 16
Q_PAD = V7X_SUBLANES


def _pad_rows(a, rows, front=0):
    return jnp.pad(a, ((0, 0), (front, rows - a.shape[1] - front), (0, 0)))


def _memo(W, fn, *args):
    key = (fn.__name__,) + args
    if key not in W["tables"]:
        W["tables"][key] = fn(W["rel_bias"], *args)
    return W["tables"][key]


def _past_bias(rel_bias, tq, past_len):
    dist = past_len + jnp.arange(tq, dtype=jnp.int32)[:, None] - jnp.arange(past_len, dtype=jnp.int32)[None, :]
    return _bias_lookup(rel_bias, dist).transpose(2, 0, 1)


def _mixer_ab(x2, b, t, W, l, i, st):
    prompt = st is None
    proj3 = mm_norm(x2, W["mix_norm"][i], W["ab_w_in"][l], W["ab_bias"][l], tn=1536).reshape(b, t, AB_COLS)
    kv_new = proj3[:, :, D_A:A_COLS]
    p_b = proj3[:, :, A_COLS:]
    sinks = W["swa_sinks"][l].reshape(H_A, 1, 1)
    if prompt:
        nw = -(-WIN_A // QBLK)
        sk = (nw + 1) * QBLK
        bias = _memo(W, _toeplitz_bias, QBLK, sk, nw * QBLK)
        o_a = band_attn(proj3, _pad_rows(kv_new, t + nw * QBLK, nw * QBLK),
                        bias, sinks, tq=QBLK, sk=sk, hd=HD_A, n_kvh_step=KVH_A, g=G_A, window=WIN_A,
                        delta=nw * QBLK, kpos_base=-nw * QBLK, kstride=QBLK, out_dtype=BF16)
        buf = kv_new[:, t - min(WIN_A, t):]
        y_b, s_new = rwkv_mix(p_b, jnp.zeros((b, 1, B_COLS), F32), jnp.zeros((b, H_B, HD_B, HD_B), F32),
                              W["rwkv"][l], c=RWKV_CHUNK, t_valid=RWKV_CHUNK)
    else:
        past_len = st["past_len"]
        old = st["swa"][l].reshape(b, -1, 2 * KVH_A * HD_A)
        wb = old.shape[1]
        kv_all = jnp.concatenate([old, kv_new], axis=1)
        buf = kv_all[:, t:]
        sk = -(-(wb + t) // V7X_LANES) * V7X_LANES
        bias = _memo(W, _toeplitz_bias, Q_PAD, sk, wb)
        o_a = band_attn(_pad_rows(proj3, Q_PAD), _pad_rows(kv_all, sk),
                        bias, sinks, tq=Q_PAD, sk=sk, hd=HD_A, n_kvh_step=KVH_A, g=G_A, window=WIN_A,
                        delta=wb, kpos_base=past_len - wb, kstride=0, out_dtype=BF16)[:, :t]
        y_b, s_new = rwkv_mix(_pad_rows(p_b, Q_PAD), st["shift"][l][:, None], st["wkv"][l],
                              W["rwkv"][l], c=Q_PAD, t_valid=t)
        y_b = y_b[:, :t]
    mix_in = jnp.concatenate([o_a, y_b], axis=-1).reshape(b * t, D_MODEL)
    x2 = mm_res(mix_in, W["ab_w_out"][l], x2)
    buf = buf.reshape(b, -1, 2, KVH_A, HD_A)
    return x2, buf, p_b[:, -1], s_new


def _mixer_c(x2, b, t, W, l, i, st):
    prompt = st is None
    kvw = KVH_C * HD_C
    proj3 = mm_norm(x2, W["mix_norm"][i], W["c_w_in"][l], W["c_zero_bias"], tn=1024).reshape(b, t, C_COLS_PAD)
    kv_cmp_new = proj3[:, :, D_C:D_C + 2 * kvw]
    kv_sel_new = proj3[:, :, D_C + 2 * kvw:D_C + 4 * kvw]
    kv_win_new = proj3[:, :, D_C + 4 * kvw:D_C + 6 * kvw]
    gate_logits = proj3[:, :, D_C + C_KV_COLS:D_C + C_KV_COLS + V7X_LANES].reshape(b * t, V7X_LANES)
    pe, w1, w2 = W["cmp_pe"][l], W["cmp_w1"][l], W["cmp_w2"][l]
    if prompt:
        n_pages = b * t // PAGE_SIZE
        kv_c = nsa_compress(kv_cmp_new.reshape(n_pages, PAGE_ROWS, HD_C), jnp.arange(n_pages, dtype=jnp.int32),
                            pe, w1, w2, pp=min(CMP_PAGES_PER_STEP, n_pages)).reshape(b, t // CMP_BLOCK, 2 * kvw)
        o_c, sel = nsa_cmp(proj3, kv_c, tq=QBLK, q0=0)
        o_s = nsa_sel_prompt(proj3, kv_sel_new[:, :, :kvw], kv_sel_new[:, :, kvw:], sel,
                             _memo(W, sel_bias_tiles, QBLK, t), tq=QBLK)
        nw = -(-WIN_C // QBLK)
        sk = (nw + 1) * QBLK
        bias = _memo(W, _toeplitz_bias, QBLK, sk, nw * QBLK)
        o_w = band_attn(proj3, _pad_rows(kv_win_new, t + nw * QBLK, nw * QBLK), bias, None,
                        tq=QBLK, sk=sk, hd=HD_C, n_kvh_step=1, g=G_C, window=WIN_C,
                        delta=nw * QBLK, kpos_base=-nw * QBLK, kstride=QBLK, out_dtype=F32)
        win_buf = kv_win_new[:, t - min(WIN_C, t):]
    else:
        past_len = st["past_len"]
        n_phys = st["cmp"].shape[1]
        table = st["page_table"].reshape(-1) + l * n_phys
        kv_c = nsa_compress(st["cmp"].reshape(-1, PAGE_ROWS, HD_C), table, pe, w1, w2,
                            pp=CMP_PAGES_PER_STEP).reshape(b, past_len // CMP_BLOCK, 2 * kvw)
        q8 = _pad_rows(proj3, Q_PAD)
        o_c, sel = nsa_cmp(q8, kv_c, tq=Q_PAD, q0=past_len)
        bias_past = _memo(W, _past_bias, Q_PAD, past_len)
        bias_new = _memo(W, _toeplitz_bias, Q_PAD, Q_PAD, 0)
        sel_new = _pad_rows(kv_sel_new, Q_PAD)
        o_s = nsa_sel_paged(q8, st["sel"].reshape(-1, PAGE_ROWS, HD_C), table, sel, bias_past,
                            sel_new[:, :, :kvw], sel_new[:, :, kvw:], bias_new, pp=SEL_PAGES_PER_STEP, tq=Q_PAD)
        old = st["win"][l].reshape(b, -1, 2 * kvw)
        wb = old.shape[1]
        kv_all = jnp.concatenate([old, kv_win_new], axis=1)
        win_buf = kv_all[:, t:]
        sk = -(-(wb + t) // V7X_LANES) * V7X_LANES
        bias = _memo(W, _toeplitz_bias, Q_PAD, sk, wb)
        o_w = band_attn(q8, _pad_rows(kv_all, sk), bias, None, tq=Q_PAD, sk=sk, hd=HD_C,
                        n_kvh_step=1, g=G_C, window=WIN_C, delta=wb, kpos_base=past_len - wb, kstride=0,
                        out_dtype=F32)
        o_c, o_s, o_w = o_c[:, :t], o_s[:, :t], o_w[:, :t]
    comb = nsa_combine(gate_logits, W["c_gate_b"][l], o_c.reshape(b * t, D_C), o_s.reshape(b * t, D_C),
                       o_w.reshape(b * t, D_C))
    x2 = mm_res(comb, W["c_w_out"][l], x2)
    shape5 = lambda a: a.reshape(b, -1, 2, KVH_C, HD_C)
    return x2, shape5(kv_cmp_new), shape5(kv_sel_new), shape5(win_buf)


def _trunk(x, p, W, st):
    b, t, _ = x.shape
    x2 = x.reshape(b * t, D_MODEL)
    swa_l, shift_l, wkv_l, cmp_l, sel_l, win_l = [], [], [], [], [], []
    for i in range(DEPTH):
        l = i // 2
        if i % 2 == 0:
            x2, buf, shift_new, s_new = _mixer_ab(x2, b, t, W, l, i, st)
            swa_l.append(buf)
            shift_l.append(shift_new)
            wkv_l.append(s_new)
        else:
            x2, cmp_new, sel_new, win_buf = _mixer_c(x2, b, t, W, l, i, st)
            cmp_l.append(cmp_new)
            sel_l.append(sel_new)
            win_l.append(win_buf)
        act = ffn_up(x2, W["ffn_norm"][i], W["ffn_w_gate"][i], W["ffn_w_up"][i])
        x2 = mm_res(act, W["ffn_w_down"][i], x2, tk=D_FF // 2)
        x2 = ple(x2, p[i].reshape(b * t, PLE_DIM), W["ple_gate_norm"][i], W["ple_w_gate"][i], W["ple_w_proj"][i],
                 W["ple_post_norm"][i], W["final_norm"], final=(i == DEPTH - 1))
    y = x2.reshape(b, t, D_MODEL)
    return (y, jnp.stack(swa_l), jnp.stack(shift_l), jnp.stack(wkv_l), jnp.stack(cmp_l), jnp.stack(sel_l),
            jnp.stack(win_l))


def kernel(x_prompt, x_sample, state_swa_kv, state_rwkv_shift, state_rwkv_wkv, cache_nsa_cmp_kv, cache_nsa_sel_kv, state_nsa_win_kv, page_table, p_prompt, p_sample, rel_bias, mix_norm, ab_w_in, ab_b_qkv, swa_sinks, rwkv_mu, rwkv_w0, rwkv_w2, rwkv_a0, rwkv_a2, rwkv_g2, rwkv_k_k, rwkv_k_a, rwkv_r_k, rwkv_ln_g, rwkv_ln_b, ab_w_out, c_w_in, c_gate_b, nsa_cmp_pos, nsa_cmp_w1, nsa_cmp_w2, c_w_out, ffn_norm, ffn_w_gate, ffn_w_up, ffn_w_down, ple_w_proj, ple_gate_norm, ple_w_gate, ple_post_norm, final_norm):
    n_ab, n_c = ab_w_in.shape[0], c_w_in.shape[0]
    bf = lambda a: a.astype(BF16)
    row = lambda a: a.reshape(a.shape[0], 1, -1).astype(F32)
    pe = jnp.broadcast_to(nsa_cmp_pos.transpose(0, 2, 1, 3)[:, :, :, None, :],
                          (n_c, CMP_BLOCK, 2, KVH_C, HD_C)).reshape(n_c, BLOCK_ROWS, HD_C)
    W = dict(
        tables={}, rel_bias=rel_bias,mix_norm=row(mix_norm), ab_w_in=bf(ab_w_in),
        ab_bias=jnp.pad(ab_b_qkv, ((0, 0), (0, AB_COLS - A_COLS))).reshape(n_ab, 1, AB_COLS),
        swa_sinks=swa_sinks, ab_w_out=bf(ab_w_out),
        rwkv=[dict(mu=rwkv_mu[l][None], w0=rwkv_w0[l][None], w2=bf(rwkv_w2[l]), a0=rwkv_a0[l][None], a2=bf(rwkv_a2[l]),
                   g2=bf(rwkv_g2[l]), k_k=rwkv_k_k[l][None], k_a=rwkv_k_a[l][None], r_k=rwkv_r_k[l].reshape(1, C_B),
                   ln_g=rwkv_ln_g[l][None], ln_b=rwkv_ln_b[l][None]) for l in range(n_ab)],
        c_w_in=bf(jnp.pad(c_w_in, ((0, 0), (0, 0), (0, C_COLS_PAD - C_COLS)))),
        c_zero_bias=jnp.zeros((1, C_COLS_PAD), F32),
        c_gate_b=jnp.pad(c_gate_b, ((0, 0), (0, V7X_LANES - 3 * H_C))).reshape(n_c, 1, V7X_LANES),
        cmp_pe=pe, cmp_w1=bf(nsa_cmp_w1), cmp_w2=bf(nsa_cmp_w2), c_w_out=bf(c_w_out),
        ffn_norm=row(ffn_norm), ffn_w_gate=bf(ffn_w_gate), ffn_w_up=bf(ffn_w_up), ffn_w_down=bf(ffn_w_down),
        ple_w_proj=bf(ple_w_proj), ple_gate_norm=row(ple_gate_norm), ple_w_gate=bf(ple_w_gate),
        ple_post_norm=row(ple_post_norm), final_norm=final_norm.reshape(1, D_MODEL),
    )
    st = dict(swa=state_swa_kv, shift=state_rwkv_shift, wkv=state_rwkv_wkv, cmp=cache_nsa_cmp_kv,
              sel=cache_nsa_sel_kv, win=state_nsa_win_kv, page_table=page_table,
              past_len=page_table.shape[1] * PAGE_SIZE)
    y_p, swa_p, shift_p, wkv_p, cmp_p, sel_p, win_p = _trunk(x_prompt, p_prompt, W, None)
    y_s, swa_s, shift_s, wkv_s, cmp_s, sel_s, win_s = _trunk(x_sample, p_sample, W, st)
    return (y_p, y_s, swa_p, swa_s, shift_p, shift_s, wkv_p, wkv_s, cmp_p, cmp_s, sel_p, sel_s, win_p, win_s)
```

```python
import functools
import math

import jax
import jax.numpy as jnp
import numpy as np
from jax import lax
from jax.experimental import pallas as pl
from jax.experimental.pallas import tpu as pltpu

F32 = jnp.float32
BF16 = jnp.bfloat16

D_MODEL = 2048
DEPTH = 4
PAGE_SIZE = 128
PLE_DIM = 256
N_BUCKETS = 32
REL_MAX_DIST = 1024
RMS_EPS = 1e-6
D_FF = 5632
QBLK = 128
HD_A = 64
H_A = 16
KVH_A = 2
G_A = 8
D_A = 1024
WIN_A = 128
A_COLS = D_A + 2 * KVH_A * HD_A
HD_B = 64
C_B = 1024
H_B = 16
LORA_W = 64
LORA_A = 64
LORA_G = 128
B_COLS = 3 * C_B + LORA_W + LORA_A + LORA_G
AB_COLS = A_COLS + B_COLS
GN_EPS = 64e-5
HD_C = 128
H_C = 16
KVH_C = 2
G_C = 8
D_C = 2048
CMP_BLOCK = 64
SEL_BLOCK = 64
SEL_SHIFT = 6
N_TOP = 15
WIN_C = 512
CMP_HIDDEN = 128
C_KV_COLS = 6 * KVH_C * HD_C
C_COLS = D_C + C_KV_COLS + 3 * H_C
NEG_INF = -1e30
FORCE_SCORE = 1e4

V7X_LANES = 128
V7X_SUBLANES = 8
V7X_VMEM_BYTES = 64 * 1024 * 1024
VMEM_LIMIT_CAP = V7X_VMEM_BYTES - 8 * 1024 * 1024

C_COLS_PAD = 4096
RWKV_CHUNK = 64


def _cparams(sem, vmem_bytes):
    limit = int(min(max(2 * vmem_bytes, 32 * 1024 * 1024), VMEM_LIMIT_CAP))
    return pltpu.CompilerParams(dimension_semantics=sem, vmem_limit_bytes=limit)


def _row_tile(m, cap):
    t = min(m, cap)
    assert m % t == 0, (m, t)
    return t


def _rms(x, g):
    return x * lax.rsqrt(jnp.mean(x * x, axis=-1, keepdims=True) + RMS_EPS) * g


def t5_bucket(dist):
    n = jnp.maximum(dist, 0)
    max_exact = N_BUCKETS // 2
    nf = jnp.maximum(n, max_exact).astype(F32)
    large = max_exact + (jnp.log(nf / max_exact) / math.log(REL_MAX_DIST / max_exact) * (N_BUCKETS - max_exact)).astype(jnp.int32)
    return jnp.where(n < max_exact, n, jnp.minimum(large, N_BUCKETS - 1))


def _mm_norm_kernel(x_ref, g_ref, w_ref, b_ref, o_ref, h_ref):
    @pl.when(pl.program_id(1) == 0)
    def _():
        h_ref[...] = _rms(x_ref[...], g_ref[...]).astype(BF16)

    o_ref[...] = jnp.dot(h_ref[...], w_ref[...], preferred_element_type=F32) + b_ref[...]


def mm_norm(x, g, w, b, *, tn=512):
    m, k = x.shape
    n = w.shape[1]
    tm = _row_tile(m, 1024)
    vmem = 2 * tm * k * 4 + tm * k * 2 + 2 * k * tn * 2 + 2 * tm * tn * 4
    return pl.pallas_call(
        _mm_norm_kernel,
        out_shape=jax.ShapeDtypeStruct((m, n), F32),
        grid=(m // tm, n // tn),
        in_specs=[pl.BlockSpec((tm, k), lambda i, j: (i, 0)),
                  pl.BlockSpec((1, k), lambda i, j: (0, 0)),
                  pl.BlockSpec((k, tn), lambda i, j: (0, j)),
                  pl.BlockSpec((1, tn), lambda i, j: (0, j))],
        out_specs=pl.BlockSpec((tm, tn), lambda i, j: (i, j)),
        scratch_shapes=[pltpu.VMEM((tm, k), BF16)],
        compiler_params=_cparams(("parallel", "arbitrary"), vmem),
        name="mm_norm",
    )(x, g, w, b)


def _mm_res_kernel(a_ref, w_ref, r_ref, o_ref, acc_ref, *, nk):
    kk = pl.program_id(2)

    @pl.when(kk == 0)
    def _():
        acc_ref[...] = jnp.zeros_like(acc_ref)

    acc_ref[...] += jnp.dot(a_ref[...], w_ref[...], preferred_element_type=F32)

    @pl.when(kk == nk - 1)
    def _():
        o_ref[...] = r_ref[...] + acc_ref[...]


def mm_res(a, w, r, *, tn=1024, tk=None):
    m, k = a.shape
    n = w.shape[1]
    tm = _row_tile(m, 1024)
    tk = k if tk is None else tk
    nk = k // tk
    vmem = 2 * tm * tk * 2 + 2 * tk * tn * 2 + 5 * tm * tn * 4
    return pl.pallas_call(
        functools.partial(_mm_res_kernel, nk=nk),
        out_shape=jax.ShapeDtypeStruct((m, n), F32),
        grid=(m // tm, n // tn, nk),
        in_specs=[pl.BlockSpec((tm, tk), lambda i, j, q: (i, q)),
                  pl.BlockSpec((tk, tn), lambda i, j, q: (q, j)),
                  pl.BlockSpec((tm, tn), lambda i, j, q: (i, j))],
        out_specs=pl.BlockSpec((tm, tn), lambda i, j, q: (i, j)),
        scratch_shapes=[pltpu.VMEM((tm, tn), F32)],
        compiler_params=_cparams(("parallel", "parallel", "arbitrary"), vmem),
        name="mm_res",
    )(a, w, r)


def _ffn_up_kernel(x_ref, g_ref, wg_ref, wu_ref, o_ref, h_ref):
    @pl.when(pl.program_id(1) == 0)
    def _():
        h_ref[...] = _rms(x_ref[...], g_ref[...]).astype(BF16)

    h = h_ref[...]
    gate = jnp.dot(h, wg_ref[...], preferred_element_type=F32)
    up = jnp.dot(h, wu_ref[...], preferred_element_type=F32)
    o_ref[...] = (jax.nn.silu(gate) * up).astype(BF16)


def ffn_up(x, g, wg, wu, *, tn=512):
    m, k = x.shape
    n = wg.shape[1]
    tm = _row_tile(m, 1024)
    vmem = 2 * tm * k * 4 + tm * k * 2 + 4 * k * tn * 2 + 2 * tm * tn * 2 + 3 * tm * tn * 4
    return pl.pallas_call(
        _ffn_up_kernel,
        out_shape=jax.ShapeDtypeStruct((m, n), BF16),
        grid=(m // tm, n // tn),
        in_specs=[pl.BlockSpec((tm, k), lambda i, j: (i, 0)),
                  pl.BlockSpec((1, k), lambda i, j: (0, 0)),
                  pl.BlockSpec((k, tn), lambda i, j: (0, j)),
                  pl.BlockSpec((k, tn), lambda i, j: (0, j))],
        out_specs=pl.BlockSpec((tm, tn), lambda i, j: (i, j)),
        scratch_shapes=[pltpu.VMEM((tm, k), BF16)],
        compiler_params=_cparams(("parallel", "arbitrary"), vmem),
        name="ffn_up",
    )(x, g, wg, wu)


def _ple_kernel(x_ref, p_ref, gn_ref, wg_ref, wp_ref, pn_ref, fn_ref, o_ref, *, final):
    x = x_ref[...]
    h = _rms(x, gn_ref[...]).astype(BF16)
    gate = jax.nn.sigmoid(jnp.dot(h, wg_ref[...], preferred_element_type=F32))
    e = jnp.dot(p_ref[...].astype(BF16), wp_ref[...], preferred_element_type=F32)
    x = x + _rms(gate * e, pn_ref[...])
    if final:
        x = _rms(x, fn_ref[...])
    o_ref[...] = x


def ple(x, p, gn, wg, wp, pn, fn, *, final):
    m, d = x.shape
    tm = _row_tile(m, 512)
    vmem = 4 * tm * d * 4 + 2 * d * d * 2 + 2 * PLE_DIM * d * 2 + 4 * tm * d * 4
    row = lambda i: (i, 0)
    fix = lambda i: (0, 0)
    return pl.pallas_call(
        functools.partial(_ple_kernel, final=final),
        out_shape=jax.ShapeDtypeStruct((m, d), F32),
        grid=(m // tm,),
        in_specs=[pl.BlockSpec((tm, d), row), pl.BlockSpec((tm, PLE_DIM), row),
                  pl.BlockSpec((1, d), fix), pl.BlockSpec((d, d), fix),
                  pl.BlockSpec((PLE_DIM, d), fix), pl.BlockSpec((1, d), fix),
                  pl.BlockSpec((1, d), fix)],
        out_specs=pl.BlockSpec((tm, d), row),
        compiler_params=_cparams(("parallel",), vmem),
        name="ple",
    )(x, p, gn, wg, wp, pn, fn)


def _masked_softmax(logits, mask, sink=None):
    lf = jnp.where(mask, logits, NEG_INF)
    m = jnp.max(lf, axis=-1, keepdims=True)
    if sink is not None:
        m = jnp.maximum(m, sink)
    e = jnp.where(mask, jnp.exp(lf - m), 0.0)
    den = jnp.sum(e, axis=-1, keepdims=True)
    if sink is not None:
        den = den + jnp.exp(sink - m)
    return e / jnp.maximum(den, 1e-30)


def _stack_heads(q, h0, g, hd):
    return jnp.concatenate([q[:, (h0 + j) * hd:(h0 + j + 1) * hd] for j in range(g)], axis=0)


def _band_attn_kernel(q_ref, k_ref, v_ref, bias_ref, sink_ref, o_ref, *,
                      tq, sk, hd, n_kvh, g, window, delta, kpos_base, kstride, has_sink):
    tl = V7X_LANES
    n_tiles = sk // tl
    ks = pl.multiple_of(pl.program_id(2) * kstride, V7X_SUBLANES)
    kslab = k_ref[pl.ds(ks, sk), :].astype(BF16)
    vslab = v_ref[pl.ds(ks, sk), :].astype(BF16)
    r = lax.broadcasted_iota(jnp.int32, (tq, tl), 0)
    c = lax.broadcasted_iota(jnp.int32, (tq, tl), 1)
    masks = []
    for t in range(n_tiles):
        dist = delta + r - (c + t * tl)
        masks.append(((dist >= 0) & (dist <= window) & (kpos_base + ks + t * tl + c >= 0))[None])
    q = q_ref[...] * (hd ** -0.5)
    chains = range(n_kvh)
    q8 = [_stack_heads(q, h * g, g, hd).astype(BF16) for h in chains]
    lf = [[lax.dot_general(q8[h], kslab[t * tl:(t + 1) * tl, h * hd:(h + 1) * hd], _NT,
                           preferred_element_type=F32).reshape(g, tq, tl)
           + jnp.where(masks[t], bias_ref[h * g:(h + 1) * g, :, t * tl:(t + 1) * tl], NEG_INF)
           for t in range(n_tiles)] for h in chains]
    m = [jnp.max(functools.reduce(jnp.maximum, lf[h]), axis=-1, keepdims=True) for h in chains]
    if has_sink:
        m = [jnp.maximum(m[h], sink_ref[h * g:(h + 1) * g]) for h in chains]
    mfull = [jnp.broadcast_to(m[h], (g, tq, tl)) for h in chains]
    e = [[jnp.exp(lf[h][t] - mfull[h]) for t in range(n_tiles)] for h in chains]
    acc = [functools.reduce(jnp.add, [jnp.dot(e[h][t].reshape(g * tq, tl).astype(BF16),
                                              vslab[t * tl:(t + 1) * tl, h * hd:(h + 1) * hd],
                                              preferred_element_type=F32) for t in range(n_tiles)])
           for h in chains]
    den = [jnp.sum(functools.reduce(jnp.add, e[h]), axis=-1, keepdims=True) for h in chains]
    if has_sink:
        den = [den[h] + jnp.exp(sink_ref[h * g:(h + 1) * g] - m[h]) for h in chains]
    for h in chains:
        o = acc[h] / jnp.maximum(den[h].reshape(g * tq, 1), 1e-30)
        for j in range(g):
            o_ref[:, (h * g + j) * hd:(h * g + j + 1) * hd] = o[j * tq:(j + 1) * tq].astype(o_ref.dtype)


def band_attn(q_arr, kv_arr, bias, sink, *, tq, sk, hd, n_kvh_step, g, window, delta,
              kpos_base, kstride, out_dtype):
    b, t = q_arr.shape[:2]
    tk = kv_arr.shape[1]
    n_kv_blocks = kv_arr.shape[2] // (2 * n_kvh_step * hd)
    qw = n_kvh_step * g * hd
    has_sink = sink is not None
    if not has_sink:
        sink = jnp.zeros((n_kv_blocks * n_kvh_step * g, 1, 1), F32)
    vmem = 2 * tq * qw * 4 * 2 + 4 * tk * n_kvh_step * hd * 4 + 2 * n_kvh_step * g * tq * sk * 4 + 6 * g * tq * sk * 4
    kern = functools.partial(_band_attn_kernel, tq=tq, sk=sk, hd=hd, n_kvh=n_kvh_step, g=g, window=window,
                             delta=delta, kpos_base=kpos_base, kstride=kstride, has_sink=has_sink)
    return pl.pallas_call(
        kern,
        out_shape=jax.ShapeDtypeStruct((b, t, n_kv_blocks * qw), out_dtype),
        grid=(b, n_kv_blocks, t // tq),
        in_specs=[pl.BlockSpec((None, tq, qw), lambda bb, kv, i: (bb, i, kv)),
                  pl.BlockSpec((None, tk, n_kvh_step * hd), lambda bb, kv, i: (bb, 0, kv)),
                  pl.BlockSpec((None, tk, n_kvh_step * hd), lambda bb, kv, i: (bb, 0, n_kv_blocks + kv)),
                  pl.BlockSpec((n_kvh_step * g, tq, sk), lambda bb, kv, i: (kv, 0, 0)),
                  pl.BlockSpec((n_kvh_step * g, 1, 1), lambda bb, kv, i: (kv, 0, 0))],
        out_specs=pl.BlockSpec((None, tq, qw), lambda bb, kv, i: (bb, i, kv)),
        compiler_params=_cparams(("parallel", "parallel", "arbitrary"), vmem),
        name="band_attn",
    )(q_arr, kv_arr, kv_arr, bias, sink)


def _bias_lookup(rel_bias, dist):
    onehot = (t5_bucket(dist)[..., None] == jnp.arange(N_BUCKETS, dtype=jnp.int32)).astype(F32)
    return jnp.einsum("...k,kh->...h", onehot, rel_bias.astype(F32), precision=lax.Precision.HIGHEST)


def _toeplitz_bias(rel_bias, tq, sk, delta):
    dist = delta + jnp.arange(tq, dtype=jnp.int32)[:, None] - jnp.arange(sk, dtype=jnp.int32)[None, :]
    return _bias_lookup(rel_bias, dist).transpose(2, 0, 1)


def _nsa_cmp_kernel(q_ref, kv_ref, oc_ref, sel_ref, *, tq, n_cmp, g, q0):
    hd = HD_C
    qp = q0 + pl.program_id(1) * tq + lax.broadcasted_iota(jnp.int32, (tq, n_cmp), 0)
    blk = lax.broadcasted_iota(jnp.int32, (tq, n_cmp), 1)
    cmask = ((blk + 1) * CMP_BLOCK <= qp + 1)[None]
    cur = qp >> SEL_SHIFT
    groups = range(KVH_C)
    q = q_ref[...]
    scores = []
    for h in groups:
        q8 = _stack_heads(q, h * g, g, hd)
        kc = kv_ref[:, h * hd:(h + 1) * hd]
        vc = kv_ref[:, (KVH_C + h) * hd:(KVH_C + h + 1) * hd].astype(BF16)
        q_hi = q8.astype(BF16)
        q_lo = (q8 - q_hi.astype(F32)).astype(BF16)
        k_hi = kc.astype(BF16)
        k_lo = (kc - k_hi.astype(F32)).astype(BF16)
        cl = (lax.dot_general(q_hi, k_hi, _NT, preferred_element_type=F32)
              + lax.dot_general(q_hi, k_lo, _NT, preferred_element_type=F32)
              + lax.dot_general(q_lo, k_hi, _NT, preferred_element_type=F32)) * (hd ** -0.5)
        p_c = _masked_softmax(cl.reshape(g, tq, n_cmp), cmask)
        o = jnp.dot(p_c.reshape(g * tq, n_cmp).astype(BF16), vc, preferred_element_type=F32)
        for j in range(g):
            oc_ref[:, (h * g + j) * hd:(h * g + j + 1) * hd] = o[j * tq:(j + 1) * tq]
        imp = jnp.sum(p_c, axis=0) + jnp.where(blk == 0, FORCE_SCORE, 0.0)
        scores.append(jnp.where(blk < cur, imp, -1.0))
    transposed = tq % V7X_LANES == 0
    if transposed:
        work = [scores[h].T for h in groups]
        idx = lax.broadcasted_iota(jnp.int32, (n_cmp, tq), 0)
    else:
        work = scores
        idx = blk
    ranks = [jnp.zeros(work[0].shape, F32) for _ in groups]
    for k in range(n_cmp):
        for h in groups:
            other = work[h][k:k + 1, :] if transposed else work[h][:, k:k + 1]
            ranks[h] = ranks[h] + jnp.where(idx > k, jnp.where(other >= work[h], 1.0, 0.0),
                                            jnp.where(other > work[h], 1.0, 0.0))
    if transposed:
        ranks = [ranks[h].T for h in groups]
    for h in groups:
        sel_ref[h] = jnp.where(((ranks[h] < N_TOP) & (scores[h] >= 0.0)) | (blk == cur), 1.0, 0.0)


def nsa_cmp(q_arr, kv_c, *, tq, q0):
    b, t = q_arr.shape[:2]
    n_cmp = kv_c.shape[1]
    assert n_cmp >= N_TOP
    kvw = 2 * KVH_C * HD_C
    vmem = 4 * tq * D_C * 4 + 2 * n_cmp * kvw * 4 + 16 * G_C * tq * n_cmp * 4
    return pl.pallas_call(
        functools.partial(_nsa_cmp_kernel, tq=tq, n_cmp=n_cmp, g=G_C, q0=q0),
        out_shape=(jax.ShapeDtypeStruct((b, t, D_C), F32),
                   jax.ShapeDtypeStruct((b, KVH_C, t, n_cmp), F32)),
        grid=(b, t // tq),
        in_specs=[pl.BlockSpec((None, tq, D_C), lambda bb, i: (bb, i, 0)),
                  pl.BlockSpec((None, n_cmp, kvw), lambda bb, i: (bb, 0, 0))],
        out_specs=(pl.BlockSpec((None, tq, D_C), lambda bb, i: (bb, i, 0)),
                   pl.BlockSpec((None, KVH_C, tq, n_cmp), lambda bb, i: (bb, 0, i, 0))),
        compiler_params=_cparams(("parallel", "arbitrary"), vmem),
        name="nsa_cmp",
    )(q_arr, kv_c)


SEL_TILES_PER_TRIP = 4


def _nsa_sel_kernel(q_ref, k_ref, v_ref, sel_ref, bias_ref, o_ref, selk_ref, mx_ref, le_ref, acc_ref, lf_ref, *,
                    tq, t, g, n_far):
    hd = HD_C
    i = pl.program_id(2)
    n_blk = t // SEL_BLOCK
    eb = lax.broadcasted_iota(jnp.int32, (n_blk, t), 0)
    ek = lax.broadcasted_iota(jnp.int32, (n_blk, t), 1)
    expand = jnp.where((ek >> SEL_SHIFT) == eb, 1.0, 0.0).astype(BF16)
    selk_ref[...] = jnp.dot(sel_ref[...].astype(BF16), expand, preferred_element_type=F32)
    q8 = (_stack_heads(q_ref[...], 0, g, hd) * (hd ** -0.5)).astype(BF16)
    r = lax.broadcasted_iota(jnp.int32, (tq, tq), 0)
    c = lax.broadcasted_iota(jnp.int32, (tq, tq), 1)

    last = t // tq - 1

    def tile_start(j):
        return pl.multiple_of(jnp.minimum(j, last) * tq, tq)

    def logits(j):
        ks = tile_start(j)
        kj = k_ref[pl.ds(ks, tq), :].astype(BF16)
        s = lax.dot_general(q8, kj, _NT, preferred_element_type=F32).reshape(g, tq, tq)
        mask = ((selk_ref[:, pl.ds(ks, tq)] > 0.5) & ((j - i) * tq + c <= r))[None]
        return s + jnp.where(mask, bias_ref[jnp.clip(i - j, 0, n_far)], NEG_INF)

    per = SEL_TILES_PER_TRIP
    n_trips = (i + per) // per
    mx_ref[...] = jnp.full(mx_ref.shape, NEG_INF, F32)

    def sweep_max(jj, carry):
        tiles = [logits(per * jj + u) for u in range(per)]
        for u in range(per):
            lf_ref[jnp.minimum(per * jj + u, last)] = tiles[u]
        mx_ref[...] = jnp.maximum(mx_ref[...], functools.reduce(jnp.maximum, tiles))
        return carry

    lax.fori_loop(0, n_trips, sweep_max, 0)
    mx_ref[...] = jnp.broadcast_to(jnp.max(mx_ref[...], axis=-1, keepdims=True), mx_ref.shape)
    le_ref[...] = jnp.zeros(le_ref.shape, F32)
    acc_ref[...] = jnp.zeros(acc_ref.shape, F32)

    def sweep_acc(jj, carry):
        mx = mx_ref[...]
        es = [jnp.exp(lf_ref[jnp.minimum(per * jj + u, last)] - mx) for u in range(per)]
        le_ref[...] += functools.reduce(jnp.add, es)
        pvs = [jnp.dot(es[u].reshape(g * tq, tq).astype(BF16),
                       v_ref[pl.ds(tile_start(per * jj + u), tq), :].astype(BF16), preferred_element_type=F32)
               for u in range(per)]
        acc_ref[...] += functools.reduce(jnp.add, pvs)
        return carry

    lax.fori_loop(0, n_trips, sweep_acc, 0)
    den = jnp.sum(le_ref[...], axis=-1, keepdims=True).reshape(g * tq, 1)
    o = acc_ref[...] / jnp.maximum(den, 1e-30)
    for j in range(g):
        o_ref[:, j * hd:(j + 1) * hd] = o[j * tq:(j + 1) * tq]


def _bias_saturation_offset(tq, t):
    d = np.arange(0, t + tq, dtype=np.float64)
    nf = np.maximum(d, N_BUCKETS // 2)
    large = N_BUCKETS // 2 + np.floor(np.log(nf / (N_BUCKETS // 2)) / math.log(REL_MAX_DIST / (N_BUCKETS // 2))
                                      * (N_BUCKETS - N_BUCKETS // 2) - 1e-3)
    saturated = np.where(d < N_BUCKETS // 2, 0, large) >= N_BUCKETS - 1
    if not saturated.any():
        return t // tq
    first_sat = int(np.argmax(saturated))
    return min(t // tq, -(-(first_sat + tq) // tq))


def sel_bias_tiles(rel_bias, tq, t):
    n_far = _bias_saturation_offset(tq, t)
    dist = (jnp.arange(n_far + 1, dtype=jnp.int32)[:, None, None] * tq
            + jnp.arange(tq, dtype=jnp.int32)[None, :, None] - jnp.arange(tq, dtype=jnp.int32)[None, None, :])
    tiles = _bias_lookup(rel_bias, dist)
    return tiles.reshape(n_far + 1, tq, tq, KVH_C, G_C).transpose(3, 0, 4, 1, 2)


def nsa_sel_prompt(q_arr, k_arr, v_arr, sel, tiles, *, tq):
    b, t = q_arr.shape[:2]
    n_blk = t // SEL_BLOCK
    n_far = tiles.shape[1] - 1
    n_tiles = t // tq
    assert n_tiles % SEL_TILES_PER_TRIP == 0
    qw = G_C * HD_C
    lf_bytes = n_tiles * G_C * tq * tq * 4
    vmem = (4 * tq * qw * 4 + 4 * t * HD_C * 4 + 2 * (n_far + 1) * G_C * tq * tq * 4 + tq * t * 4
            + 8 * G_C * tq * tq * 4 + n_blk * t * 4 + lf_bytes)
    return pl.pallas_call(
        functools.partial(_nsa_sel_kernel, tq=tq, t=t, g=G_C, n_far=n_far),
        out_shape=jax.ShapeDtypeStruct((b, t, D_C), F32),
        grid=(b, KVH_C, t // tq),
        in_specs=[pl.BlockSpec((None, tq, qw), lambda bb, kv, i: (bb, i, kv)),
                  pl.BlockSpec((None, t, HD_C), lambda bb, kv, i: (bb, 0, kv)),
                  pl.BlockSpec((None, t, HD_C), lambda bb, kv, i: (bb, 0, kv)),
                  pl.BlockSpec((None, None, tq, n_blk), lambda bb, kv, i: (bb, kv, i, 0)),
                  pl.BlockSpec((None, n_far + 1, G_C, tq, tq), lambda bb, kv, i: (kv, 0, 0, 0, 0))],
        out_specs=pl.BlockSpec((None, tq, qw), lambda bb, kv, i: (bb, i, kv)),
        scratch_shapes=[pltpu.VMEM((tq, t), F32), pltpu.VMEM((G_C, tq, tq), F32),
                        pltpu.VMEM((G_C, tq, tq), F32), pltpu.VMEM((G_C * tq, HD_C), F32),
                        pltpu.VMEM((n_tiles, G_C, tq, tq), F32)],
        compiler_params=_cparams(("parallel", "parallel", "arbitrary"), vmem),
        name="nsa_sel_prompt",
    )(q_arr, k_arr, v_arr, sel, tiles)


def _bdot(a, b, dims=(((1,), (0,)), ((), ()))):
    return lax.dot_general(a.astype(BF16), b.astype(BF16), dims, preferred_element_type=F32)


_NT = (((1,), (1,)), ((), ()))
_TN = (((0,), (0,)), ((), ()))


def _rwkv_kernel(p_ref, shift_ref, s0_ref, mu_ref, w0_ref, w2_ref, a0_ref, a2_ref, g2_ref, kk_ref, ka_ref,
                 rk_ref, lng_ref, lnb_ref, y_ref, sout_ref, carry_ref, state_ref, *, c, t_valid, n_chunks):
    ci = pl.program_id(1)

    @pl.when(ci == 0)
    def _():
        carry_ref[...] = shift_ref[...]
        state_ref[...] = s0_ref[...]

    p = p_ref[0]
    row = lax.broadcasted_iota(jnp.int32, (c, 1), 0)
    prev = jnp.where(row == 0, carry_ref[...], pltpu.roll(p, 1, axis=0))
    carry_ref[...] = p[c - 1:c, :]
    xs = p + (prev - p) * mu_ref[...]
    o = 3 * C_B
    r = xs[:, :C_B]
    k = xs[:, C_B:2 * C_B]
    v = xs[:, 2 * C_B:o]
    wd = xs[:, o:o + LORA_W]
    ad = xs[:, o + LORA_W:o + LORA_W + LORA_A]
    gd = xs[:, o + LORA_W + LORA_A:]
    w_raw = w0_ref[...] + _bdot(jnp.tanh(wd), w2_ref[...])
    logd = -jnp.exp(-jax.nn.softplus(-w_raw) - 0.5)
    a = jax.nn.sigmoid(a0_ref[...] + _bdot(ad, a2_ref[...]))
    gate = _bdot(jax.nn.sigmoid(gd), g2_ref[...])
    kk = k * kk_ref[...]
    k = k * (1.0 + (a - 1.0) * ka_ref[...])
    if t_valid < c:
        valid = row < t_valid
        logd = jnp.where(valid, logd, 0.0)
        r = jnp.where(valid, r, 0.0)
        k = jnp.where(valid, k, 0.0)
        v = jnp.where(valid, v, 0.0)
        kk = jnp.where(valid, kk, 0.0)
    ti = lax.broadcasted_iota(jnp.int32, (c, c), 0)
    si = lax.broadcasted_iota(jnp.int32, (c, c), 1)
    incl = si <= ti
    strict = si < ti
    tri = jnp.where(incl, 1.0, 0.0).astype(BF16)
    hi = logd.astype(BF16)
    rem = logd - hi.astype(F32)
    mid = rem.astype(BF16)
    lo = (rem - mid.astype(F32)).astype(BF16)
    cs = (jnp.dot(tri, hi, preferred_element_type=F32) + jnp.dot(tri, mid, preferred_element_type=F32)
          + jnp.dot(tri, lo, preferred_element_type=F32))
    e_pos = jnp.exp(cs)
    e_prev = jnp.exp(cs - logd)
    e_neg = jnp.exp(-cs)
    rk = r * k * rk_ref[...]
    n_levels = int(math.log2(c))
    assert 2 ** n_levels == c
    heads = range(H_B)
    sls = [slice(h * HD_B, (h + 1) * HD_B) for h in heads]
    kkn = []
    for sl in sls:
        kk_h = kk[:, sl]
        kkn.append(kk_h / jnp.maximum(jnp.sqrt(jnp.sum(kk_h * kk_h, axis=-1, keepdims=True)), 1e-12))
    v_h = [v[:, sl].astype(BF16) for sl in sls]
    lhs = [jnp.concatenate([-kkn[h] * e_prev[:, sls[h]], r[:, sls[h]] * e_pos[:, sls[h]]], axis=0).astype(BF16)
           for h in heads]
    rhs = [jnp.concatenate([kkn[h] * a[:, sls[h]] * e_neg[:, sls[h]], k[:, sls[h]] * e_neg[:, sls[h]]],
                           axis=0).astype(BF16) for h in heads]
    s0 = [state_ref[h] for h in heads]
    mm = [_bdot(lhs[h], rhs[h], _NT) for h in heads]
    ars = [_bdot(lhs[h], s0[h], _NT) for h in heads]
    u = [ars[h][:c] + _bdot(jnp.where(strict, mm[h][:c, c:], 0.0), v_h[h]) for h in heads]
    lp = [jnp.where(strict, mm[h][:c, :c], 0.0) for h in heads]
    for lvl in range(n_levels):
        u = [u[h] + _bdot(lp[h], u[h]) for h in heads]
        if lvl < n_levels - 1:
            lp = [_bdot(lp[h], lp[h]) for h in heads]
    uv = [jnp.concatenate([u[h].astype(BF16), v_h[h]], axis=0) for h in heads]
    t2 = lax.broadcasted_iota(jnp.int32, (c, 2 * c), 0)
    s2 = lax.broadcasted_iota(jnp.int32, (c, 2 * c), 1)
    incl2 = jnp.where(s2 >= c, s2 - c, s2) <= t2
    y = [ars[h][c:] + _bdot(jnp.where(incl2, mm[h][c:], 0.0), uv[h]) for h in heads]
    for h in heads:
        state_ref[h] = (s0[h] + _bdot(uv[h], rhs[h], _TN)) * e_pos[c - 1:c, sls[h]]
    outs = []
    for h in heads:
        sl = sls[h]
        mean = jnp.mean(y[h], axis=-1, keepdims=True)
        var = jnp.mean(jnp.square(y[h] - mean), axis=-1, keepdims=True)
        yn = (y[h] - mean) * lax.rsqrt(var + GN_EPS) * lng_ref[:, sl] + lnb_ref[:, sl]
        bonus = jnp.sum(rk[:, sl], axis=-1, keepdims=True) * v[:, sl]
        outs.append((yn + bonus) * gate[:, sl])
    y_ref[...] = jnp.concatenate(outs, axis=-1).astype(y_ref.dtype)

    @pl.when(ci == n_chunks - 1)
    def _():
        sout_ref[...] = state_ref[...]


def rwkv_mix(p, shift0, s0, w, *, c, t_valid, col0=0):
    b, t, _ = p.shape
    n_chunks = t // c
    fix2 = lambda bb, ci: (0, 0)
    vec = lambda n: pl.BlockSpec((1, n), fix2)
    vmem = 6 * c * B_COLS * 4 + 4 * H_B * HD_B * HD_B * 4 + 40 * c * C_B * 4 + (LORA_W + LORA_A + LORA_G) * C_B * 4
    return pl.pallas_call(
        functools.partial(_rwkv_kernel, c=c, t_valid=t_valid, n_chunks=n_chunks),
        out_shape=(jax.ShapeDtypeStruct((b, t, C_B), BF16),
                   jax.ShapeDtypeStruct((b, H_B, HD_B, HD_B), F32)),
        grid=(b, n_chunks),
        in_specs=[pl.BlockSpec((pl.Element(1), pl.Element(c), pl.Element(B_COLS)),
                               lambda bb, ci: (bb, ci * c, col0)),
                  pl.BlockSpec((None, 1, B_COLS), lambda bb, ci: (bb, 0, 0)),
                  pl.BlockSpec((None, H_B, HD_B, HD_B), lambda bb, ci: (bb, 0, 0, 0)),
                  vec(B_COLS), vec(C_B), pl.BlockSpec((LORA_W, C_B), fix2),
                  vec(C_B), pl.BlockSpec((LORA_A, C_B), fix2), pl.BlockSpec((LORA_G, C_B), fix2),
                  vec(C_B), vec(C_B), vec(C_B), vec(C_B), vec(C_B)],
        out_specs=(pl.BlockSpec((None, c, C_B), lambda bb, ci: (bb, ci, 0)),
                   pl.BlockSpec((None, H_B, HD_B, HD_B), lambda bb, ci: (bb, 0, 0, 0))),
        scratch_shapes=[pltpu.VMEM((1, B_COLS), F32), pltpu.VMEM((H_B, HD_B, HD_B), F32)],
        compiler_params=_cparams(("parallel", "arbitrary"), vmem),
        name="rwkv_mix",
    )(p, shift0, s0, w["mu"], w["w0"], w["w2"], w["a0"], w["a2"], w["g2"], w["k_k"], w["k_a"],
      w["r_k"], w["ln_g"], w["ln_b"])


PAGE_CH = 2 * KVH_C
PAGE_ROWS = PAGE_SIZE * PAGE_CH


def _stream_pages(make_copies):
    step = pl.program_id(0)
    n_steps = pl.num_programs(0)
    slot = step % 2

    @pl.when(step == 0)
    def _():
        for cp in make_copies(step, slot):
            cp.start()

    @pl.when(step + 1 < n_steps)
    def _():
        for cp in make_copies(step + 1, 1 - slot):
            cp.start()

    for cp in make_copies(step, slot):
        cp.wait()
    return slot


BLOCK_ROWS = CMP_BLOCK * PAGE_CH


def _compress_kernel(table_ref, cache_ref, pe_ref, w1_ref, w2_ref, o_ref, buf_ref, sem_ref, *, pp):
    def make_copies(step, slot):
        out = []
        for k in range(pp):
            page = table_ref[step * pp + k]
            for n in range(2):
                out.append(pltpu.make_async_copy(cache_ref.at[page, pl.ds(n * BLOCK_ROWS, BLOCK_ROWS), :],
                                                 buf_ref.at[slot, :, 2 * k + n, :], sem_ref.at[slot]))
        return out

    slot = _stream_pages(make_copies)
    nr = 2 * pp
    for cc in range(2):
        cols = []
        for pos in range(CMP_BLOCK):
            q0 = pos * PAGE_CH + cc * KVH_C
            cols.append(jnp.concatenate([(buf_ref[slot, q0 + h] + pe_ref[q0 + h:q0 + h + 1, :]).astype(BF16)
                                         for h in range(KVH_C)], axis=0))
        flat = jnp.concatenate(cols, axis=1)
        acc = jnp.dot(flat, w1_ref[cc], preferred_element_type=F32)
        res = jnp.dot(jax.nn.gelu(acc).astype(BF16), w2_ref[cc], preferred_element_type=F32)
        for h in range(KVH_C):
            col = (cc * KVH_C + h) * HD_C
            o_ref[:, col:col + HD_C] = res[h * nr:(h + 1) * nr]


def nsa_compress(cache, table, pe, w1, w2, *, pp):
    n_pages = table.shape[0]
    assert n_pages % pp == 0
    nr = 2 * pp
    grid_spec = pltpu.PrefetchScalarGridSpec(
        num_scalar_prefetch=1,
        grid=(n_pages // pp,),
        in_specs=[pl.BlockSpec(memory_space=pl.ANY),
                  pl.BlockSpec((BLOCK_ROWS, HD_C), lambda s, tbl: (0, 0)),
                  pl.BlockSpec((2, CMP_BLOCK * HD_C, CMP_HIDDEN), lambda s, tbl: (0, 0, 0)),
                  pl.BlockSpec((2, CMP_HIDDEN, HD_C), lambda s, tbl: (0, 0, 0))],
        out_specs=pl.BlockSpec((nr, PAGE_CH * HD_C), lambda s, tbl: (s, 0)),
        scratch_shapes=[pltpu.VMEM((2, BLOCK_ROWS, nr, HD_C), F32), pltpu.SemaphoreType.DMA((2,))],
    )
    vmem = 2 * BLOCK_ROWS * nr * HD_C * 4 + 4 * CMP_BLOCK * HD_C * CMP_HIDDEN * 2 + 8 * nr * 512 * 4
    return pl.pallas_call(
        functools.partial(_compress_kernel, pp=pp),
        out_shape=jax.ShapeDtypeStruct((2 * n_pages, PAGE_CH * HD_C), F32),
        grid_spec=grid_spec,
        compiler_params=_cparams(("arbitrary",), vmem),
        name="nsa_compress",
    )(table, cache, pe, w1, w2)


def _nsa_sel_paged_kernel(table_ref, cache_ref, q_ref, sel_ref, bias_ref, knew_ref, vnew_ref, bnew_ref, o_ref,
                          buf_ref, sem_ref, m_ref, l_ref, acc_ref, *, pp, chunks, tq, g):
    hd = HD_C

    def make_copies(step, slot):
        return [pltpu.make_async_copy(cache_ref.at[table_ref[step * pp + k]], buf_ref.at[slot, k], sem_ref.at[slot])
                for k in range(pp)]

    slot = _stream_pages(make_copies)

    def page_rows(ch):
        return jnp.concatenate([buf_ref[slot, k, pl.ds(ch, PAGE_SIZE, stride=PAGE_CH), :] for k in range(pp)], axis=0)

    chunk = pl.program_id(0) % chunks
    nk = pp * PAGE_SIZE
    n_blk = sel_ref.shape[-1]
    scale = hd ** -0.5

    @pl.when(chunk == 0)
    def _():
        m_ref[...] = jnp.full(m_ref.shape, NEG_INF, F32)
        l_ref[...] = jnp.zeros(l_ref.shape, F32)
        acc_ref[...] = jnp.zeros(acc_ref.shape, F32)

    eb = lax.broadcasted_iota(jnp.int32, (n_blk, nk), 0)
    ek = lax.broadcasted_iota(jnp.int32, (n_blk, nk), 1)
    expand = jnp.where(eb == (ek >> SEL_SHIFT), 1.0, 0.0).astype(BF16)
    q = q_ref[...]

    groups = range(KVH_C)
    q8 = [(_stack_heads(q, h * g, g, hd) * scale).astype(BF16) for h in groups]

    def update(lf, vv):
        m_old = [m_ref[h] for h in groups]
        m_new = [jnp.maximum(m_old[h], jnp.max(lf[h], axis=-1, keepdims=True)) for h in groups]
        e = [jnp.exp(lf[h] - m_new[h]) for h in groups]
        alpha = [jnp.exp(m_old[h] - m_new[h]) for h in groups]
        pv = [jnp.dot(e[h].reshape(g * tq, -1).astype(BF16), vv[h], preferred_element_type=F32) for h in groups]
        for h in groups:
            l_ref[h] = alpha[h] * l_ref[h] + jnp.sum(e[h], axis=-1, keepdims=True)
            acc_ref[h] = alpha[h].reshape(g * tq, 1) * acc_ref[h] + pv[h]
            m_ref[h] = m_new[h]

    kk = [page_rows(h).astype(BF16) for h in groups]
    vv = [page_rows(KVH_C + h).astype(BF16) for h in groups]
    selk = [jnp.dot(sel_ref[h].astype(BF16), expand, preferred_element_type=F32) for h in groups]
    lf = [lax.dot_general(q8[h], kk[h], _NT, preferred_element_type=F32).reshape(g, tq, nk)
          + jnp.where((selk[h] > 0.5)[None], bias_ref[h * g:(h + 1) * g], NEG_INF) for h in groups]
    update(lf, vv)

    @pl.when(chunk == chunks - 1)
    def _():
        r = lax.broadcasted_iota(jnp.int32, (tq, tq), 0)
        c = lax.broadcasted_iota(jnp.int32, (tq, tq), 1)
        kn = [knew_ref[:, h * hd:(h + 1) * hd].astype(BF16) for h in groups]
        vn = [vnew_ref[:, h * hd:(h + 1) * hd].astype(BF16) for h in groups]
        lf_new = [lax.dot_general(q8[h], kn[h], _NT, preferred_element_type=F32).reshape(g, tq, tq)
                  + jnp.where((c <= r)[None], bnew_ref[h * g:(h + 1) * g], NEG_INF) for h in groups]
        update(lf_new, vn)
        for h in groups:
            o = acc_ref[h] / jnp.maximum(l_ref[h].reshape(g * tq, 1), 1e-30)
            for j in range(g):
                col = (h * g + j) * hd
                o_ref[:, col:col + hd] = o[j * tq:(j + 1) * tq]


def nsa_sel_paged(q_arr, cache, table, sel, bias, k_new, v_new, bias_new, *, pp, tq):
    b = q_arr.shape[0]
    n_pages = table.shape[0] // b
    chunks = n_pages // pp
    nk = pp * PAGE_SIZE
    n_blk = nk // SEL_BLOCK
    kvw = KVH_C * HD_C
    sel = sel.reshape(b, KVH_C, tq, chunks, n_blk).transpose(0, 3, 1, 2, 4)
    grid_spec = pltpu.PrefetchScalarGridSpec(
        num_scalar_prefetch=1,
        grid=(b * chunks,),
        in_specs=[pl.BlockSpec(memory_space=pl.ANY),
                  pl.BlockSpec((None, tq, D_C), lambda s, tbl: (s // chunks, 0, 0)),
                  pl.BlockSpec((None, None, KVH_C, tq, n_blk), lambda s, tbl: (s // chunks, s % chunks, 0, 0, 0)),
                  pl.BlockSpec((H_C, tq, nk), lambda s, tbl: (0, 0, s % chunks)),
                  pl.BlockSpec((None, tq, kvw), lambda s, tbl: (s // chunks, 0, 0)),
                  pl.BlockSpec((None, tq, kvw), lambda s, tbl: (s // chunks, 0, 0)),
                  pl.BlockSpec((H_C, tq, tq), lambda s, tbl: (0, 0, 0))],
        out_specs=pl.BlockSpec((None, tq, D_C), lambda s, tbl: (s // chunks, 0, 0)),
        scratch_shapes=[pltpu.VMEM((2, pp, PAGE_ROWS, HD_C), F32), pltpu.SemaphoreType.DMA((2,)),
                        pltpu.VMEM((KVH_C, G_C, tq, 1), F32), pltpu.VMEM((KVH_C, G_C, tq, 1), F32),
                        pltpu.VMEM((KVH_C, G_C * tq, HD_C), F32)],
    )
    vmem = 2 * pp * PAGE_SIZE * 2 * kvw * 4 + 2 * H_C * tq * nk * 4 + 10 * G_C * tq * nk * 4 + n_blk * nk * 4
    return pl.pallas_call(
        functools.partial(_nsa_sel_paged_kernel, pp=pp, chunks=chunks, tq=tq, g=G_C),
        out_shape=jax.ShapeDtypeStruct((b, tq, D_C), F32),
        grid_spec=grid_spec,
        compiler_params=_cparams(("arbitrary",), vmem),
        name="nsa_sel_paged",
    )(table, cache, q_arr, sel, bias, k_new, v_new, bias_new)


def _nsa_combine_kernel(gl_ref, gb_ref, oc_ref, os_ref, ow_ref, o_ref):
    gates = jax.nn.sigmoid(gl_ref[...] + gb_ref[...])
    for h in range(H_C):
        sl = slice(h * HD_C, (h + 1) * HD_C)
        o_ref[:, sl] = (gates[:, h:h + 1] * oc_ref[:, sl] + gates[:, H_C + h:H_C + h + 1] * os_ref[:, sl]
                        + gates[:, 2 * H_C + h:2 * H_C + h + 1] * ow_ref[:, sl]).astype(o_ref.dtype)


def nsa_combine(gate_logits, gate_bias, o_c, o_s, o_w):
    m = o_c.shape[0]
    tm = _row_tile(m, 512)
    row = lambda i: (i, 0)
    vmem = 2 * tm * (128 + 3 * D_C) * 4 + 2 * tm * D_C * 2
    return pl.pallas_call(
        _nsa_combine_kernel,
        out_shape=jax.ShapeDtypeStruct((m, D_C), BF16),
        grid=(m // tm,),
        in_specs=[pl.BlockSpec((tm, 128), row), pl.BlockSpec((1, 128), lambda i: (0, 0)),
                  pl.BlockSpec((tm, D_C), row), pl.BlockSpec((tm, D_C), row), pl.BlockSpec((tm, D_C), row)],
        out_specs=pl.BlockSpec((tm, D_C), row),
        compiler_params=_cparams(("parallel",), vmem),
        name="nsa_combine",
    )(gate_logits, gate_bias, o_c, o_s, o_w)


CMP_PAGES_PER_STEP = 32
SEL_PAGES_PER_STEP = 16
Q_PAD = V7X_SUBLANES


def _pad_rows(a, rows, front=0):
    return jnp.pad(a, ((0, 0), (front, rows - a.shape[1] - front), (0, 0)))


def _memo(W, fn, *args):
    key = (fn.__name__,) + args
    if key not in W["tables"]:
        W["tables"][key] = fn(W["rel_bias"], *args)
    return W["tables"][key]


def _past_bias(rel_bias, tq, past_len):
    dist = past_len + jnp.arange(tq, dtype=jnp.int32)[:, None] - jnp.arange(past_len, dtype=jnp.int32)[None, :]
    return _bias_lookup(rel_bias, dist).transpose(2, 0, 1)


def _mixer_ab(x2, b, t, W, l, i, st):
    prompt = st is None
    proj3 = mm_norm(x2, W["mix_norm"][i], W["ab_w_in"][l], W["ab_bias"][l], tn=1536).reshape(b, t, AB_COLS)
    kv_new = proj3[:, :, D_A:A_COLS]
    sinks = W["swa_sinks"][l].reshape(H_A, 1, 1)
    if prompt:
        nw = -(-WIN_A // QBLK)
        sk = (nw + 1) * QBLK
        bias = _memo(W, _toeplitz_bias, QBLK, sk, nw * QBLK)
        o_a = band_attn(proj3, _pad_rows(kv_new, t + nw * QBLK, nw * QBLK),
                        bias, sinks, tq=QBLK, sk=sk, hd=HD_A, n_kvh_step=KVH_A, g=G_A, window=WIN_A,
                        delta=nw * QBLK, kpos_base=-nw * QBLK, kstride=QBLK, out_dtype=BF16)
        buf = kv_new[:, t - min(WIN_A, t):]
        y_b, s_new = rwkv_mix(proj3, jnp.zeros((b, 1, B_COLS), F32), jnp.zeros((b, H_B, HD_B, HD_B), F32),
                              W["rwkv"][l], c=RWKV_CHUNK, t_valid=RWKV_CHUNK, col0=A_COLS)
    else:
        past_len = st["past_len"]
        old = st["swa"][l].reshape(b, -1, 2 * KVH_A * HD_A)
        wb = old.shape[1]
        kv_all = jnp.concatenate([old, kv_new], axis=1)
        buf = kv_all[:, t:]
        sk = -(-(wb + t) // V7X_LANES) * V7X_LANES
        bias = _memo(W, _toeplitz_bias, Q_PAD, sk, wb)
        proj_pad = _pad_rows(proj3, Q_PAD)
        o_a = band_attn(proj_pad, _pad_rows(kv_all, sk),
                        bias, sinks, tq=Q_PAD, sk=sk, hd=HD_A, n_kvh_step=KVH_A, g=G_A, window=WIN_A,
                        delta=wb, kpos_base=past_len - wb, kstride=0, out_dtype=BF16)[:, :t]
        y_b, s_new = rwkv_mix(proj_pad, st["shift"][l][:, None], st["wkv"][l],
                              W["rwkv"][l], c=Q_PAD, t_valid=t, col0=A_COLS)
        y_b = y_b[:, :t]
    mix_in = jnp.concatenate([o_a, y_b], axis=-1).reshape(b * t, D_MODEL)
    x2 = mm_res(mix_in, W["ab_w_out"][l], x2)
    buf = buf.reshape(b, -1, 2, KVH_A, HD_A)
    return x2, buf, proj3[:, -1, A_COLS:], s_new


def _mixer_c(x2, b, t, W, l, i, st):
    prompt = st is None
    kvw = KVH_C * HD_C
    proj3 = mm_norm(x2, W["mix_norm"][i], W["c_w_in"][l], W["c_zero_bias"], tn=1024).reshape(b, t, C_COLS_PAD)
    kv_cmp_new = proj3[:, :, D_C:D_C + 2 * kvw]
    kv_sel_new = proj3[:, :, D_C + 2 * kvw:D_C + 4 * kvw]
    kv_win_new = proj3[:, :, D_C + 4 * kvw:D_C + 6 * kvw]
    gate_logits = proj3[:, :, D_C + C_KV_COLS:D_C + C_KV_COLS + V7X_LANES].reshape(b * t, V7X_LANES)
    pe, w1, w2 = W["cmp_pe"][l], W["cmp_w1"][l], W["cmp_w2"][l]
    if prompt:
        n_pages = b * t // PAGE_SIZE
        kv_c = nsa_compress(kv_cmp_new.reshape(n_pages, PAGE_ROWS, HD_C), jnp.arange(n_pages, dtype=jnp.int32),
                            pe, w1, w2, pp=min(CMP_PAGES_PER_STEP, n_pages)).reshape(b, t // CMP_BLOCK, 2 * kvw)
        o_c, sel = nsa_cmp(proj3, kv_c, tq=QBLK, q0=0)
        o_s = nsa_sel_prompt(proj3, kv_sel_new[:, :, :kvw], kv_sel_new[:, :, kvw:], sel,
                             _memo(W, sel_bias_tiles, QBLK, t), tq=QBLK)
        nw = -(-WIN_C // QBLK)
        sk = (nw + 1) * QBLK
        bias = _memo(W, _toeplitz_bias, QBLK, sk, nw * QBLK)
        o_w = band_attn(proj3, _pad_rows(kv_win_new, t + nw * QBLK, nw * QBLK), bias, None,
                        tq=QBLK, sk=sk, hd=HD_C, n_kvh_step=1, g=G_C, window=WIN_C,
                        delta=nw * QBLK, kpos_base=-nw * QBLK, kstride=QBLK, out_dtype=F32)
        win_buf = kv_win_new[:, t - min(WIN_C, t):]
    else:
        past_len = st["past_len"]
        n_phys = st["cmp"].shape[1]
        table = st["page_table"].reshape(-1) + l * n_phys
        kv_c = nsa_compress(st["cmp"].reshape(-1, PAGE_ROWS, HD_C), table, pe, w1, w2,
                            pp=CMP_PAGES_PER_STEP).reshape(b, past_len // CMP_BLOCK, 2 * kvw)
        q8 = _pad_rows(proj3, Q_PAD)
        o_c, sel = nsa_cmp(q8, kv_c, tq=Q_PAD, q0=past_len)
        bias_past = _memo(W, _past_bias, Q_PAD, past_len)
        bias_new = _memo(W, _toeplitz_bias, Q_PAD, Q_PAD, 0)
        sel_new = _pad_rows(kv_sel_new, Q_PAD)
        o_s = nsa_sel_paged(q8, st["sel"].reshape(-1, PAGE_ROWS, HD_C), table, sel, bias_past,
                            sel_new[:, :, :kvw], sel_new[:, :, kvw:], bias_new, pp=SEL_PAGES_PER_STEP, tq=Q_PAD)
        old = st["win"][l].reshape(b, -1, 2 * kvw)
        wb = old.shape[1]
        kv_all = jnp.concatenate([old, kv_win_new], axis=1)
        win_buf = kv_all[:, t:]
        sk = -(-(wb + t) // V7X_LANES) * V7X_LANES
        bias = _memo(W, _toeplitz_bias, Q_PAD, sk, wb)
        o_w = band_attn(q8, _pad_rows(kv_all, sk), bias, None, tq=Q_PAD, sk=sk, hd=HD_C,
                        n_kvh_step=1, g=G_C, window=WIN_C, delta=wb, kpos_base=past_len - wb, kstride=0,
                        out_dtype=F32)
        o_c, o_s, o_w = o_c[:, :t], o_s[:, :t], o_w[:, :t]
    comb = nsa_combine(gate_logits, W["c_gate_b"][l], o_c.reshape(b * t, D_C), o_s.reshape(b * t, D_C),
                       o_w.reshape(b * t, D_C))
    x2 = mm_res(comb, W["c_w_out"][l], x2)
    shape5 = lambda a: a.reshape(b, -1, 2, KVH_C, HD_C)
    return x2, shape5(kv_cmp_new), shape5(kv_sel_new), shape5(win_buf)


def _trunk(x, p, W, st):
    b, t, _ = x.shape
    x2 = x.reshape(b * t, D_MODEL)
    swa_l, shift_l, wkv_l, cmp_l, sel_l, win_l = [], [], [], [], [], []
    for i in range(DEPTH):
        l = i // 2
        if i % 2 == 0:
            x2, buf, shift_new, s_new = _mixer_ab(x2, b, t, W, l, i, st)
            swa_l.append(buf)
            shift_l.append(shift_new)
            wkv_l.append(s_new)
        else:
            x2, cmp_new, sel_new, win_buf = _mixer_c(x2, b, t, W, l, i, st)
            cmp_l.append(cmp_new)
            sel_l.append(sel_new)
            win_l.append(win_buf)
        act = ffn_up(x2, W["ffn_norm"][i], W["ffn_w_gate"][i], W["ffn_w_up"][i])
        x2 = mm_res(act, W["ffn_w_down"][i], x2, tk=D_FF // 2)
        x2 = ple(x2, p[i].reshape(b * t, PLE_DIM), W["ple_gate_norm"][i], W["ple_w_gate"][i], W["ple_w_proj"][i],
                 W["ple_post_norm"][i], W["final_norm"], final=(i == DEPTH - 1))
    y = x2.reshape(b, t, D_MODEL)
    return (y, jnp.stack(swa_l), jnp.stack(shift_l), jnp.stack(wkv_l), jnp.stack(cmp_l), jnp.stack(sel_l),
            jnp.stack(win_l))


def kernel(x_prompt, x_sample, state_swa_kv, state_rwkv_shift, state_rwkv_wkv, cache_nsa_cmp_kv, cache_nsa_sel_kv, state_nsa_win_kv, page_table, p_prompt, p_sample, rel_bias, mix_norm, ab_w_in, ab_b_qkv, swa_sinks, rwkv_mu, rwkv_w0, rwkv_w2, rwkv_a0, rwkv_a2, rwkv_g2, rwkv_k_k, rwkv_k_a, rwkv_r_k, rwkv_ln_g, rwkv_ln_b, ab_w_out, c_w_in, c_gate_b, nsa_cmp_pos, nsa_cmp_w1, nsa_cmp_w2, c_w_out, ffn_norm, ffn_w_gate, ffn_w_up, ffn_w_down, ple_w_proj, ple_gate_norm, ple_w_gate, ple_post_norm, final_norm):
    n_ab, n_c = ab_w_in.shape[0], c_w_in.shape[0]
    bf = lambda a: a.astype(BF16)
    row = lambda a: a.reshape(a.shape[0], 1, -1).astype(F32)
    pe = jnp.broadcast_to(nsa_cmp_pos.transpose(0, 2, 1, 3)[:, :, :, None, :],
                          (n_c, CMP_BLOCK, 2, KVH_C, HD_C)).reshape(n_c, BLOCK_ROWS, HD_C)
    W = dict(
        tables={}, rel_bias=rel_bias,mix_norm=row(mix_norm), ab_w_in=bf(ab_w_in),
        ab_bias=jnp.pad(ab_b_qkv, ((0, 0), (0, AB_COLS - A_COLS))).reshape(n_ab, 1, AB_COLS),
        swa_sinks=swa_sinks, ab_w_out=bf(ab_w_out),
        rwkv=[dict(mu=rwkv_mu[l][None], w0=rwkv_w0[l][None], w2=bf(rwkv_w2[l]), a0=rwkv_a0[l][None], a2=bf(rwkv_a2[l]),
                   g2=bf(rwkv_g2[l]), k_k=rwkv_k_k[l][None], k_a=rwkv_k_a[l][None], r_k=rwkv_r_k[l].reshape(1, C_B),
                   ln_g=rwkv_ln_g[l][None], ln_b=rwkv_ln_b[l][None]) for l in range(n_ab)],
        c_w_in=bf(jnp.pad(c_w_in, ((0, 0), (0, 0), (0, C_COLS_PAD - C_COLS)))),
        c_zero_bias=jnp.zeros((1, C_COLS_PAD), F32),
        c_gate_b=jnp.pad(c_gate_b, ((0, 0), (0, V7X_LANES - 3 * H_C))).reshape(n_c, 1, V7X_LANES),
        cmp_pe=pe, cmp_w1=bf(nsa_cmp_w1), cmp_w2=bf(nsa_cmp_w2), c_w_out=bf(c_w_out),
        ffn_norm=row(ffn_norm), ffn_w_gate=bf(ffn_w_gate), ffn_w_up=bf(ffn_w_up), ffn_w_down=bf(ffn_w_down),
        ple_w_proj=bf(ple_w_proj), ple_gate_norm=row(ple_gate_norm), ple_w_gate=bf(ple_w_gate),
        ple_post_norm=row(ple_post_norm), final_norm=final_norm.reshape(1, D_MODEL),
    )
    st = dict(swa=state_swa_kv, shift=state_rwkv_shift, wkv=state_rwkv_wkv, cmp=cache_nsa_cmp_kv,
              sel=cache_nsa_sel_kv, win=state_nsa_win_kv, page_table=page_table,
              past_len=page_table.shape[1] * PAGE_SIZE)
    y_p, swa_p, shift_p, wkv_p, cmp_p, sel_p, win_p = _trunk(x_prompt, p_prompt, W, None)
    y_s, swa_s, shift_s, wkv_s, cmp_s, sel_s, win_s = _trunk(x_sample, p_sample, W, st)
    return (y_p, y_s, swa_p, swa_s, shift_p, shift_s, wkv_p, wkv_s, cmp_p, cmp_s, sel_p, sel_s, win_p, win_s)
```

```python
import functools
import math

import jax
import jax.numpy as jnp
import numpy as np
from jax import lax
from jax.experimental import pallas as pl
from jax.experimental.pallas import tpu as pltpu

F32 = jnp.float32
BF16 = jnp.bfloat16

D_MODEL = 2048
DEPTH = 4
PAGE_SIZE = 128
PLE_DIM = 256
N_BUCKETS = 32
REL_MAX_DIST = 1024
RMS_EPS = 1e-6
D_FF = 5632
QBLK = 128
HD_A = 64
H_A = 16
KVH_A = 2
G_A = 8
D_A = 1024
WIN_A = 128
A_COLS = D_A + 2 * KVH_A * HD_A
HD_B = 64
C_B = 1024
H_B = 16
LORA_W = 64
LORA_A = 64
LORA_G = 128
B_COLS = 3 * C_B + LORA_W + LORA_A + LORA_G
AB_COLS = A_COLS + B_COLS
GN_EPS = 64e-5
HD_C = 128
H_C = 16
KVH_C = 2
G_C = 8
D_C = 2048
CMP_BLOCK = 64
SEL_BLOCK = 64
SEL_SHIFT = 6
N_TOP = 15
WIN_C = 512
CMP_HIDDEN = 128
C_KV_COLS = 6 * KVH_C * HD_C
C_COLS = D_C + C_KV_COLS + 3 * H_C
NEG_INF = -1e30
FORCE_SCORE = 1e4

V7X_LANES = 128
V7X_SUBLANES = 8
V7X_VMEM_BYTES = 64 * 1024 * 1024
VMEM_LIMIT_CAP = V7X_VMEM_BYTES - 8 * 1024 * 1024

C_COLS_PAD = 4096
RWKV_CHUNK = 64


def _cparams(sem, vmem_bytes):
    limit = int(min(max(2 * vmem_bytes, 32 * 1024 * 1024), VMEM_LIMIT_CAP))
    return pltpu.CompilerParams(dimension_semantics=sem, vmem_limit_bytes=limit)


def _row_tile(m, cap):
    t = min(m, cap)
    assert m % t == 0, (m, t)
    return t


def _rms(x, g):
    return x * lax.rsqrt(jnp.mean(x * x, axis=-1, keepdims=True) + RMS_EPS) * g


def t5_bucket(dist):
    n = jnp.maximum(dist, 0)
    max_exact = N_BUCKETS // 2
    nf = jnp.maximum(n, max_exact).astype(F32)
    large = max_exact + (jnp.log(nf / max_exact) / math.log(REL_MAX_DIST / max_exact) * (N_BUCKETS - max_exact)).astype(jnp.int32)
    return jnp.where(n < max_exact, n, jnp.minimum(large, N_BUCKETS - 1))


def _mm_norm_kernel(x_ref, g_ref, w_ref, b_ref, o_ref, h_ref):
    @pl.when(pl.program_id(1) == 0)
    def _():
        h_ref[...] = _rms(x_ref[...], g_ref[...]).astype(BF16)

    o_ref[...] = jnp.dot(h_ref[...], w_ref[...], preferred_element_type=F32) + b_ref[...]


def mm_norm(x, g, w, b, *, tn=512):
    m, k = x.shape
    n = w.shape[1]
    tm = _row_tile(m, 1024)
    vmem = 2 * tm * k * 4 + tm * k * 2 + 2 * k * tn * 2 + 2 * tm * tn * 4
    return pl.pallas_call(
        _mm_norm_kernel,
        out_shape=jax.ShapeDtypeStruct((m, n), F32),
        grid=(m // tm, n // tn),
        in_specs=[pl.BlockSpec((tm, k), lambda i, j: (i, 0)),
                  pl.BlockSpec((1, k), lambda i, j: (0, 0)),
                  pl.BlockSpec((k, tn), lambda i, j: (0, j)),
                  pl.BlockSpec((1, tn), lambda i, j: (0, j))],
        out_specs=pl.BlockSpec((tm, tn), lambda i, j: (i, j)),
        scratch_shapes=[pltpu.VMEM((tm, k), BF16)],
        compiler_params=_cparams(("parallel", "arbitrary"), vmem),
        name="mm_norm",
    )(x, g, w, b)


def _mm_res_kernel(a_ref, w_ref, r_ref, o_ref, acc_ref, *, nk):
    kk = pl.program_id(2)

    @pl.when(kk == 0)
    def _():
        acc_ref[...] = jnp.zeros_like(acc_ref)

    acc_ref[...] += jnp.dot(a_ref[...], w_ref[...], preferred_element_type=F32)

    @pl.when(kk == nk - 1)
    def _():
        o_ref[...] = r_ref[...] + acc_ref[...]


def mm_res(a, w, r, *, tn=1024, tk=None):
    m, k = a.shape
    n = w.shape[1]
    tm = _row_tile(m, 1024)
    tk = k if tk is None else tk
    nk = k // tk
    vmem = 2 * tm * tk * 2 + 2 * tk * tn * 2 + 5 * tm * tn * 4
    return pl.pallas_call(
        functools.partial(_mm_res_kernel, nk=nk),
        out_shape=jax.ShapeDtypeStruct((m, n), F32),
        grid=(m // tm, n // tn, nk),
        in_specs=[pl.BlockSpec((tm, tk), lambda i, j, q: (i, q)),
                  pl.BlockSpec((tk, tn), lambda i, j, q: (q, j)),
                  pl.BlockSpec((tm, tn), lambda i, j, q: (i, j))],
        out_specs=pl.BlockSpec((tm, tn), lambda i, j, q: (i, j)),
        scratch_shapes=[pltpu.VMEM((tm, tn), F32)],
        compiler_params=_cparams(("parallel", "parallel", "arbitrary"), vmem),
        name="mm_res",
    )(a, w, r)


def _ffn_up_kernel(x_ref, g_ref, wg_ref, wu_ref, o_ref, h_ref):
    @pl.when(pl.program_id(1) == 0)
    def _():
        h_ref[...] = _rms(x_ref[...], g_ref[...]).astype(BF16)

    h = h_ref[...]
    gate = jnp.dot(h, wg_ref[...], preferred_element_type=F32)
    up = jnp.dot(h, wu_ref[...], preferred_element_type=F32)
    o_ref[...] = (jax.nn.silu(gate) * up).astype(BF16)


def ffn_up(x, g, wg, wu, *, tn=512):
    m, k = x.shape
    n = wg.shape[1]
    tm = _row_tile(m, 1024)
    vmem = 2 * tm * k * 4 + tm * k * 2 + 4 * k * tn * 2 + 2 * tm * tn * 2 + 3 * tm * tn * 4
    return pl.pallas_call(
        _ffn_up_kernel,
        out_shape=jax.ShapeDtypeStruct((m, n), BF16),
        grid=(m // tm, n // tn),
        in_specs=[pl.BlockSpec((tm, k), lambda i, j: (i, 0)),
                  pl.BlockSpec((1, k), lambda i, j: (0, 0)),
                  pl.BlockSpec((k, tn), lambda i, j: (0, j)),
                  pl.BlockSpec((k, tn), lambda i, j: (0, j))],
        out_specs=pl.BlockSpec((tm, tn), lambda i, j: (i, j)),
        scratch_shapes=[pltpu.VMEM((tm, k), BF16)],
        compiler_params=_cparams(("parallel", "arbitrary"), vmem),
        name="ffn_up",
    )(x, g, wg, wu)


def _ple_kernel(x_ref, p_ref, gn_ref, wg_ref, wp_ref, pn_ref, fn_ref, o_ref, *, final):
    x = x_ref[...]
    h = _rms(x, gn_ref[...]).astype(BF16)
    gate = jax.nn.sigmoid(jnp.dot(h, wg_ref[...], preferred_element_type=F32))
    e = jnp.dot(p_ref[...].astype(BF16), wp_ref[...], preferred_element_type=F32)
    x = x + _rms(gate * e, pn_ref[...])
    if final:
        x = _rms(x, fn_ref[...])
    o_ref[...] = x


def ple(x, p, gn, wg, wp, pn, fn, *, final):
    m, d = x.shape
    tm = _row_tile(m, 512)
    vmem = 4 * tm * d * 4 + 2 * d * d * 2 + 2 * PLE_DIM * d * 2 + 4 * tm * d * 4
    row = lambda i: (i, 0)
    fix = lambda i: (0, 0)
    return pl.pallas_call(
        functools.partial(_ple_kernel, final=final),
        out_shape=jax.ShapeDtypeStruct((m, d), F32),
        grid=(m // tm,),
        in_specs=[pl.BlockSpec((tm, d), row), pl.BlockSpec((tm, PLE_DIM), row),
                  pl.BlockSpec((1, d), fix), pl.BlockSpec((d, d), fix),
                  pl.BlockSpec((PLE_DIM, d), fix), pl.BlockSpec((1, d), fix),
                  pl.BlockSpec((1, d), fix)],
        out_specs=pl.BlockSpec((tm, d), row),
        compiler_params=_cparams(("parallel",), vmem),
        name="ple",
    )(x, p, gn, wg, wp, pn, fn)


def _masked_softmax(logits, mask, sink=None):
    lf = jnp.where(mask, logits, NEG_INF)
    m = jnp.max(lf, axis=-1, keepdims=True)
    if sink is not None:
        m = jnp.maximum(m, sink)
    e = jnp.where(mask, jnp.exp(lf - m), 0.0)
    den = jnp.sum(e, axis=-1, keepdims=True)
    if sink is not None:
        den = den + jnp.exp(sink - m)
    return e / jnp.maximum(den, 1e-30)


def _stack_heads(q, h0, g, hd):
    return jnp.concatenate([q[:, (h0 + j) * hd:(h0 + j + 1) * hd] for j in range(g)], axis=0)


def _band_attn_kernel(q_ref, k_ref, v_ref, bias_ref, sink_ref, o_ref, *,
                      tq, sk, hd, n_kvh, g, window, delta, kpos_base, kstride, has_sink):
    tl = V7X_LANES
    n_tiles = sk // tl
    ks = pl.multiple_of(pl.program_id(2) * kstride, V7X_SUBLANES)
    kslab = k_ref[pl.ds(ks, sk), :].astype(BF16)
    vslab = v_ref[pl.ds(ks, sk), :].astype(BF16)
    r = lax.broadcasted_iota(jnp.int32, (tq, tl), 0)
    c = lax.broadcasted_iota(jnp.int32, (tq, tl), 1)
    masks = []
    for t in range(n_tiles):
        dist = delta + r - (c + t * tl)
        masks.append(((dist >= 0) & (dist <= window) & (kpos_base + ks + t * tl + c >= 0))[None])
    q = q_ref[...] * (hd ** -0.5)
    chains = range(n_kvh)
    q8 = [_stack_heads(q, h * g, g, hd).astype(BF16) for h in chains]
    lf = [[lax.dot_general(q8[h], kslab[t * tl:(t + 1) * tl, h * hd:(h + 1) * hd], _NT,
                           preferred_element_type=F32).reshape(g, tq, tl)
           + jnp.where(masks[t], bias_ref[h * g:(h + 1) * g, :, t * tl:(t + 1) * tl], NEG_INF)
           for t in range(n_tiles)] for h in chains]
    m = [jnp.max(functools.reduce(jnp.maximum, lf[h]), axis=-1, keepdims=True) for h in chains]
    if has_sink:
        m = [jnp.maximum(m[h], sink_ref[h * g:(h + 1) * g]) for h in chains]
    mfull = [jnp.broadcast_to(m[h], (g, tq, tl)) for h in chains]
    e = [[jnp.exp(lf[h][t] - mfull[h]) for t in range(n_tiles)] for h in chains]
    acc = [functools.reduce(jnp.add, [jnp.dot(e[h][t].reshape(g * tq, tl).astype(BF16),
                                              vslab[t * tl:(t + 1) * tl, h * hd:(h + 1) * hd],
                                              preferred_element_type=F32) for t in range(n_tiles)])
           for h in chains]
    den = [jnp.sum(functools.reduce(jnp.add, e[h]), axis=-1, keepdims=True) for h in chains]
    if has_sink:
        den = [den[h] + jnp.exp(sink_ref[h * g:(h + 1) * g] - m[h]) for h in chains]
    for h in chains:
        o = acc[h] / jnp.maximum(den[h].reshape(g * tq, 1), 1e-30)
        for j in range(g):
            o_ref[:, (h * g + j) * hd:(h * g + j + 1) * hd] = o[j * tq:(j + 1) * tq].astype(o_ref.dtype)


def band_attn(q_arr, kv_arr, bias, sink, *, tq, sk, hd, n_kvh_step, g, window, delta,
              kpos_base, kstride, out_dtype):
    b, t = q_arr.shape[:2]
    tk = kv_arr.shape[1]
    n_kv_blocks = kv_arr.shape[2] // (2 * n_kvh_step * hd)
    qw = n_kvh_step * g * hd
    has_sink = sink is not None
    if not has_sink:
        sink = jnp.zeros((n_kv_blocks * n_kvh_step * g, 1, 1), F32)
    vmem = 2 * tq * qw * 4 * 2 + 4 * tk * n_kvh_step * hd * 4 + 2 * n_kvh_step * g * tq * sk * 4 + 6 * g * tq * sk * 4
    kern = functools.partial(_band_attn_kernel, tq=tq, sk=sk, hd=hd, n_kvh=n_kvh_step, g=g, window=window,
                             delta=delta, kpos_base=kpos_base, kstride=kstride, has_sink=has_sink)
    return pl.pallas_call(
        kern,
        out_shape=jax.ShapeDtypeStruct((b, t, n_kv_blocks * qw), out_dtype),
        grid=(b, n_kv_blocks, t // tq),
        in_specs=[pl.BlockSpec((None, tq, qw), lambda bb, kv, i: (bb, i, kv)),
                  pl.BlockSpec((None, tk, n_kvh_step * hd), lambda bb, kv, i: (bb, 0, kv)),
                  pl.BlockSpec((None, tk, n_kvh_step * hd), lambda bb, kv, i: (bb, 0, n_kv_blocks + kv)),
                  pl.BlockSpec((n_kvh_step * g, tq, sk), lambda bb, kv, i: (kv, 0, 0)),
                  pl.BlockSpec((n_kvh_step * g, 1, 1), lambda bb, kv, i: (kv, 0, 0))],
        out_specs=pl.BlockSpec((None, tq, qw), lambda bb, kv, i: (bb, i, kv)),
        compiler_params=_cparams(("parallel", "parallel", "arbitrary"), vmem),
        name="band_attn",
    )(q_arr, kv_arr, kv_arr, bias, sink)


def _bias_lookup(rel_bias, dist):
    onehot = (t5_bucket(dist)[..., None] == jnp.arange(N_BUCKETS, dtype=jnp.int32)).astype(F32)
    return jnp.einsum("...k,kh->...h", onehot, rel_bias.astype(F32), precision=lax.Precision.HIGHEST)


def _toeplitz_bias(rel_bias, tq, sk, delta):
    dist = delta + jnp.arange(tq, dtype=jnp.int32)[:, None] - jnp.arange(sk, dtype=jnp.int32)[None, :]
    return _bias_lookup(rel_bias, dist).transpose(2, 0, 1)


def _nsa_cmp_kernel(q_ref, kv_ref, oc_ref, sel_ref, *, tq, n_cmp, g, q0):
    hd = HD_C
    qp = q0 + pl.program_id(1) * tq + lax.broadcasted_iota(jnp.int32, (tq, n_cmp), 0)
    blk = lax.broadcasted_iota(jnp.int32, (tq, n_cmp), 1)
    cmask = ((blk + 1) * CMP_BLOCK <= qp + 1)[None]
    cur = qp >> SEL_SHIFT
    groups = range(KVH_C)
    q = q_ref[...]
    scores = []
    for h in groups:
        q8 = _stack_heads(q, h * g, g, hd)
        kc = kv_ref[:, h * hd:(h + 1) * hd]
        vc = kv_ref[:, (KVH_C + h) * hd:(KVH_C + h + 1) * hd].astype(BF16)
        q_hi = q8.astype(BF16)
        q_lo = (q8 - q_hi.astype(F32)).astype(BF16)
        k_hi = kc.astype(BF16)
        k_lo = (kc - k_hi.astype(F32)).astype(BF16)
        cl = (lax.dot_general(q_hi, k_hi, _NT, preferred_element_type=F32)
              + lax.dot_general(q_hi, k_lo, _NT, preferred_element_type=F32)
              + lax.dot_general(q_lo, k_hi, _NT, preferred_element_type=F32)) * (hd ** -0.5)
        p_c = _masked_softmax(cl.reshape(g, tq, n_cmp), cmask)
        o = jnp.dot(p_c.reshape(g * tq, n_cmp).astype(BF16), vc, preferred_element_type=F32)
        for j in range(g):
            oc_ref[:, (h * g + j) * hd:(h * g + j + 1) * hd] = o[j * tq:(j + 1) * tq]
        imp = jnp.sum(p_c, axis=0) + jnp.where(blk == 0, FORCE_SCORE, 0.0)
        scores.append(jnp.where(blk < cur, imp, -1.0))
    transposed = tq % V7X_LANES == 0
    if transposed:
        work = [scores[h].T for h in groups]
        idx = lax.broadcasted_iota(jnp.int32, (n_cmp, tq), 0)
    else:
        work = scores
        idx = blk
    ranks = [jnp.zeros(work[0].shape, F32) for _ in groups]
    for k in range(n_cmp):
        for h in groups:
            other = work[h][k:k + 1, :] if transposed else work[h][:, k:k + 1]
            ranks[h] = ranks[h] + jnp.where(idx > k, jnp.where(other >= work[h], 1.0, 0.0),
                                            jnp.where(other > work[h], 1.0, 0.0))
    if transposed:
        ranks = [ranks[h].T for h in groups]
    for h in groups:
        sel_ref[h] = jnp.where(((ranks[h] < N_TOP) & (scores[h] >= 0.0)) | (blk == cur), 1.0, 0.0)


def nsa_cmp(q_arr, kv_c, *, tq, q0):
    b, t = q_arr.shape[:2]
    n_cmp = kv_c.shape[1]
    assert n_cmp >= N_TOP
    kvw = 2 * KVH_C * HD_C
    vmem = 4 * tq * D_C * 4 + 2 * n_cmp * kvw * 4 + 16 * G_C * tq * n_cmp * 4
    return pl.pallas_call(
        functools.partial(_nsa_cmp_kernel, tq=tq, n_cmp=n_cmp, g=G_C, q0=q0),
        out_shape=(jax.ShapeDtypeStruct((b, t, D_C), F32),
                   jax.ShapeDtypeStruct((b, KVH_C, t, n_cmp), F32)),
        grid=(b, t // tq),
        in_specs=[pl.BlockSpec((None, tq, D_C), lambda bb, i: (bb, i, 0)),
                  pl.BlockSpec((None, n_cmp, kvw), lambda bb, i: (bb, 0, 0))],
        out_specs=(pl.BlockSpec((None, tq, D_C), lambda bb, i: (bb, i, 0)),
                   pl.BlockSpec((None, KVH_C, tq, n_cmp), lambda bb, i: (bb, 0, i, 0))),
        compiler_params=_cparams(("parallel", "arbitrary"), vmem),
        name="nsa_cmp",
    )(q_arr, kv_c)


SEL_TILES_PER_TRIP = 4


def _nsa_sel_kernel(q_ref, k_ref, v_ref, sel_ref, bias_ref, o_ref, selk_ref, mx_ref, le_ref, acc_ref, lf_ref, *,
                    tq, t, g, n_far):
    hd = HD_C
    i = pl.program_id(2)
    n_blk = t // SEL_BLOCK
    eb = lax.broadcasted_iota(jnp.int32, (n_blk, t), 0)
    ek = lax.broadcasted_iota(jnp.int32, (n_blk, t), 1)
    expand = jnp.where((ek >> SEL_SHIFT) == eb, 1.0, 0.0).astype(BF16)
    selk_ref[...] = jnp.dot(sel_ref[...].astype(BF16), expand, preferred_element_type=F32)
    q8 = (_stack_heads(q_ref[...], 0, g, hd) * (hd ** -0.5)).astype(BF16)
    r = lax.broadcasted_iota(jnp.int32, (tq, tq), 0)
    c = lax.broadcasted_iota(jnp.int32, (tq, tq), 1)

    last = t // tq - 1

    def tile_start(j):
        return pl.multiple_of(jnp.minimum(j, last) * tq, tq)

    def logits(j):
        ks = tile_start(j)
        kj = k_ref[pl.ds(ks, tq), :].astype(BF16)
        s = lax.dot_general(q8, kj, _NT, preferred_element_type=F32).reshape(g, tq, tq)
        mask = ((selk_ref[:, pl.ds(ks, tq)] > 0.5) & ((j - i) * tq + c <= r))[None]
        return s + jnp.where(mask, bias_ref[jnp.clip(i - j, 0, n_far)], NEG_INF)

    per = SEL_TILES_PER_TRIP
    n_trips = (i + per) // per
    mx_ref[...] = jnp.full(mx_ref.shape, NEG_INF, F32)

    def sweep_max(jj, carry):
        tiles = [logits(per * jj + u) for u in range(per)]
        for u in range(per):
            lf_ref[jnp.minimum(per * jj + u, last)] = tiles[u]
        mx_ref[...] = jnp.maximum(mx_ref[...], functools.reduce(jnp.maximum, tiles))
        return carry

    lax.fori_loop(0, n_trips, sweep_max, 0)
    mx_ref[...] = jnp.broadcast_to(jnp.max(mx_ref[...], axis=-1, keepdims=True), mx_ref.shape)
    le_ref[...] = jnp.zeros(le_ref.shape, F32)
    acc_ref[...] = jnp.zeros(acc_ref.shape, F32)

    def sweep_acc(jj, carry):
        mx = mx_ref[...]
        es = [jnp.exp(lf_ref[jnp.minimum(per * jj + u, last)] - mx) for u in range(per)]
        le_ref[...] += functools.reduce(jnp.add, es)
        pvs = [jnp.dot(es[u].reshape(g * tq, tq).astype(BF16),
                       v_ref[pl.ds(tile_start(per * jj + u), tq), :].astype(BF16), preferred_element_type=F32)
               for u in range(per)]
        acc_ref[...] += functools.reduce(jnp.add, pvs)
        return carry

    lax.fori_loop(0, n_trips, sweep_acc, 0)
    den = jnp.sum(le_ref[...], axis=-1, keepdims=True).reshape(g * tq, 1)
    o = acc_ref[...] / jnp.maximum(den, 1e-30)
    for j in range(g):
        o_ref[:, j * hd:(j + 1) * hd] = o[j * tq:(j + 1) * tq]


def _bias_saturation_offset(tq, t):
    d = np.arange(0, t + tq, dtype=np.float64)
    nf = np.maximum(d, N_BUCKETS // 2)
    large = N_BUCKETS // 2 + np.floor(np.log(nf / (N_BUCKETS // 2)) / math.log(REL_MAX_DIST / (N_BUCKETS // 2))
                                      * (N_BUCKETS - N_BUCKETS // 2) - 1e-3)
    saturated = np.where(d < N_BUCKETS // 2, 0, large) >= N_BUCKETS - 1
    if not saturated.any():
        return t // tq
    first_sat = int(np.argmax(saturated))
    return min(t // tq, -(-(first_sat + tq) // tq))


def sel_bias_tiles(rel_bias, tq, t):
    n_far = _bias_saturation_offset(tq, t)
    dist = (jnp.arange(n_far + 1, dtype=jnp.int32)[:, None, None] * tq
            + jnp.arange(tq, dtype=jnp.int32)[None, :, None] - jnp.arange(tq, dtype=jnp.int32)[None, None, :])
    tiles = _bias_lookup(rel_bias, dist)
    return tiles.reshape(n_far + 1, tq, tq, KVH_C, G_C).transpose(3, 0, 4, 1, 2)


def nsa_sel_prompt(q_arr, k_arr, v_arr, sel, tiles, *, tq, k_blk0=0, v_blk0=0):
    b, t = q_arr.shape[:2]
    n_blk = t // SEL_BLOCK
    n_far = tiles.shape[1] - 1
    n_tiles = t // tq
    assert n_tiles % SEL_TILES_PER_TRIP == 0
    qw = G_C * HD_C
    lf_bytes = n_tiles * G_C * tq * tq * 4
    vmem = (4 * tq * qw * 4 + 4 * t * HD_C * 4 + 2 * (n_far + 1) * G_C * tq * tq * 4 + tq * t * 4
            + 8 * G_C * tq * tq * 4 + n_blk * t * 4 + lf_bytes)
    return pl.pallas_call(
        functools.partial(_nsa_sel_kernel, tq=tq, t=t, g=G_C, n_far=n_far),
        out_shape=jax.ShapeDtypeStruct((b, t, D_C), F32),
        grid=(b, KVH_C, t // tq),
        in_specs=[pl.BlockSpec((None, tq, qw), lambda bb, kv, i: (bb, i, kv)),
                  pl.BlockSpec((None, t, HD_C), lambda bb, kv, i: (bb, 0, k_blk0 + kv)),
                  pl.BlockSpec((None, t, HD_C), lambda bb, kv, i: (bb, 0, v_blk0 + kv)),
                  pl.BlockSpec((None, None, tq, n_blk), lambda bb, kv, i: (bb, kv, i, 0)),
                  pl.BlockSpec((None, n_far + 1, G_C, tq, tq), lambda bb, kv, i: (kv, 0, 0, 0, 0))],
        out_specs=pl.BlockSpec((None, tq, qw), lambda bb, kv, i: (bb, i, kv)),
        scratch_shapes=[pltpu.VMEM((tq, t), F32), pltpu.VMEM((G_C, tq, tq), F32),
                        pltpu.VMEM((G_C, tq, tq), F32), pltpu.VMEM((G_C * tq, HD_C), F32),
                        pltpu.VMEM((n_tiles, G_C, tq, tq), F32)],
        compiler_params=_cparams(("parallel", "parallel", "arbitrary"), vmem),
        name="nsa_sel_prompt",
    )(q_arr, k_arr, v_arr, sel, tiles)


def _bdot(a, b, dims=(((1,), (0,)), ((), ()))):
    return lax.dot_general(a.astype(BF16), b.astype(BF16), dims, preferred_element_type=F32)


_NT = (((1,), (1,)), ((), ()))
_TN = (((0,), (0,)), ((), ()))


def _rwkv_kernel(p_ref, shift_ref, s0_ref, mu_ref, w0_ref, w2_ref, a0_ref, a2_ref, g2_ref, kk_ref, ka_ref,
                 rk_ref, lng_ref, lnb_ref, y_ref, sout_ref, carry_ref, state_ref, *, c, t_valid, n_chunks):
    ci = pl.program_id(1)

    @pl.when(ci == 0)
    def _():
        carry_ref[...] = shift_ref[...]
        state_ref[...] = s0_ref[...]

    p = p_ref[0]
    row = lax.broadcasted_iota(jnp.int32, (c, 1), 0)
    prev = jnp.where(row == 0, carry_ref[...], pltpu.roll(p, 1, axis=0))
    carry_ref[...] = p[c - 1:c, :]
    xs = p + (prev - p) * mu_ref[...]
    o = 3 * C_B
    r = xs[:, :C_B]
    k = xs[:, C_B:2 * C_B]
    v = xs[:, 2 * C_B:o]
    wd = xs[:, o:o + LORA_W]
    ad = xs[:, o + LORA_W:o + LORA_W + LORA_A]
    gd = xs[:, o + LORA_W + LORA_A:]
    w_raw = w0_ref[...] + _bdot(jnp.tanh(wd), w2_ref[...])
    logd = -jnp.exp(-jax.nn.softplus(-w_raw) - 0.5)
    a = jax.nn.sigmoid(a0_ref[...] + _bdot(ad, a2_ref[...]))
    gate = _bdot(jax.nn.sigmoid(gd), g2_ref[...])
    kk = k * kk_ref[...]
    k = k * (1.0 + (a - 1.0) * ka_ref[...])
    if t_valid < c:
        valid = row < t_valid
        logd = jnp.where(valid, logd, 0.0)
        r = jnp.where(valid, r, 0.0)
        k = jnp.where(valid, k, 0.0)
        v = jnp.where(valid, v, 0.0)
        kk = jnp.where(valid, kk, 0.0)
    ti = lax.broadcasted_iota(jnp.int32, (c, c), 0)
    si = lax.broadcasted_iota(jnp.int32, (c, c), 1)
    incl = si <= ti
    strict = si < ti
    tri = jnp.where(incl, 1.0, 0.0).astype(BF16)
    hi = logd.astype(BF16)
    rem = logd - hi.astype(F32)
    mid = rem.astype(BF16)
    lo = (rem - mid.astype(F32)).astype(BF16)
    cs = (jnp.dot(tri, hi, preferred_element_type=F32) + jnp.dot(tri, mid, preferred_element_type=F32)
          + jnp.dot(tri, lo, preferred_element_type=F32))
    e_pos = jnp.exp(cs)
    e_prev = jnp.exp(cs - logd)
    e_neg = jnp.exp(-cs)
    rk = r * k * rk_ref[...]
    n_levels = int(math.log2(c))
    assert 2 ** n_levels == c
    heads = range(H_B)
    sls = [slice(h * HD_B, (h + 1) * HD_B) for h in heads]
    kkn = []
    for sl in sls:
        kk_h = kk[:, sl]
        kkn.append(kk_h / jnp.maximum(jnp.sqrt(jnp.sum(kk_h * kk_h, axis=-1, keepdims=True)), 1e-12))
    v_h = [v[:, sl].astype(BF16) for sl in sls]
    lhs = [jnp.concatenate([-kkn[h] * e_prev[:, sls[h]], r[:, sls[h]] * e_pos[:, sls[h]]], axis=0).astype(BF16)
           for h in heads]
    rhs = [jnp.concatenate([kkn[h] * a[:, sls[h]] * e_neg[:, sls[h]], k[:, sls[h]] * e_neg[:, sls[h]]],
                           axis=0).astype(BF16) for h in heads]
    s0 = [state_ref[h] for h in heads]
    mm = [_bdot(lhs[h], rhs[h], _NT) for h in heads]
    ars = [_bdot(lhs[h], s0[h], _NT) for h in heads]
    u = [ars[h][:c] + _bdot(jnp.where(strict, mm[h][:c, c:], 0.0), v_h[h]) for h in heads]
    lp = [jnp.where(strict, mm[h][:c, :c], 0.0) for h in heads]
    for lvl in range(n_levels):
        u = [u[h] + _bdot(lp[h], u[h]) for h in heads]
        if lvl < n_levels - 1:
            lp = [_bdot(lp[h], lp[h]) for h in heads]
    uv = [jnp.concatenate([u[h].astype(BF16), v_h[h]], axis=0) for h in heads]
    t2 = lax.broadcasted_iota(jnp.int32, (c, 2 * c), 0)
    s2 = lax.broadcasted_iota(jnp.int32, (c, 2 * c), 1)
    incl2 = jnp.where(s2 >= c, s2 - c, s2) <= t2
    y = [ars[h][c:] + _bdot(jnp.where(incl2, mm[h][c:], 0.0), uv[h]) for h in heads]
    for h in heads:
        state_ref[h] = (s0[h] + _bdot(uv[h], rhs[h], _TN)) * e_pos[c - 1:c, sls[h]]
    outs = []
    for h in heads:
        sl = sls[h]
        mean = jnp.mean(y[h], axis=-1, keepdims=True)
        var = jnp.mean(jnp.square(y[h] - mean), axis=-1, keepdims=True)
        yn = (y[h] - mean) * lax.rsqrt(var + GN_EPS) * lng_ref[:, sl] + lnb_ref[:, sl]
        bonus = jnp.sum(rk[:, sl], axis=-1, keepdims=True) * v[:, sl]
        outs.append((yn + bonus) * gate[:, sl])
    y_ref[...] = jnp.concatenate(outs, axis=-1).astype(y_ref.dtype)

    @pl.when(ci == n_chunks - 1)
    def _():
        sout_ref[...] = state_ref[...]


def rwkv_mix(p, shift0, s0, w, *, c, t_valid, col0=0):
    b, t, _ = p.shape
    n_chunks = t // c
    fix2 = lambda bb, ci: (0, 0)
    vec = lambda n: pl.BlockSpec((1, n), fix2)
    vmem = 6 * c * B_COLS * 4 + 4 * H_B * HD_B * HD_B * 4 + 40 * c * C_B * 4 + (LORA_W + LORA_A + LORA_G) * C_B * 4
    return pl.pallas_call(
        functools.partial(_rwkv_kernel, c=c, t_valid=t_valid, n_chunks=n_chunks),
        out_shape=(jax.ShapeDtypeStruct((b, t, C_B), BF16),
                   jax.ShapeDtypeStruct((b, H_B, HD_B, HD_B), F32)),
        grid=(b, n_chunks),
        in_specs=[pl.BlockSpec((pl.Element(1), pl.Element(c), pl.Element(B_COLS)),
                               lambda bb, ci: (bb, ci * c, col0)),
                  pl.BlockSpec((None, 1, B_COLS), lambda bb, ci: (bb, 0, 0)),
                  pl.BlockSpec((None, H_B, HD_B, HD_B), lambda bb, ci: (bb, 0, 0, 0)),
                  vec(B_COLS), vec(C_B), pl.BlockSpec((LORA_W, C_B), fix2),
                  vec(C_B), pl.BlockSpec((LORA_A, C_B), fix2), pl.BlockSpec((LORA_G, C_B), fix2),
                  vec(C_B), vec(C_B), vec(C_B), vec(C_B), vec(C_B)],
        out_specs=(pl.BlockSpec((None, c, C_B), lambda bb, ci: (bb, ci, 0)),
                   pl.BlockSpec((None, H_B, HD_B, HD_B), lambda bb, ci: (bb, 0, 0, 0))),
        scratch_shapes=[pltpu.VMEM((1, B_COLS), F32), pltpu.VMEM((H_B, HD_B, HD_B), F32)],
        compiler_params=_cparams(("parallel", "arbitrary"), vmem),
        name="rwkv_mix",
    )(p, shift0, s0, w["mu"], w["w0"], w["w2"], w["a0"], w["a2"], w["g2"], w["k_k"], w["k_a"],
      w["r_k"], w["ln_g"], w["ln_b"])


PAGE_CH = 2 * KVH_C
PAGE_ROWS = PAGE_SIZE * PAGE_CH


def _stream_pages(make_copies):
    step = pl.program_id(0)
    n_steps = pl.num_programs(0)
    slot = step % 2

    @pl.when(step == 0)
    def _():
        for cp in make_copies(step, slot):
            cp.start()

    @pl.when(step + 1 < n_steps)
    def _():
        for cp in make_copies(step + 1, 1 - slot):
            cp.start()

    for cp in make_copies(step, slot):
        cp.wait()
    return slot


BLOCK_ROWS = CMP_BLOCK * PAGE_CH


def _compress_kernel(table_ref, cache_ref, pe_ref, w1_ref, w2_ref, o_ref, buf_ref, sem_ref, *, pp):
    def make_copies(step, slot):
        out = []
        for k in range(pp):
            page = table_ref[step * pp + k]
            for n in range(2):
                out.append(pltpu.make_async_copy(cache_ref.at[page, pl.ds(n * BLOCK_ROWS, BLOCK_ROWS), :],
                                                 buf_ref.at[slot, :, 2 * k + n, :], sem_ref.at[slot]))
        return out

    slot = _stream_pages(make_copies)
    nr = 2 * pp
    for cc in range(2):
        cols = []
        for pos in range(CMP_BLOCK):
            q0 = pos * PAGE_CH + cc * KVH_C
            cols.append(jnp.concatenate([(buf_ref[slot, q0 + h] + pe_ref[q0 + h:q0 + h + 1, :]).astype(BF16)
                                         for h in range(KVH_C)], axis=0))
        flat = jnp.concatenate(cols, axis=1)
        acc = jnp.dot(flat, w1_ref[cc], preferred_element_type=F32)
        res = jnp.dot(jax.nn.gelu(acc).astype(BF16), w2_ref[cc], preferred_element_type=F32)
        for h in range(KVH_C):
            col = (cc * KVH_C + h) * HD_C
            o_ref[:, col:col + HD_C] = res[h * nr:(h + 1) * nr]


def nsa_compress(cache, table, pe, w1, w2, *, pp):
    n_pages = table.shape[0]
    assert n_pages % pp == 0
    nr = 2 * pp
    grid_spec = pltpu.PrefetchScalarGridSpec(
        num_scalar_prefetch=1,
        grid=(n_pages // pp,),
        in_specs=[pl.BlockSpec(memory_space=pl.ANY),
                  pl.BlockSpec((BLOCK_ROWS, HD_C), lambda s, tbl: (0, 0)),
                  pl.BlockSpec((2, CMP_BLOCK * HD_C, CMP_HIDDEN), lambda s, tbl: (0, 0, 0)),
                  pl.BlockSpec((2, CMP_HIDDEN, HD_C), lambda s, tbl: (0, 0, 0))],
        out_specs=pl.BlockSpec((nr, PAGE_CH * HD_C), lambda s, tbl: (s, 0)),
        scratch_shapes=[pltpu.VMEM((2, BLOCK_ROWS, nr, HD_C), F32), pltpu.SemaphoreType.DMA((2,))],
    )
    vmem = 2 * BLOCK_ROWS * nr * HD_C * 4 + 4 * CMP_BLOCK * HD_C * CMP_HIDDEN * 2 + 8 * nr * 512 * 4
    return pl.pallas_call(
        functools.partial(_compress_kernel, pp=pp),
        out_shape=jax.ShapeDtypeStruct((2 * n_pages, PAGE_CH * HD_C), F32),
        grid_spec=grid_spec,
        compiler_params=_cparams(("arbitrary",), vmem),
        name="nsa_compress",
    )(table, cache, pe, w1, w2)


def _nsa_sel_paged_kernel(table_ref, cache_ref, q_ref, sel_ref, bias_ref, knew_ref, vnew_ref, bnew_ref, o_ref,
                          buf_ref, sem_ref, m_ref, l_ref, acc_ref, *, pp, chunks, tq, g):
    hd = HD_C

    def make_copies(step, slot):
        return [pltpu.make_async_copy(cache_ref.at[table_ref[step * pp + k]], buf_ref.at[slot, k], sem_ref.at[slot])
                for k in range(pp)]

    slot = _stream_pages(make_copies)

    def page_rows(ch):
        return jnp.concatenate([buf_ref[slot, k, pl.ds(ch, PAGE_SIZE, stride=PAGE_CH), :] for k in range(pp)], axis=0)

    chunk = pl.program_id(0) % chunks
    nk = pp * PAGE_SIZE
    n_blk = sel_ref.shape[-1]
    scale = hd ** -0.5

    @pl.when(chunk == 0)
    def _():
        m_ref[...] = jnp.full(m_ref.shape, NEG_INF, F32)
        l_ref[...] = jnp.zeros(l_ref.shape, F32)
        acc_ref[...] = jnp.zeros(acc_ref.shape, F32)

    eb = lax.broadcasted_iota(jnp.int32, (n_blk, nk), 0)
    ek = lax.broadcasted_iota(jnp.int32, (n_blk, nk), 1)
    expand = jnp.where(eb == (ek >> SEL_SHIFT), 1.0, 0.0).astype(BF16)
    q = q_ref[...]

    groups = range(KVH_C)
    q8 = [(_stack_heads(q, h * g, g, hd) * scale).astype(BF16) for h in groups]

    def update(lf, vv):
        m_old = [m_ref[h] for h in groups]
        m_new = [jnp.maximum(m_old[h], jnp.max(lf[h], axis=-1, keepdims=True)) for h in groups]
        e = [jnp.exp(lf[h] - m_new[h]) for h in groups]
        alpha = [jnp.exp(m_old[h] - m_new[h]) for h in groups]
        pv = [jnp.dot(e[h].reshape(g * tq, -1).astype(BF16), vv[h], preferred_element_type=F32) for h in groups]
        for h in groups:
            l_ref[h] = alpha[h] * l_ref[h] + jnp.sum(e[h], axis=-1, keepdims=True)
            acc_ref[h] = alpha[h].reshape(g * tq, 1) * acc_ref[h] + pv[h]
            m_ref[h] = m_new[h]

    kk = [page_rows(h).astype(BF16) for h in groups]
    vv = [page_rows(KVH_C + h).astype(BF16) for h in groups]
    selk = [jnp.dot(sel_ref[h].astype(BF16), expand, preferred_element_type=F32) for h in groups]
    lf = [lax.dot_general(q8[h], kk[h], _NT, preferred_element_type=F32).reshape(g, tq, nk)
          + jnp.where((selk[h] > 0.5)[None], bias_ref[h * g:(h + 1) * g], NEG_INF) for h in groups]
    update(lf, vv)

    @pl.when(chunk == chunks - 1)
    def _():
        r = lax.broadcasted_iota(jnp.int32, (tq, tq), 0)
        c = lax.broadcasted_iota(jnp.int32, (tq, tq), 1)
        kn = [knew_ref[:, h * hd:(h + 1) * hd].astype(BF16) for h in groups]
        vn = [vnew_ref[:, h * hd:(h + 1) * hd].astype(BF16) for h in groups]
        lf_new = [lax.dot_general(q8[h], kn[h], _NT, preferred_element_type=F32).reshape(g, tq, tq)
                  + jnp.where((c <= r)[None], bnew_ref[h * g:(h + 1) * g], NEG_INF) for h in groups]
        update(lf_new, vn)
        for h in groups:
            o = acc_ref[h] / jnp.maximum(l_ref[h].reshape(g * tq, 1), 1e-30)
            for j in range(g):
                col = (h * g + j) * hd
                o_ref[:, col:col + hd] = o[j * tq:(j + 1) * tq]


def nsa_sel_paged(q_arr, cache, table, sel, bias, k_new, v_new, bias_new, *, pp, tq):
    b = q_arr.shape[0]
    n_pages = table.shape[0] // b
    chunks = n_pages // pp
    nk = pp * PAGE_SIZE
    n_blk = nk // SEL_BLOCK
    kvw = KVH_C * HD_C
    sel = sel.reshape(b, KVH_C, tq, chunks, n_blk).transpose(0, 3, 1, 2, 4)
    grid_spec = pltpu.PrefetchScalarGridSpec(
        num_scalar_prefetch=1,
        grid=(b * chunks,),
        in_specs=[pl.BlockSpec(memory_space=pl.ANY),
                  pl.BlockSpec((None, tq, D_C), lambda s, tbl: (s // chunks, 0, 0)),
                  pl.BlockSpec((None, None, KVH_C, tq, n_blk), lambda s, tbl: (s // chunks, s % chunks, 0, 0, 0)),
                  pl.BlockSpec((H_C, tq, nk), lambda s, tbl: (0, 0, s % chunks)),
                  pl.BlockSpec((None, tq, kvw), lambda s, tbl: (s // chunks, 0, 0)),
                  pl.BlockSpec((None, tq, kvw), lambda s, tbl: (s // chunks, 0, 0)),
                  pl.BlockSpec((H_C, tq, tq), lambda s, tbl: (0, 0, 0))],
        out_specs=pl.BlockSpec((None, tq, D_C), lambda s, tbl: (s // chunks, 0, 0)),
        scratch_shapes=[pltpu.VMEM((2, pp, PAGE_ROWS, HD_C), F32), pltpu.SemaphoreType.DMA((2,)),
                        pltpu.VMEM((KVH_C, G_C, tq, 1), F32), pltpu.VMEM((KVH_C, G_C, tq, 1), F32),
                        pltpu.VMEM((KVH_C, G_C * tq, HD_C), F32)],
    )
    vmem = 2 * pp * PAGE_SIZE * 2 * kvw * 4 + 2 * H_C * tq * nk * 4 + 10 * G_C * tq * nk * 4 + n_blk * nk * 4
    return pl.pallas_call(
        functools.partial(_nsa_sel_paged_kernel, pp=pp, chunks=chunks, tq=tq, g=G_C),
        out_shape=jax.ShapeDtypeStruct((b, tq, D_C), F32),
        grid_spec=grid_spec,
        compiler_params=_cparams(("arbitrary",), vmem),
        name="nsa_sel_paged",
    )(table, cache, q_arr, sel, bias, k_new, v_new, bias_new)


def _nsa_combine_kernel(gl_ref, gb_ref, oc_ref, os_ref, ow_ref, o_ref):
    gates = jax.nn.sigmoid(gl_ref[...] + gb_ref[...])
    for h in range(H_C):
        sl = slice(h * HD_C, (h + 1) * HD_C)
        o_ref[:, sl] = (gates[:, h:h + 1] * oc_ref[:, sl] + gates[:, H_C + h:H_C + h + 1] * os_ref[:, sl]
                        + gates[:, 2 * H_C + h:2 * H_C + h + 1] * ow_ref[:, sl]).astype(o_ref.dtype)


def nsa_combine(gate_logits, gate_bias, o_c, o_s, o_w):
    m = o_c.shape[0]
    tm = _row_tile(m, 512)
    row = lambda i: (i, 0)
    vmem = 2 * tm * (128 + 3 * D_C) * 4 + 2 * tm * D_C * 2
    return pl.pallas_call(
        _nsa_combine_kernel,
        out_shape=jax.ShapeDtypeStruct((m, D_C), BF16),
        grid=(m // tm,),
        in_specs=[pl.BlockSpec((tm, 128), row), pl.BlockSpec((1, 128), lambda i: (0, 0)),
                  pl.BlockSpec((tm, D_C), row), pl.BlockSpec((tm, D_C), row), pl.BlockSpec((tm, D_C), row)],
        out_specs=pl.BlockSpec((tm, D_C), row),
        compiler_params=_cparams(("parallel",), vmem),
        name="nsa_combine",
    )(gate_logits, gate_bias, o_c, o_s, o_w)


CMP_PAGES_PER_STEP = 32
SEL_PAGES_PER_STEP = 32
Q_PAD = V7X_SUBLANES


def _pad_rows(a, rows, front=0):
    return jnp.pad(a, ((0, 0), (front, rows - a.shape[1] - front), (0, 0)))


def _memo(W, fn, *args):
    key = (fn.__name__,) + args
    if key not in W["tables"]:
        W["tables"][key] = fn(W["rel_bias"], *args)
    return W["tables"][key]


def _past_bias(rel_bias, tq, past_len):
    dist = past_len + jnp.arange(tq, dtype=jnp.int32)[:, None] - jnp.arange(past_len, dtype=jnp.int32)[None, :]
    return _bias_lookup(rel_bias, dist).transpose(2, 0, 1)


def _mixer_ab(x2, b, t, W, l, i, st):
    prompt = st is None
    proj3 = mm_norm(x2, W["mix_norm"][i], W["ab_w_in"][l], W["ab_bias"][l], tn=1536).reshape(b, t, AB_COLS)
    kv_new = proj3[:, :, D_A:A_COLS]
    sinks = W["swa_sinks"][l].reshape(H_A, 1, 1)
    if prompt:
        nw = -(-WIN_A // QBLK)
        sk = (nw + 1) * QBLK
        bias = _memo(W, _toeplitz_bias, QBLK, sk, nw * QBLK)
        o_a = band_attn(proj3, _pad_rows(kv_new, t + nw * QBLK, nw * QBLK),
                        bias, sinks, tq=QBLK, sk=sk, hd=HD_A, n_kvh_step=KVH_A, g=G_A, window=WIN_A,
                        delta=nw * QBLK, kpos_base=-nw * QBLK, kstride=QBLK, out_dtype=BF16)
        buf = kv_new[:, t - min(WIN_A, t):]
        y_b, s_new = rwkv_mix(proj3, jnp.zeros((b, 1, B_COLS), F32), jnp.zeros((b, H_B, HD_B, HD_B), F32),
                              W["rwkv"][l], c=RWKV_CHUNK, t_valid=RWKV_CHUNK, col0=A_COLS)
    else:
        past_len = st["past_len"]
        old = st["swa"][l].reshape(b, -1, 2 * KVH_A * HD_A)
        wb = old.shape[1]
        kv_all = jnp.concatenate([old, kv_new], axis=1)
        buf = kv_all[:, t:]
        sk = -(-(wb + t) // V7X_LANES) * V7X_LANES
        bias = _memo(W, _toeplitz_bias, Q_PAD, sk, wb)
        proj_pad = _pad_rows(proj3, Q_PAD)
        o_a = band_attn(proj_pad, _pad_rows(kv_all, sk),
                        bias, sinks, tq=Q_PAD, sk=sk, hd=HD_A, n_kvh_step=KVH_A, g=G_A, window=WIN_A,
                        delta=wb, kpos_base=past_len - wb, kstride=0, out_dtype=BF16)[:, :t]
        y_b, s_new = rwkv_mix(proj_pad, st["shift"][l][:, None], st["wkv"][l],
                              W["rwkv"][l], c=Q_PAD, t_valid=t, col0=A_COLS)
        y_b = y_b[:, :t]
    mix_in = jnp.concatenate([o_a, y_b], axis=-1).reshape(b * t, D_MODEL)
    x2 = mm_res(mix_in, W["ab_w_out"][l], x2)
    buf = buf.reshape(b, -1, 2, KVH_A, HD_A)
    return x2, buf, proj3[:, -1, A_COLS:], s_new


def _mixer_c(x2, b, t, W, l, i, st):
    prompt = st is None
    kvw = KVH_C * HD_C
    proj3 = mm_norm(x2, W["mix_norm"][i], W["c_w_in"][l], W["c_zero_bias"], tn=1024).reshape(b, t, C_COLS_PAD)
    kv_cmp_new = proj3[:, :, D_C:D_C + 2 * kvw]
    kv_sel_new = proj3[:, :, D_C + 2 * kvw:D_C + 4 * kvw]
    kv_win_new = proj3[:, :, D_C + 4 * kvw:D_C + 6 * kvw]
    gate_logits = proj3[:, :, D_C + C_KV_COLS:D_C + C_KV_COLS + V7X_LANES].reshape(b * t, V7X_LANES)
    pe, w1, w2 = W["cmp_pe"][l], W["cmp_w1"][l], W["cmp_w2"][l]
    if prompt:
        n_pages = b * t // PAGE_SIZE
        kv_c = nsa_compress(kv_cmp_new.reshape(n_pages, PAGE_ROWS, HD_C), jnp.arange(n_pages, dtype=jnp.int32),
                            pe, w1, w2, pp=min(CMP_PAGES_PER_STEP, n_pages)).reshape(b, t // CMP_BLOCK, 2 * kvw)
        o_c, sel = nsa_cmp(proj3, kv_c, tq=QBLK, q0=0)
        sel_col = (D_C + 2 * kvw) // HD_C
        o_s = nsa_sel_prompt(proj3, proj3, proj3, sel, _memo(W, sel_bias_tiles, QBLK, t), tq=QBLK,
                             k_blk0=sel_col, v_blk0=sel_col + KVH_C)
        nw = -(-WIN_C // QBLK)
        sk = (nw + 1) * QBLK
        bias = _memo(W, _toeplitz_bias, QBLK, sk, nw * QBLK)
        o_w = band_attn(proj3, _pad_rows(kv_win_new, t + nw * QBLK, nw * QBLK), bias, None,
                        tq=QBLK, sk=sk, hd=HD_C, n_kvh_step=1, g=G_C, window=WIN_C,
                        delta=nw * QBLK, kpos_base=-nw * QBLK, kstride=QBLK, out_dtype=F32)
        win_buf = kv_win_new[:, t - min(WIN_C, t):]
    else:
        past_len = st["past_len"]
        n_phys = st["cmp"].shape[1]
        table = st["page_table"].reshape(-1) + l * n_phys
        kv_c = nsa_compress(st["cmp"].reshape(-1, PAGE_ROWS, HD_C), table, pe, w1, w2,
                            pp=CMP_PAGES_PER_STEP).reshape(b, past_len // CMP_BLOCK, 2 * kvw)
        q8 = _pad_rows(proj3, Q_PAD)
        o_c, sel = nsa_cmp(q8, kv_c, tq=Q_PAD, q0=past_len)
        bias_past = _memo(W, _past_bias, Q_PAD, past_len)
        bias_new = _memo(W, _toeplitz_bias, Q_PAD, Q_PAD, 0)
        sel_new = _pad_rows(kv_sel_new, Q_PAD)
        o_s = nsa_sel_paged(q8, st["sel"].reshape(-1, PAGE_ROWS, HD_C), table, sel, bias_past,
                            sel_new[:, :, :kvw], sel_new[:, :, kvw:], bias_new, pp=SEL_PAGES_PER_STEP, tq=Q_PAD)
        old = st["win"][l].reshape(b, -1, 2 * kvw)
        wb = old.shape[1]
        kv_all = jnp.concatenate([old, kv_win_new], axis=1)
        win_buf = kv_all[:, t:]
        sk = -(-(wb + t) // V7X_LANES) * V7X_LANES
        bias = _memo(W, _toeplitz_bias, Q_PAD, sk, wb)
        o_w = band_attn(q8, _pad_rows(kv_all, sk), bias, None, tq=Q_PAD, sk=sk, hd=HD_C,
                        n_kvh_step=1, g=G_C, window=WIN_C, delta=wb, kpos_base=past_len - wb, kstride=0,
                        out_dtype=F32)
        o_c, o_s, o_w = o_c[:, :t], o_s[:, :t], o_w[:, :t]
    comb = nsa_combine(gate_logits, W["c_gate_b"][l], o_c.reshape(b * t, D_C), o_s.reshape(b * t, D_C),
                       o_w.reshape(b * t, D_C))
    x2 = mm_res(comb, W["c_w_out"][l], x2)
    shape5 = lambda a: a.reshape(b, -1, 2, KVH_C, HD_C)
    return x2, shape5(kv_cmp_new), shape5(kv_sel_new), shape5(win_buf)


def _trunk(x, p, W, st):
    b, t, _ = x.shape
    x2 = x.reshape(b * t, D_MODEL)
    swa_l, shift_l, wkv_l, cmp_l, sel_l, win_l = [], [], [], [], [], []
    for i in range(DEPTH):
        l = i // 2
        if i % 2 == 0:
            x2, buf, shift_new, s_new = _mixer_ab(x2, b, t, W, l, i, st)
            swa_l.append(buf)
            shift_l.append(shift_new)
            wkv_l.append(s_new)
        else:
            x2, cmp_new, sel_new, win_buf = _mixer_c(x2, b, t, W, l, i, st)
            cmp_l.append(cmp_new)
            sel_l.append(sel_new)
            win_l.append(win_buf)
        act = ffn_up(x2, W["ffn_norm"][i], W["ffn_w_gate"][i], W["ffn_w_up"][i])
        x2 = mm_res(act, W["ffn_w_down"][i], x2, tk=D_FF // 2)
        x2 = ple(x2, p[i].reshape(b * t, PLE_DIM), W["ple_gate_norm"][i], W["ple_w_gate"][i], W["ple_w_proj"][i],
                 W["ple_post_norm"][i], W["final_norm"], final=(i == DEPTH - 1))
    y = x2.reshape(b, t, D_MODEL)
    return (y, jnp.stack(swa_l), jnp.stack(shift_l), jnp.stack(wkv_l), jnp.stack(cmp_l), jnp.stack(sel_l),
            jnp.stack(win_l))


def kernel(x_prompt, x_sample, state_swa_kv, state_rwkv_shift, state_rwkv_wkv, cache_nsa_cmp_kv, cache_nsa_sel_kv, state_nsa_win_kv, page_table, p_prompt, p_sample, rel_bias, mix_norm, ab_w_in, ab_b_qkv, swa_sinks, rwkv_mu, rwkv_w0, rwkv_w2, rwkv_a0, rwkv_a2, rwkv_g2, rwkv_k_k, rwkv_k_a, rwkv_r_k, rwkv_ln_g, rwkv_ln_b, ab_w_out, c_w_in, c_gate_b, nsa_cmp_pos, nsa_cmp_w1, nsa_cmp_w2, c_w_out, ffn_norm, ffn_w_gate, ffn_w_up, ffn_w_down, ple_w_proj, ple_gate_norm, ple_w_gate, ple_post_norm, final_norm):
    n_ab, n_c = ab_w_in.shape[0], c_w_in.shape[0]
    bf = lambda a: a.astype(BF16)
    row = lambda a: a.reshape(a.shape[0], 1, -1).astype(F32)
    pe = jnp.broadcast_to(nsa_cmp_pos.transpose(0, 2, 1, 3)[:, :, :, None, :],
                          (n_c, CMP_BLOCK, 2, KVH_C, HD_C)).reshape(n_c, BLOCK_ROWS, HD_C)
    W = dict(
        tables={}, rel_bias=rel_bias,mix_norm=row(mix_norm), ab_w_in=bf(ab_w_in),
        ab_bias=jnp.pad(ab_b_qkv, ((0, 0), (0, AB_COLS - A_COLS))).reshape(n_ab, 1, AB_COLS),
        swa_sinks=swa_sinks, ab_w_out=bf(ab_w_out),
        rwkv=[dict(mu=rwkv_mu[l][None], w0=rwkv_w0[l][None], w2=bf(rwkv_w2[l]), a0=rwkv_a0[l][None], a2=bf(rwkv_a2[l]),
                   g2=bf(rwkv_g2[l]), k_k=rwkv_k_k[l][None], k_a=rwkv_k_a[l][None], r_k=rwkv_r_k[l].reshape(1, C_B),
                   ln_g=rwkv_ln_g[l][None], ln_b=rwkv_ln_b[l][None]) for l in range(n_ab)],
        c_w_in=bf(jnp.pad(c_w_in, ((0, 0), (0, 0), (0, C_COLS_PAD - C_COLS)))),
        c_zero_bias=jnp.zeros((1, C_COLS_PAD), F32),
        c_gate_b=jnp.pad(c_gate_b, ((0, 0), (0, V7X_LANES - 3 * H_C))).reshape(n_c, 1, V7X_LANES),
        cmp_pe=pe, cmp_w1=bf(nsa_cmp_w1), cmp_w2=bf(nsa_cmp_w2), c_w_out=bf(c_w_out),
        ffn_norm=row(ffn_norm), ffn_w_gate=bf(ffn_w_gate), ffn_w_up=bf(ffn_w_up), ffn_w_down=bf(ffn_w_down),
        ple_w_proj=bf(ple_w_proj), ple_gate_norm=row(ple_gate_norm), ple_w_gate=bf(ple_w_gate),
        ple_post_norm=row(ple_post_norm), final_norm=final_norm.reshape(1, D_MODEL),
    )
    st = dict(swa=state_swa_kv, shift=state_rwkv_shift, wkv=state_rwkv_wkv, cmp=cache_nsa_cmp_kv,
              sel=cache_nsa_sel_kv, win=state_nsa_win_kv, page_table=page_table,
              past_len=page_table.shape[1] * PAGE_SIZE)
    y_p, swa_p, shift_p, wkv_p, cmp_p, sel_p, win_p = _trunk(x_prompt, p_prompt, W, None)
    y_s, swa_s, shift_s, wkv_s, cmp_s, sel_s, win_s = _trunk(x_sample, p_sample, W, st)
    return (y_p, y_s, swa_p, swa_s, shift_p, shift_s, wkv_p, wkv_s, cmp_p, cmp_s, sel_p, sel_s, win_p, win_s)
```
